```python
import math
import jax
import jax.numpy as jnp
from jax import lax
import numpy as np

D_MODEL = 1024
BATCH = 2
SEQ = 8192
DEPTH = 2

CTX_LEN = 256
GRID_W = 64
CONV_W = 3
EPS = 1e-6
F32 = jnp.float32

HG_HEADS = 4
HG_DK = 64
HG_DV = 64
HG_WIDTH = HG_HEADS * HG_DV
HG_CHUNK = 32

HY_WIDTH = 256
HY_ORDER = 2
HY_EMB_BANDS = 16
HY_EMB_DIM = 1 + 2 * HY_EMB_BANDS
HY_FILTER_HIDDEN = 64
HY_DECAY_TARGET = 1e-2
HY_FAST_DECAY_PCT = 0.3
HY_SLOW_DECAY_PCT = 1.5
HY_MIN_DECAY = math.log(HY_DECAY_TARGET) / HY_SLOW_DECAY_PCT
HY_MAX_DECAY = math.log(HY_DECAY_TARGET) / HY_FAST_DECAY_PCT

SSD_HEADS = 8
SSD_HEAD_DIM = 64
SSD_WIDTH = SSD_HEADS * SSD_HEAD_DIM
SSD_GROUPS = 2
SSD_STATE = 128
SSD_XBC = SSD_WIDTH + 2 * SSD_GROUPS * SSD_STATE
SSD_CHUNK = 64

MIX_WIDTH = HG_WIDTH + HY_WIDTH + SSD_WIDTH
HG_COLS = 5 * HG_WIDTH
HY_COLS = (HY_ORDER + 1) * HY_WIDTH
SSD_COLS = SSD_WIDTH + SSD_XBC + 2 * SSD_HEADS
IN_COLS = HG_COLS + HY_COLS + SSD_COLS

D_FF = 2816

kernel_name = "parallel_hybrid_flow_backbone"


def rms_norm(x, g):
    xf = x.astype(F32)
    y = xf * lax.rsqrt(jnp.mean(xf * xf, axis=-1, keepdims=True) + EPS)
    return (y * g.astype(F32)).astype(x.dtype)


def group_rms_norm(x, g, groups):
    shp = x.shape
    xf = x.astype(F32).reshape(shp[:-1] + (groups, shp[-1] // groups))
    y = xf * lax.rsqrt(jnp.mean(xf * xf, axis=-1, keepdims=True) + EPS)
    return y.reshape(shp) * g.astype(F32)


def modulate(h, shift, scale):
    return h * (1.0 + scale) + shift


def heads(t, n):
    return t.reshape(t.shape[:-1] + (n, t.shape[-1] // n))


def to_chunks(t, size):
    return t.reshape((t.shape[0], t.shape[1] // size, size) + t.shape[2:])


def dwconv1d(u, w, b):
    out = lax.conv_general_dilated(u, w[:, None, :], window_strides=(1,), padding="SAME",
                                   dimension_numbers=("NWC", "WIO", "NWC"),
                                   feature_group_count=u.shape[-1])
    return out + b


def dwconv2d(u, w, b):
    out = lax.conv_general_dilated(u, w[:, :, None, :], window_strides=(1, 1), padding="SAME",
                                   dimension_numbers=("NHWC", "HWIO", "NHWC"),
                                   feature_group_count=u.shape[-1])
    return out + b


def chunk_scan(decay, update, s0):
    def step(s, inp):
        d, u = inp
        return d * s + u, s
    s_fin, s_prev = lax.scan(step, s0, (jnp.moveaxis(decay, 1, 0), jnp.moveaxis(update, 1, 0)))
    return jnp.moveaxis(s_prev, 0, 1), s_fin


def gla_direction(q, f_logit, v, lb, s0, reverse, with_output):
    if reverse:
        q, f_logit, v = jnp.flip(q, 1), jnp.flip(f_logit, 1), jnp.flip(v, 1)
    logf = jnp.logaddexp(jnp.log(lb), jnp.log1p(-lb) + jax.nn.log_sigmoid(f_logit.astype(F32)))
    k = 1.0 - jnp.exp(logf)
    b = jnp.cumsum(to_chunks(logf, HG_CHUNK), axis=2)
    kc = to_chunks(k, HG_CHUNK)
    vc = to_chunks(v.astype(F32), HG_CHUNK)
    b_end = b[:, :, -1:]
    upd = jnp.einsum("bcshk,bcshv->bchkv", kc * jnp.exp(b_end - b), vc)
    s_prev, s_fin = chunk_scan(jnp.exp(b_end[:, :, 0])[..., None], upd, s0)
    if not with_output:
        return None, s_fin
    qc = to_chunks(q.astype(F32), HG_CHUNK)
    b_mid = b[:, :, HG_CHUNK // 2 - 1:HG_CHUNK // 2]
    scores = jnp.einsum("bcthk,bcshk->bchts", qc * jnp.exp(b - b_mid), kc * jnp.exp(b_mid - b))
    lower_tri = jnp.tril(jnp.ones((HG_CHUNK, HG_CHUNK), dtype=bool))
    scores = jnp.where(lower_tri, scores, 0.0)
    o = (jnp.einsum("bchts,bcshv->bcthv", scores, vc)
         + jnp.einsum("bcthk,bchkv->bcthv", qc * jnp.exp(b), s_prev))
    o = o.reshape(v.shape)
    if reverse:
        o = jnp.flip(o, 1)
    return o, s_fin


def ssd_direction(x, dt, a, bm, cm, s0, reverse, with_output):
    if reverse:
        x, dt, bm, cm = jnp.flip(x, 1), jnp.flip(dt, 1), jnp.flip(bm, 1), jnp.flip(cm, 1)
    bsz, seq_len, n_heads, head_dim = x.shape
    groups = bm.shape[2]
    hpg = n_heads // groups
    xc = to_chunks(x.astype(F32).reshape(bsz, seq_len, groups, hpg, head_dim), SSD_CHUNK)
    dtc = to_chunks(dt.astype(F32).reshape(bsz, seq_len, groups, hpg), SSD_CHUNK)
    bc = to_chunks(bm.astype(F32), SSD_CHUNK)
    cc = to_chunks(cm.astype(F32), SSD_CHUNK)
    a_cs = jnp.cumsum(dtc * a.reshape(groups, hpg), axis=2)
    xdt = xc * dtc[..., None]
    a_end = a_cs[:, :, -1:]
    upd = jnp.einsum("bcsgn,bcsgh,bcsghp->bcghpn", bc, jnp.exp(a_end - a_cs), xdt)
    s_prev, s_fin = chunk_scan(jnp.exp(a_end[:, :, 0])[..., None, None], upd, s0)
    if not with_output:
        return None, s_fin
    lower_tri = jnp.tril(jnp.ones((SSD_CHUNK, SSD_CHUNK), dtype=bool))
    seg = a_cs[:, :, :, None] - a_cs[:, :, None]
    decay = jnp.exp(jnp.where(lower_tri[:, :, None, None], seg, -jnp.inf))
    cb = jnp.einsum("bctgn,bcsgn->bctsg", cc, bc)
    y = jnp.einsum("bctsgh,bcsghp->bctghp", cb[..., None] * decay, xdt)
    y = y + jnp.einsum("bctgn,bcghpn,bctgh->bctghp", cc, s_prev, jnp.exp(a_cs))
    y = y.reshape(bsz, seq_len, n_heads, head_dim)
    if reverse:
        y = jnp.flip(y, 1)
    return y, s_fin


def hgrn2_branch(u, lb, norm_g, s0_f, s0_b, with_output):
    q, f_fwd, f_bwd, v, g = jnp.split(u, 5, axis=-1)
    q = heads(q, HG_HEADS) * (HG_DK ** -0.5)
    v = heads(v, HG_HEADS)
    o_f, s_f = gla_direction(q, heads(f_fwd, HG_HEADS), v, lb, s0_f, False, with_output)
    o_b, s_b = gla_direction(q, heads(f_bwd, HG_HEADS), v, lb, s0_b, True, with_output)
    if not with_output:
        return None, s_f, s_b
    o = (o_f + o_b).reshape(u.shape[:-1] + (HG_WIDTH,))
    o = group_rms_norm(o, norm_g, HG_HEADS) * jax.nn.silu(g.astype(F32))
    return o.astype(u.dtype), s_f, s_b


def hyena_filter_spectrum(seq_len, w1, b1, freq1, w2, b2, freq2, w3):
    t = jnp.linspace(0.0, 1.0, seq_len, dtype=F32)[:, None]
    w = 2.0 * math.pi * jnp.arange(seq_len, dtype=F32)[:, None] / seq_len
    bands = jnp.linspace(1e-4, HY_EMB_BANDS - 1, HY_EMB_BANDS, dtype=F32)[None]
    z = jnp.concatenate([t, jnp.cos(bands * w), -jnp.sin(bands * w)], axis=-1)
    h = jnp.sin(freq1.astype(F32) * (z @ w1.astype(F32) + b1.astype(F32)))
    h = jnp.sin(freq2.astype(F32) * (h @ w2.astype(F32) + b2.astype(F32)))
    h = (h @ w3.astype(F32)).reshape(seq_len, 2, HY_ORDER, HY_WIDTH)
    deltas = jnp.abs(jnp.linspace(HY_MIN_DECAY, HY_MAX_DECAY, HY_WIDTH, dtype=F32))
    h = h * jnp.exp(-t[:, :, None, None] * deltas)
    buf = jnp.concatenate([h[:, 0], jnp.zeros((1, HY_ORDER, HY_WIDTH), F32), h[:0:-1, 1]], axis=0)
    buf = buf / jnp.sum(jnp.abs(buf), axis=0, keepdims=True)
    return jnp.fft.rfft(buf, axis=0)


def fft_long_conv(u, spec, bias):
    seq_len = u.shape[1]
    uf = u.astype(F32)
    y = jnp.fft.irfft(jnp.fft.rfft(uf, n=2 * seq_len, axis=1) * spec, n=2 * seq_len, axis=1)[:, :seq_len]
    return y + uf * bias.astype(F32)


def hyena_branch(u, p):
    zc = dwconv1d(u, p["hy_conv_w"], p["hy_conv_b"])
    v, x1, x2 = jnp.split(zc, 3, axis=-1)
    spec = hyena_filter_spectrum(u.shape[1], p["hy_w1"], p["hy_b1"], p["hy_freq1"],
                                 p["hy_w2"], p["hy_b2"], p["hy_freq2"], p["hy_w3"])
    y = v
    for n, gate in enumerate((x1, x2)):
        y = gate.astype(F32) * fft_long_conv(y, spec[:, n], p["hy_bias"][n])
    return y.astype(u.dtype)


def ssd_branch(u, p, s0_f, s0_b, with_output):
    z, xbc, dt_raw = jnp.split(u, [SSD_WIDTH, SSD_WIDTH + SSD_XBC], axis=-1)
    xbc = jax.nn.silu(dwconv1d(xbc, p["ssd_conv_w"], p["ssd_conv_b"]))
    xs, bm, cm = jnp.split(xbc, [SSD_WIDTH, SSD_WIDTH + SSD_GROUPS * SSD_STATE], axis=-1)
    xs = heads(xs, SSD_HEADS)
    bm = heads(bm, SSD_GROUPS)
    cm = heads(cm, SSD_GROUPS)
    dt = jax.nn.softplus(heads(dt_raw.astype(F32), 2) + p["ssd_dt_bias"].astype(F32))
    a = -jnp.exp(p["ssd_a_log"].astype(F32))
    y_f, s_f = ssd_direction(xs, dt[..., 0, :], a[0], bm, cm, s0_f, False, with_output)
    y_b, s_b = ssd_direction(xs, dt[..., 1, :], a[1], bm, cm, s0_b, True, with_output)
    if not with_output:
        return None, s_f, s_b
    y = y_f + y_b + p["ssd_d"].astype(F32)[:, None] * xs.astype(F32)
    y = y.reshape(u.shape[:-1] + (SSD_WIDTH,)) * jax.nn.silu(z.astype(F32))
    return group_rms_norm(y, p["ssd_norm_g"], SSD_GROUPS).astype(u.dtype), s_f, s_b


def token_mixers(hx, hc, p, lb, ctx_out):
    bsz = hx.shape[0]
    ux = hx @ p["w_in"]
    uc = hc @ p["w_in"]
    ux_hg, ux_hy, ux_ssd = jnp.split(ux, [HG_COLS, HG_COLS + HY_COLS], axis=-1)
    uc_hg, uc_hy, uc_ssd = jnp.split(uc, [HG_COLS, HG_COLS + HY_COLS], axis=-1)
    zero_hg = jnp.zeros((bsz, HG_HEADS, HG_DK, HG_DV), F32)
    o_hg_c, s_hf, s_hb = hgrn2_branch(uc_hg, lb, p["hg_norm_g"], zero_hg, zero_hg, ctx_out)
    o_hg_x, _, _ = hgrn2_branch(ux_hg, lb, p["hg_norm_g"], s_hf, s_hb, True)
    zero_ssd = jnp.zeros((bsz, SSD_GROUPS, SSD_HEADS // SSD_GROUPS, SSD_HEAD_DIM, SSD_STATE), F32)
    o_ssd_c, s_sf, s_sb = ssd_branch(uc_ssd, p, zero_ssd, zero_ssd, ctx_out)
    o_ssd_x, _, _ = ssd_branch(ux_ssd, p, s_sf, s_sb, True)
    o_hy_x = hyena_branch(ux_hy, p)
    yx = jnp.concatenate([o_hg_x, o_hy_x, o_ssd_x], axis=-1) @ p["w_out"]
    if not ctx_out:
        return yx, None
    o_hy_c = hyena_branch(uc_hy, p)
    yc = jnp.concatenate([o_hg_c, o_hy_c, o_ssd_c], axis=-1) @ p["w_out"]
    return yx, yc


def conv_ffn(h, p, rows):
    bsz, seq_len, _ = h.shape
    a = h @ p["ffn_w_gate"]
    a = dwconv2d(a.reshape(bsz, rows, seq_len // rows, D_FF), p["ffn_conv_w"], p["ffn_conv_b"])
    a = a.reshape(bsz, seq_len, D_FF)
    return (jax.nn.silu(a) * (h @ p["ffn_w_up"])) @ p["ffn_w_down"]


def setup_inputs(seed: int = 0) -> dict:
    key = jax.random.key(seed)
    ks = iter(jax.random.split(key, 48))

    def nrm(shape, scale):
        return scale * jax.random.normal(next(ks), shape, F32)

    dt0 = jnp.exp(jax.random.uniform(next(ks), (DEPTH, 2, SSD_HEADS), F32, math.log(1e-3), math.log(1e-1)))
    return {
        "x": nrm((BATCH, SEQ, D_MODEL), 1.0),
        "c": nrm((BATCH, D_MODEL), 1.0),
        "ctx": nrm((BATCH, CTX_LEN, D_MODEL), 1.0),
        "c_ctx": nrm((D_MODEL,), 1.0),
        "w_ada": nrm((DEPTH, D_MODEL, 6 * D_MODEL), 0.5 * D_MODEL ** -0.5),
        "b_ada": nrm((DEPTH, 6 * D_MODEL), 0.02),
        "norm1_g": 1.0 + nrm((DEPTH, D_MODEL), 0.05),
        "norm2_g": 1.0 + nrm((DEPTH, D_MODEL), 0.05),
        "w_in": nrm((DEPTH, D_MODEL, IN_COLS), D_MODEL ** -0.5),
        "w_out": nrm((DEPTH, MIX_WIDTH, D_MODEL), MIX_WIDTH ** -0.5),
        "hg_lb_logits": nrm((DEPTH, HG_HEADS * HG_DK), 1.0),
        "hg_norm_g": 1.0 + nrm((DEPTH, HG_WIDTH), 0.05),
        "hy_conv_w": nrm((DEPTH, CONV_W, HY_COLS), CONV_W ** -0.5),
        "hy_conv_b": nrm((DEPTH, HY_COLS), 0.02),
        "hy_w1": nrm((DEPTH, HY_EMB_DIM, HY_FILTER_HIDDEN), HY_EMB_DIM ** -0.5),
        "hy_b1": nrm((DEPTH, HY_FILTER_HIDDEN), 0.1),
        "hy_freq1": 1.0 + nrm((DEPTH, HY_FILTER_HIDDEN), 0.05),
        "hy_w2": nrm((DEPTH, HY_FILTER_HIDDEN, HY_FILTER_HIDDEN), HY_FILTER_HIDDEN ** -0.5),
        "hy_b2": nrm((DEPTH, HY_FILTER_HIDDEN), 0.1),
        "hy_freq2": 1.0 + nrm((DEPTH, HY_FILTER_HIDDEN), 0.05),
        "hy_w3": nrm((DEPTH, HY_FILTER_HIDDEN, 2 * HY_ORDER * HY_WIDTH), HY_FILTER_HIDDEN ** -0.5),
        "hy_bias": nrm((DEPTH, HY_ORDER, HY_WIDTH), 1.0),
        "ssd_conv_w": nrm((DEPTH, CONV_W, SSD_XBC), CONV_W ** -0.5),
        "ssd_conv_b": nrm((DEPTH, SSD_XBC), 0.02),
        "ssd_dt_bias": dt0 + jnp.log(-jnp.expm1(-dt0)),
        "ssd_a_log": jnp.log(jax.random.uniform(next(ks), (DEPTH, 2, SSD_HEADS), F32, 1.0, 16.0)),
        "ssd_d": 1.0 + nrm((DEPTH, SSD_HEADS), 0.1),
        "ssd_norm_g": 1.0 + nrm((DEPTH, SSD_WIDTH), 0.05),
        "ffn_w_gate": nrm((DEPTH, D_MODEL, D_FF), D_MODEL ** -0.5),
        "ffn_w_up": nrm((DEPTH, D_MODEL, D_FF), D_MODEL ** -0.5),
        "ffn_conv_w": nrm((DEPTH, 3, 3, D_FF), 1.0 / 3.0),
        "ffn_conv_b": nrm((DEPTH, D_FF), 0.02),
        "ffn_w_down": nrm((DEPTH, D_FF, D_MODEL), D_FF ** -0.5),
        "final_norm_g": 1.0 + nrm((D_MODEL,), 0.05),
    }


def reference(x, c, ctx, c_ctx, w_ada, b_ada, norm1_g, norm2_g, w_in, w_out,
              hg_lb_logits, hg_norm_g, hy_conv_w, hy_conv_b, hy_w1, hy_b1, hy_freq1,
              hy_w2, hy_b2, hy_freq2, hy_w3, hy_bias, ssd_conv_w, ssd_conv_b,
              ssd_dt_bias, ssd_a_log, ssd_d, ssd_norm_g, ffn_w_gate, ffn_w_up,
              ffn_conv_w, ffn_conv_b, ffn_w_down, final_norm_g):
    rows = x.shape[1] // GRID_W
    lb_all = jnp.cumsum(jax.nn.softmax(hg_lb_logits.astype(F32), axis=0), axis=0)
    lb_all = lb_all - lb_all[0]
    for l in range(DEPTH):
        last = l == DEPTH - 1
        p = {
            "w_in": w_in[l], "w_out": w_out[l], "hg_norm_g": hg_norm_g[l],
            "hy_conv_w": hy_conv_w[l], "hy_conv_b": hy_conv_b[l],
            "hy_w1": hy_w1[l], "hy_b1": hy_b1[l], "hy_freq1": hy_freq1[l],
            "hy_w2": hy_w2[l], "hy_b2": hy_b2[l], "hy_freq2": hy_freq2[l],
            "hy_w3": hy_w3[l], "hy_bias": hy_bias[l],
            "ssd_conv_w": ssd_conv_w[l], "ssd_conv_b": ssd_conv_b[l],
            "ssd_dt_bias": ssd_dt_bias[l], "ssd_a_log": ssd_a_log[l],
            "ssd_d": ssd_d[l], "ssd_norm_g": ssd_norm_g[l],
            "ffn_w_gate": ffn_w_gate[l], "ffn_w_up": ffn_w_up[l],
            "ffn_conv_w": ffn_conv_w[l], "ffn_conv_b": ffn_conv_b[l],
            "ffn_w_down": ffn_w_down[l],
        }
        mx = [m[:, None, :] for m in jnp.split(jax.nn.silu(c) @ w_ada[l] + b_ada[l], 6, axis=-1)]
        mc = jnp.split(jax.nn.silu(c_ctx) @ w_ada[l] + b_ada[l], 6, axis=-1)
        hx = modulate(rms_norm(x, norm1_g[l]), mx[0], mx[1])
        hc = modulate(rms_norm(ctx, norm1_g[l]), mc[0], mc[1])
        yx, yc = token_mixers(hx, hc, p, lb_all[l].reshape(HG_HEADS, HG_DK), not last)
        x = x + mx[2] * yx
        x = x + mx[5] * conv_ffn(modulate(rms_norm(x, norm2_g[l]), mx[3], mx[4]), p, rows)
        if not last:
            ctx = ctx + mc[2] * yc
            ctx = ctx + mc[5] * conv_ffn(modulate(rms_norm(ctx, norm2_g[l]), mc[3], mc[4]), p, 1)
    return rms_norm(x, final_norm_g)
```

```python
import functools
import math

import numpy as np
import jax
import jax.numpy as jnp
from jax import lax
from jax.experimental import pallas as pl
from jax.experimental.pallas import tpu as pltpu

F32 = jnp.float32
BF16 = jnp.bfloat16

D_MODEL = 1024
DEPTH = 2
CTX_LEN = 256
GRID_W = 64
EPS = 1e-6

HG_HEADS = 4
HG_DK = 64
HG_WIDTH = 256
HG_CHUNK = 32

HY_WIDTH = 256
HY_EMB_BANDS = 16
HY_HIDDEN = 64
HY_MIN_DECAY = math.log(1e-2) / 1.5
HY_MAX_DECAY = math.log(1e-2) / 0.3

SSD_HEADS = 8
SSD_HEAD_DIM = 64
SSD_WIDTH = 512
SSD_GROUPS = 2
SSD_STATE = 128
SSD_XBC = 1024
SSD_CHUNK = 64

D_FF = 2816
HG_COLS = 5 * HG_WIDTH
HY_COLS = 3 * HY_WIDTH

TOKEN_BLOCK = 256
FFT_N1 = 128
VMEM_LIMIT = 56 * 1024 * 1024


def _cparams(sem, big=False):
    kw = dict(dimension_semantics=sem)
    if big:
        kw["vmem_limit_bytes"] = VMEM_LIMIT
    return pltpu.CompilerParams(**kw)


def _dot(a, b):
    return lax.dot_general(a, b, (((1,), (0,)), ((), ())), preferred_element_type=F32)


def _dot_nt(a, b):
    return lax.dot_general(a, b, (((1,), (1,)), ((), ())), preferred_element_type=F32)


def _dot_tn(a, b):
    return lax.dot_general(a, b, (((0,), (0,)), ((), ())), preferred_element_type=F32)


def _split3(a):
    hi = a.astype(BF16)
    r = a - hi.astype(F32)
    mid = r.astype(BF16)
    lo = (r - mid.astype(F32)).astype(BF16)
    return hi, mid, lo


def _dot_exact_lhs(mask_bf16, a):
    h, m, l = _split3(a)
    return _dot(mask_bf16, h) + _dot(mask_bf16, m) + _dot(mask_bf16, l)


def _dot_exact_rhs(a, mask_bf16):
    h, m, l = _split3(a)
    return _dot(h, mask_bf16) + _dot(m, mask_bf16) + _dot(l, mask_bf16)


def _dot3(a, b):
    ah, am, _ = _split3(a)
    bh, bm, _ = _split3(b)
    return _dot(ah, bh) + _dot(ah, bm) + _dot(am, bh)


def _dot3_pre(fparts, b):
    fh, fm = fparts
    bh, bm, _ = _split3(b)
    return _dot(fh, bh) + _dot(fh, bm) + _dot(fm, bh)


def _sigmoid(x):
    return 1.0 / (1.0 + jnp.exp(-x))


def _silu(x):
    return x * _sigmoid(x)


def _softplus(x):
    return jnp.maximum(x, 0.0) + jnp.log(1.0 + jnp.exp(-jnp.abs(x)))


def _log_sigmoid(x):
    return jnp.minimum(x, 0.0) - jnp.log(1.0 + jnp.exp(-jnp.abs(x)))


def _conv3(blk, prev_row, next_row, w_ref, b_ref):
    rows = blk.shape[0]
    ri = lax.broadcasted_iota(jnp.int32, blk.shape, 0)
    dn = jnp.where(ri == 0, prev_row, pltpu.roll(blk, 1, 0))
    up = jnp.where(ri == rows - 1, next_row, pltpu.roll(blk, rows - 1, 0))
    return dn * w_ref[0:1, :] + blk * w_ref[1:2, :] + up * w_ref[2:3, :] + b_ref[...]


def _ada_body(c_ref, w_ref, b_ref, o_ref):
    cc = c_ref[...]
    o_ref[0] = _dot3(_silu(cc), w_ref[0]) + b_ref[0]


def _ada(cc, w_ada, b_ada):
    tn = 1536
    n = w_ada.shape[-1]
    return pl.pallas_call(
        _ada_body,
        grid=(DEPTH, n // tn),
        in_specs=[
            pl.BlockSpec((8, D_MODEL), lambda l, j: (0, 0)),
            pl.BlockSpec((1, D_MODEL, tn), lambda l, j: (l, 0, j)),
            pl.BlockSpec((1, 1, tn), lambda l, j: (l, 0, j)),
        ],
        out_specs=pl.BlockSpec((1, 8, tn), lambda l, j: (l, 0, j)),
        out_shape=jax.ShapeDtypeStruct((DEPTH, 8, n), F32),
        compiler_params=_cparams(("arbitrary", "arbitrary"), big=True),
        name="adaln",
    )(cc, w_ada, b_ada.reshape(DEPTH, 1, n))


def _nmm_body(x_ref, ss_ref, g_ref, w_ref, *o_refs, offs, chunk):
    x = x_ref[0]
    ms = jnp.mean(x * x, axis=-1, keepdims=True)
    y = x * lax.rsqrt(ms + EPS) * g_ref[...]
    ss = ss_ref[0]
    y = y * (1.0 + ss[1:2, :]) + ss[0:1, :]
    yb = y.astype(BF16)
    for o_ref, off in zip(o_refs, offs):
        n = o_ref.shape[-1]
        for j in range(0, n, chunk):
            w = min(chunk, n - j)
            o_ref[0, :, j:j + w] = _dot(yb, w_ref[:, off + j:off + j + w]).astype(o_ref.dtype)


def _nmm(x, ss, g, w, widths, dtypes, tm, name):
    bsz, seq, d = x.shape
    offs = [int(v) for v in np.cumsum([0] + list(widths[:-1]))]
    return pl.pallas_call(
        functools.partial(_nmm_body, offs=offs, chunk=512),
        grid=(bsz, seq // tm),
        in_specs=[
            pl.BlockSpec((1, tm, d), lambda b, t: (b, t, 0)),
            pl.BlockSpec((1, 2, d), lambda b, t: (b, 0, 0)),
            pl.BlockSpec((1, d), lambda b, t: (0, 0)),
            pl.BlockSpec(w.shape, lambda b, t: (0, 0)),
        ],
        out_specs=[pl.BlockSpec((1, tm, n), lambda b, t: (b, t, 0)) for n in widths],
        out_shape=[jax.ShapeDtypeStruct((bsz, seq, n), dt) for n, dt in zip(widths, dtypes)],
        compiler_params=_cparams(("parallel", "arbitrary"), big=True),
        name=name,
    )(x, ss, g.reshape(1, d), w)


def _gla_body(qx, fx, vx, qc, fc, vc, lb_ref, ox, oc, st_ref, *, layer, reverse):
    t = pl.program_id(1)
    is_ctx = t == 0
    tb, ch = TOKEN_BLOCK, HG_CHUNK
    nch = tb // ch

    @pl.when(is_ctx)
    def _():
        st_ref[...] = jnp.zeros_like(st_ref)

    q = jnp.where(is_ctx, qc[0], qx[0]) * (HG_DK ** -0.5)
    a = jnp.where(is_ctx, fc[0], fx[0])
    v = jnp.where(is_ctx, vc[0], vx[0])

    lg = lb_ref[...]
    e = jnp.exp(lg - jnp.max(lg, axis=0, keepdims=True))
    den = jnp.sum(e, axis=0, keepdims=True)
    num = jnp.zeros_like(den)
    for r in range(1, layer + 1):
        num = num + e[r:r + 1, :]
    lb = num / den
    p = jnp.log(lb)
    qv = jnp.log(1.0 - lb) + _log_sigmoid(a)
    logf = jnp.maximum(p, qv) + jnp.log(1.0 + jnp.exp(-jnp.abs(p - qv)))
    k = 1.0 - jnp.exp(logf)

    ri = lax.broadcasted_iota(jnp.int32, (tb, tb), 0)
    ci = lax.broadcasted_iota(jnp.int32, (tb, tb), 1)
    same = (ri // ch) == (ci // ch)
    tri = (ci >= ri) if reverse else (ci <= ri)
    mask = jnp.logical_and(same, tri)
    mask_b = jnp.where(mask, 1.0, 0.0).astype(BF16)
    bdiag = (ri // HG_DK) == (ci // HG_DK)

    bcs = _dot_exact_lhs(mask_b, logf)
    b3 = bcs.reshape(nch, ch, tb)
    mid = ch // 2 if reverse else ch // 2 - 1
    end = 0 if reverse else ch - 1
    b_mid = b3[:, mid:mid + 1, :]
    b_end = b3[:, end:end + 1, :]
    q3 = q.reshape(nch, ch, tb)
    k3 = k.reshape(nch, ch, tb)
    d1 = b3 - b_mid
    qd = (q3 * jnp.exp(d1)).reshape(tb, tb)
    kd = (k3 * jnp.exp(-d1)).reshape(tb, tb).astype(BF16)
    kup = (k3 * jnp.exp(b_end - b3)).astype(BF16)
    qb = (q3 * jnp.exp(b3)).astype(BF16)
    dec = jnp.exp(b_end)
    v3 = v.astype(BF16).reshape(nch, ch, tb)

    lane_head = lax.broadcasted_iota(jnp.int32, (1, tb), 1) // HG_DK
    acc = jnp.zeros((tb, tb), F32)
    for h in range(HG_HEADS):
        sel = lane_head == h
        qh = jnp.where(sel, qd, 0.0).astype(BF16)
        s = _dot_nt(qh, kd)
        s = jnp.where(mask, s, 0.0).astype(BF16)
        vh = jnp.where(sel, v, 0.0).astype(BF16)
        acc = acc + _dot(s, vh)

    inter = [None] * nch
    order = range(nch - 1, -1, -1) if reverse else range(nch)
    for c in order:
        st = st_ref[...]
        inter[c] = _dot_nt(qb[c], st.astype(BF16))
        upd = _dot_tn(v3[c], kup[c])
        st_ref[...] = st * dec[c] + jnp.where(bdiag, upd, 0.0)
    o = acc + jnp.concatenate(inter, axis=0)

    @pl.when(is_ctx)
    def _():
        oc[0] = o

    @pl.when(jnp.logical_not(is_ctx))
    def _():
        ox[0] = o


def _scan_block_maps(nxb, reverse):
    if reverse:
        return lambda t: jnp.where(t == 0, nxb - 1, nxb - t)
    return lambda t: jnp.maximum(t - 1, 0)


def _gla(u_x, u_c, lb_logits, layer, reverse):
    bsz, seq, _ = u_x.shape
    tb = TOKEN_BLOCK
    nxb = seq // tb
    xb = _scan_block_maps(nxb, reverse)
    fcol = 2 if reverse else 1

    def xs(col):
        return pl.BlockSpec((1, tb, HG_WIDTH), lambda b, t: (b, xb(t), col))

    def cs(col):
        return pl.BlockSpec((1, tb, HG_WIDTH), lambda b, t: (b, 0, col))

    return pl.pallas_call(
        functools.partial(_gla_body, layer=layer, reverse=reverse),
        grid=(bsz, nxb + 1),
        in_specs=[xs(0), xs(fcol), xs(3), cs(0), cs(fcol), cs(3),
                  pl.BlockSpec((DEPTH, HG_WIDTH), lambda b, t: (0, 0))],
        out_specs=[pl.BlockSpec((1, tb, HG_WIDTH), lambda b, t: (b, xb(t), 0)),
                   pl.BlockSpec((1, tb, HG_WIDTH), lambda b, t: (b, 0, 0))],
        out_shape=[jax.ShapeDtypeStruct((bsz, seq, HG_WIDTH), F32),
                   jax.ShapeDtypeStruct((bsz, CTX_LEN, HG_WIDTH), F32)],
        scratch_shapes=[pltpu.VMEM((HG_WIDTH, HG_WIDTH), F32)],
        compiler_params=_cparams(("parallel", "arbitrary"), big=True),
        name="hgrn2_bwd" if reverse else "hgrn2_fwd",
    )(u_x, u_x, u_x, u_c, u_c, u_c, lb_logits)


def _ssd_body(xx, hp, hn, dtx, dttx, xc, dtc, dttc, cw, cb, dtb_e, alog_e, dtb_t, alog_t, dsk,
              ox, oc, st_ref, y_ref, *, reverse, nxb):
    t = pl.program_id(1)
    is_ctx = t == 0
    tb, ch = TOKEN_BLOCK, SSD_CHUNK
    nch = tb // ch
    d = 1 if reverse else 0
    xblk = (nxb - t) if reverse else (t - 1)

    @pl.when(is_ctx)
    def _():
        st_ref[...] = jnp.zeros_like(st_ref)

    raw = jnp.where(is_ctx, xc[0], xx[0])
    prev_ok = jnp.logical_and(jnp.logical_not(is_ctx), xblk > 0)
    next_ok = jnp.logical_and(jnp.logical_not(is_ctx), xblk < nxb - 1)
    prev_row = jnp.where(prev_ok, hp[0, 7:8, :], 0.0)
    next_row = jnp.where(next_ok, hn[0, 0:1, :], 0.0)
    xbc = _silu(_conv3(raw, prev_row, next_row, cw, cb))
    xs = xbc[:, :SSD_WIDTH]
    bm = xbc[:, SSD_WIDTH:SSD_WIDTH + 256].astype(BF16)
    cm = xbc[:, SSD_WIDTH + 256:].astype(BF16)

    ri = lax.broadcasted_iota(jnp.int32, (tb, tb), 0)
    ci = lax.broadcasted_iota(jnp.int32, (tb, tb), 1)
    same = (ri // ch) == (ci // ch)
    mask = jnp.logical_and(same, (ci >= ri) if reverse else (ci <= ri))
    mask_b = jnp.where(mask, 1.0, 0.0).astype(BF16)
    mask_t = jnp.logical_and(same, (ri >= ci) if reverse else (ri <= ci))
    mask_tb = jnp.where(mask_t, 1.0, 0.0).astype(BF16)

    dt_raw = jnp.where(is_ctx, dtc[0], dtx[0])
    ej = lax.broadcasted_iota(jnp.int32, (128, SSD_WIDTH), 0)
    el = lax.broadcasted_iota(jnp.int32, (128, SSD_WIDTH), 1)
    expand = jnp.where(ej == d * SSD_HEADS + el // SSD_HEAD_DIM, 1.0, 0.0).astype(BF16)
    dte = _softplus(_dot_exact_rhs(dt_raw, expand) + dtb_e[d:d + 1, :])
    a_e = -jnp.exp(alog_e[d:d + 1, :])
    acs = _dot_exact_lhs(mask_b, dte * a_e)
    xdt = xs * dte
    acs3 = acs.reshape(nch, ch, SSD_WIDTH)
    end = 0 if reverse else ch - 1
    a_end = acs3[:, end:end + 1, :]
    xw = (xdt.reshape(nch, ch, SSD_WIDTH) * jnp.exp(a_end - acs3)).astype(BF16)
    ea = jnp.exp(acs3)
    dec = jnp.exp(a_end)

    dtt_raw = jnp.where(is_ctx, dttc[0], dttx[0])[d * SSD_HEADS:(d + 1) * SSD_HEADS, :]
    dtt = _softplus(dtt_raw + dtb_t[d * SSD_HEADS:(d + 1) * SSD_HEADS, :])
    a_t = -jnp.exp(alog_t[d * SSD_HEADS:(d + 1) * SSD_HEADS, :])
    acs_t = _dot_exact_rhs(dtt * a_t, mask_tb)

    xdt_b = xdt.astype(BF16)
    hpg = SSD_HEADS // SSD_GROUPS
    for g in range(SSD_GROUPS):
        cbm = _dot_nt(cm[:, g * SSD_STATE:(g + 1) * SSD_STATE], bm[:, g * SSD_STATE:(g + 1) * SSD_STATE])
        for hh in range(hpg):
            h = g * hpg + hh
            lo = h * SSD_HEAD_DIM
            seg = acs[:, lo:lo + 1] - acs_t[h:h + 1, :]
            lm = jnp.exp(jnp.where(mask, seg, -1e30))
            wm = (cbm * lm).astype(BF16)
            y_ref[:, lo:lo + SSD_HEAD_DIM] = _dot(wm, xdt_b[:, lo:lo + SSD_HEAD_DIM])

    gw = SSD_WIDTH // SSD_GROUPS
    inter = [[None] * SSD_GROUPS for _ in range(nch)]
    order = range(nch - 1, -1, -1) if reverse else range(nch)
    for c in order:
        rows = slice(c * ch, (c + 1) * ch)
        for g in range(SSD_GROUPS):
            st = st_ref[g]
            cols = slice(g * gw, (g + 1) * gw)
            inter[c][g] = _dot(cm[rows, g * SSD_STATE:(g + 1) * SSD_STATE], st.astype(BF16)) * ea[c][:, cols]
            upd = _dot_tn(bm[rows, g * SSD_STATE:(g + 1) * SSD_STATE], xw[c][:, cols])
            st_ref[g] = st * dec[c][:, cols] + upd
    y_inter = jnp.concatenate([jnp.concatenate(r, axis=1) for r in inter], axis=0)
    o = y_ref[...] + y_inter
    if not reverse:
        o = o + dsk[...] * xs

    @pl.when(is_ctx)
    def _():
        oc[0] = o

    @pl.when(jnp.logical_not(is_ctx))
    def _():
        ox[0] = o


def _ssd(zx_x, zx_c, dt_x, dt_c, dtt_x, dtt_c, prm, reverse):
    bsz, seq, _ = zx_x.shape
    tb = TOKEN_BLOCK
    nxb = seq // tb
    xb = _scan_block_maps(nxb, reverse)
    hb = tb // 8
    full = lambda a: pl.BlockSpec(a.shape, lambda b, t: (0,) * a.ndim)
    params = [prm["cw"], prm["cb"], prm["dtb_e"], prm["alog_e"], prm["dtb_t"], prm["alog_t"], prm["dsk"]]
    return pl.pallas_call(
        functools.partial(_ssd_body, reverse=reverse, nxb=nxb),
        grid=(bsz, nxb + 1),
        in_specs=[
            pl.BlockSpec((1, tb, SSD_XBC), lambda b, t: (b, xb(t), 0)),
            pl.BlockSpec((1, 8, SSD_XBC), lambda b, t: (b, jnp.maximum(xb(t) * hb - 1, 0), 0)),
            pl.BlockSpec((1, 8, SSD_XBC), lambda b, t: (b, jnp.minimum((xb(t) + 1) * hb, nxb * hb - 1), 0)),
            pl.BlockSpec((1, tb, 128), lambda b, t: (b, xb(t), 0)),
            pl.BlockSpec((1, 16, tb), lambda b, t: (b, 0, xb(t))),
            pl.BlockSpec((1, tb, SSD_XBC), lambda b, t: (b, 0, 0)),
            pl.BlockSpec((1, tb, 128), lambda b, t: (b, 0, 0)),
            pl.BlockSpec((1, 16, tb), lambda b, t: (b, 0, 0)),
        ] + [full(a) for a in params],
        out_specs=[pl.BlockSpec((1, tb, SSD_WIDTH), lambda b, t: (b, xb(t), 0)),
                   pl.BlockSpec((1, tb, SSD_WIDTH), lambda b, t: (b, 0, 0))],
        out_shape=[jax.ShapeDtypeStruct((bsz, seq, SSD_WIDTH), F32),
                   jax.ShapeDtypeStruct((bsz, CTX_LEN, SSD_WIDTH), F32)],
        scratch_shapes=[pltpu.VMEM((SSD_GROUPS, SSD_STATE, SSD_WIDTH // SSD_GROUPS), F32),
                        pltpu.VMEM((tb, SSD_WIDTH), F32)],
        compiler_params=_cparams(("parallel", "arbitrary"), big=True),
        name="ssd_bwd" if reverse else "ssd_fwd",
    )(zx_x, zx_x, zx_x, dt_x, dtt_x, zx_c, dt_c, dtt_c, *params)


def _hy_pre_body(u, hp, hn, cw, cb, v_ref, x1_ref, x2_ref, *, nblk):
    t = pl.program_id(1)
    prev_row = jnp.where(t > 0, hp[0, 7:8, :], 0.0)
    next_row = jnp.where(t < nblk - 1, hn[0, 0:1, :], 0.0)
    z = _conv3(u[0], prev_row, next_row, cw, cb)
    v_ref[0] = z[:, :HY_WIDTH]
    x1_ref[0] = z[:, HY_WIDTH:2 * HY_WIDTH]
    x2_ref[0] = z[:, 2 * HY_WIDTH:]


def _hy_pre(u, cw, cb):
    bsz, seq, _ = u.shape
    tb = TOKEN_BLOCK
    nblk = seq // tb
    hb = tb // 8
    out = jax.ShapeDtypeStruct((bsz, seq, HY_WIDTH), F32)
    ospec = pl.BlockSpec((1, tb, HY_WIDTH), lambda b, t: (b, t, 0))
    return pl.pallas_call(
        functools.partial(_hy_pre_body, nblk=nblk),
        grid=(bsz, nblk),
        in_specs=[
            pl.BlockSpec((1, tb, HY_COLS), lambda b, t: (b, t, 0)),
            pl.BlockSpec((1, 8, HY_COLS), lambda b, t: (b, jnp.maximum(t * hb - 1, 0), 0)),
            pl.BlockSpec((1, 8, HY_COLS), lambda b, t: (b, jnp.minimum((t + 1) * hb, nblk * hb - 1), 0)),
            pl.BlockSpec((3, HY_COLS), lambda b, t: (0, 0)),
            pl.BlockSpec((1, HY_COLS), lambda b, t: (0, 0)),
        ],
        out_specs=[ospec, ospec, ospec],
        out_shape=[out, out, out],
        compiler_params=_cparams(("parallel", "arbitrary")),
        name="hyena_shortconv",
    )(u, u, u, cw, cb)


def _filt_body(w1, b1, f1, w2, b2, f2, w3, delta, buf_ref, sa_ref, *, seq_len, tr):
    i = pl.program_id(0)
    jrow = i * tr + lax.broadcasted_iota(jnp.int32, (tr, 128), 0)
    pos = jnp.where(jrow < seq_len, jrow, 2 * seq_len - jrow).astype(F32)
    tpos = pos / (seq_len - 1.0)
    wpos = pos * (2.0 * math.pi / seq_len)
    lane = lax.broadcasted_iota(jnp.int32, (tr, 128), 1)
    bidx = jnp.where(lane <= HY_EMB_BANDS, lane - 1, lane - 1 - HY_EMB_BANDS).astype(F32)
    band = 1e-4 + bidx * ((HY_EMB_BANDS - 1 - 1e-4) / (HY_EMB_BANDS - 1))
    ang = band * wpos + jnp.where(lane > HY_EMB_BANDS, 0.5 * math.pi, 0.0)
    z = jnp.where(lane == 0, tpos, jnp.where(lane <= 2 * HY_EMB_BANDS, jnp.cos(ang), 0.0))
    h = jnp.sin(f1[...] * (_dot3(z, w1[...]) + b1[...]))
    h = jnp.sin(f2[...] * (_dot3(h, w2[...]) + b2[...]))
    o = _dot3(h, w3[0])
    o = o * jnp.exp(-tpos[:, 0:1] * delta[...])
    o = jnp.where(jrow[:, 0:1] == seq_len, 0.0, o)
    buf_ref[0] = o[:, :HY_WIDTH]
    buf_ref[1] = o[:, HY_WIDTH:]

    @pl.when(i == 0)
    def _():
        sa_ref[...] = jnp.zeros_like(sa_ref)

    sa_ref[...] = sa_ref[...] + jnp.sum(jnp.abs(o), axis=0, keepdims=True)


def _filt_gen(p, seq_len, tr):
    n = 2 * seq_len
    assert seq_len % tr == 0
    half = seq_len // tr
    full = lambda a: pl.BlockSpec(a.shape, lambda i: (0,) * a.ndim)
    small = [p["w1"], p["b1"], p["f1"], p["w2"], p["b2"], p["f2"]]
    return pl.pallas_call(
        functools.partial(_filt_body, seq_len=seq_len, tr=tr),
        grid=(n // tr,),
        in_specs=[full(a) for a in small] + [
            pl.BlockSpec((1, HY_HIDDEN, 2 * HY_WIDTH), lambda i: (i // half, 0, 0)),
            full(p["delta"]),
        ],
        out_specs=[pl.BlockSpec((2, tr, HY_WIDTH), lambda i: (0, i, 0)),
                   pl.BlockSpec((8, 2 * HY_WIDTH), lambda i: (0, 0))],
        out_shape=[jax.ShapeDtypeStruct((2, n, HY_WIDTH), F32),
                   jax.ShapeDtypeStruct((8, 2 * HY_WIDTH), F32)],
        compiler_params=_cparams(("arbitrary",)),
        name="hyena_filter",
    )(*small, p["w3"], p["delta"])


def _expand_twiddle(t_ref):
    tr = jnp.concatenate([jnp.broadcast_to(t_ref[0, 0, :, j:j + 1], (FFT_N1, HY_WIDTH)) for j in range(8)], axis=1)
    ti = jnp.concatenate([jnp.broadcast_to(t_ref[0, 1, :, j:j + 1], (FFT_N1, HY_WIDTH)) for j in range(8)], axis=1)
    return tr, ti


def _s1_body(x_ref, f_ref, t_ref, o_ref):
    a = _dot3_pre((f_ref[0], f_ref[1]), x_ref[0])
    tr, ti = _expand_twiddle(t_ref)
    ar, ai = a[:FFT_N1], a[FFT_N1:]
    o_ref[0, :FFT_N1, :] = ar * tr - ai * ti
    o_ref[0, FFT_N1:, :] = ar * ti + ai * tr


def _fft_stage1(x, fmat, twid):
    g, rows, cols = x.shape
    tile = 8 * HY_WIDTH
    return pl.pallas_call(
        _s1_body,
        grid=(g, cols // tile),
        in_specs=[
            pl.BlockSpec((1, rows, tile), lambda i, j: (i, 0, j)),
            pl.BlockSpec(fmat.shape, lambda i, j: (0, 0, 0)),
            pl.BlockSpec((1, 2, FFT_N1, 8), lambda i, j: (j, 0, 0, 0)),
        ],
        out_specs=pl.BlockSpec((1, 2 * FFT_N1, tile), lambda i, j: (i, 0, j)),
        out_shape=jax.ShapeDtypeStruct((g, 2 * FFT_N1, cols), F32),
        compiler_params=_cparams(("parallel", "arbitrary"), big=True),
        name="fft_stage1",
    )(x, fmat, twid)


def _s2_body(a_ref, h_ref, f_ref, fc_ref, o_ref, *, kb):
    for k in range(kb):
        z = jnp.concatenate([a_ref[0, k], a_ref[1, k]], axis=0)
        x = _dot3_pre((f_ref[0], f_ref[1]), z)
        xr, xi = x[:FFT_N1], x[FFT_N1:]
        hr, hi = h_ref[0, k], h_ref[1, k]
        y = jnp.concatenate([xr * hr - xi * hi, xr * hi + xi * hr], axis=0)
        c = _dot3_pre((fc_ref[0], fc_ref[1]), y)
        o_ref[0, k] = c[:FFT_N1]
        o_ref[1, k] = c[FFT_N1:]


def _fft_stage2(a, hspec, fmat, fmat_c, kb=8):
    blk = pl.BlockSpec((2, kb, FFT_N1, HY_WIDTH), lambda i: (0, i, 0, 0))
    return pl.pallas_call(
        functools.partial(_s2_body, kb=kb),
        grid=(FFT_N1 // kb,),
        in_specs=[blk, blk,
                  pl.BlockSpec(fmat.shape, lambda i: (0, 0, 0)),
                  pl.BlockSpec(fmat_c.shape, lambda i: (0, 0, 0))],
        out_specs=blk,
        out_shape=jax.ShapeDtypeStruct(a.shape, F32),
        compiler_params=_cparams(("arbitrary",), big=True),
        name="fft_stage2_mul",
    )(a, hspec, fmat, fmat_c)


def _s2f_body(a_ref, sa_ref, f_ref, o_ref, *, kb, n):
    o = pl.program_id(0)
    sa = jnp.where(o == 0, sa_ref[0:1, :HY_WIDTH], sa_ref[0:1, HY_WIDTH:])
    scale = 1.0 / (sa * float(n))
    for k in range(kb):
        z = jnp.concatenate([a_ref[0, 0, k], a_ref[0, 1, k]], axis=0)
        x = _dot3_pre((f_ref[0], f_ref[1]), z)
        o_ref[0, 0, k] = x[:FFT_N1] * scale
        o_ref[0, 1, k] = x[FFT_N1:] * scale


def _fft_stage2_filter(a, sumabs, fmat, n, kb=8):
    blk = pl.BlockSpec((1, 2, kb, FFT_N1, HY_WIDTH), lambda o, i: (o, 0, i, 0, 0))
    return pl.pallas_call(
        functools.partial(_s2f_body, kb=kb, n=n),
        grid=(2, FFT_N1 // kb),
        in_specs=[blk, pl.BlockSpec(sumabs.shape, lambda o, i: (0, 0)),
                  pl.BlockSpec(fmat.shape, lambda o, i: (0, 0, 0))],
        out_specs=blk,
        out_shape=jax.ShapeDtypeStruct(a.shape, F32),
        compiler_params=_cparams(("parallel", "arbitrary"), big=True),
        name="fft_stage2_filter",
    )(a, sumabs, fmat)


def _s3_body(c_ref, t_ref, f_ref, gate_ref, v_ref, bias_ref, o_ref):
    tr, ti = _expand_twiddle(t_ref)
    cr, ci = c_ref[:FFT_N1, :], c_ref[FFT_N1:, :]
    dmat = jnp.concatenate([cr * tr + ci * ti, ci * tr - cr * ti], axis=0)
    w = _dot3_pre((f_ref[0], f_ref[1]), dmat)
    v = v_ref[...]
    o_ref[...] = gate_ref[...] * (w + v * bias_ref[...])


def _fft_stage3(c, twid, fmat, gate, v, bias_row):
    rows, cols = gate.shape
    tile = 8 * HY_WIDTH
    dspec = pl.BlockSpec((rows, tile), lambda j: (0, j))
    return pl.pallas_call(
        _s3_body,
        grid=(cols // tile,),
        in_specs=[
            pl.BlockSpec((2 * FFT_N1, tile), lambda j: (0, j)),
            pl.BlockSpec((1, 2, FFT_N1, 8), lambda j: (j, 0, 0, 0)),
            pl.BlockSpec(fmat.shape, lambda j: (0, 0, 0)),
            dspec, dspec,
            pl.BlockSpec((1, tile), lambda j: (0, 0)),
        ],
        out_specs=dspec,
        out_shape=jax.ShapeDtypeStruct((rows, cols), F32),
        compiler_params=_cparams(("arbitrary",), big=True),
        name="fft_stage3_gate",
    )(c, twid, fmat, gate, v, bias_row)


def _hyc_body(v_ref, x1_ref, x2_ref, buf_ref, sa_ref, cs_ref, bias_ref, o_ref):
    n = 2 * CTX_LEN
    cmat, smat = cs_ref[0], cs_ref[1]
    c_in, s_in = cmat[:, :CTX_LEN], smat[:, :CTX_LEN]
    c_out, s_out = cmat[:CTX_LEN, :], smat[:CTX_LEN, :]
    yr, yi = v_ref[0], v_ref[1]
    for o, gate in enumerate((x1_ref, x2_ref)):
        bufo = buf_ref[o]
        scale = 1.0 / (sa_ref[0:1, o * HY_WIDTH:(o + 1) * HY_WIDTH] * float(n))
        hr = _dot3(cmat, bufo) * scale
        hi = -_dot3(smat, bufo) * scale
        xr = _dot3(c_in, yr) + _dot3(s_in, yi)
        xi = _dot3(c_in, yi) - _dot3(s_in, yr)
        zr = xr * hr - xi * hi
        zi = xr * hi + xi * hr
        wr = _dot3(c_out, zr) - _dot3(s_out, zi)
        wi = _dot3(c_out, zi) + _dot3(s_out, zr)
        b = bias_ref[o:o + 1, :]
        yr = gate[0] * (wr + yr * b)
        yi = gate[1] * (wi + yi * b)
    o_ref[0] = yr
    o_ref[1] = yi


def _hy_ctx(v, x1, x2, buf, sumabs, cs, bias):
    return pl.pallas_call(
        _hyc_body,
        out_shape=jax.ShapeDtypeStruct(v.shape, F32),
        compiler_params=pltpu.CompilerParams(vmem_limit_bytes=VMEM_LIMIT),
        name="hyena_ctx",
    )(v, x1, x2, buf, sumabs, cs, bias)


def _dft_constants(seq_len):
    n = 2 * seq_len
    n1 = FFT_N1
    assert n == n1 * n1
    idx = np.arange(n1, dtype=np.float64)
    th = 2.0 * np.pi * np.outer(idx, idx) / n1
    fr, fi = np.cos(th), -np.sin(th)
    half = n1 // 2

    def parts(m):
        m32 = jnp.asarray(m, F32)
        hi = m32.astype(BF16)
        mid = (m32 - hi.astype(F32)).astype(BF16)
        return jnp.stack([hi, mid])

    f1_data = np.block([[fr[:, :half], -fi[:, :half]], [fi[:, :half], fr[:, :half]]])
    f1_real = np.concatenate([fr, fi], axis=0)
    f2 = np.block([[fr, -fi], [fi, fr]])
    f2c = np.block([[fr, fi], [-fi, fr]])
    f3 = np.block([[fr[:half], fi[:half]], [-fi[:half], fr[:half]]])
    tw = 2.0 * np.pi * np.outer(idx, idx) / n
    twid = np.stack([np.cos(tw), -np.sin(tw)])
    twid = twid.reshape(2, n1, n1 // 8, 8).transpose(2, 0, 1, 3)
    return dict(f1_data=parts(f1_data), f1_real=parts(f1_real), f2=parts(f2), f2c=parts(f2c),
                f3=parts(f3), twid=jnp.asarray(twid, F32))


def _ctx_dft_constants():
    n = 2 * CTX_LEN
    idx = np.arange(n, dtype=np.float64)
    th = 2.0 * np.pi * np.outer(idx, idx) / n
    return jnp.asarray(np.stack([np.cos(th), np.sin(th)]), F32)


def _hyena_filter_params(hy_w1, hy_b1, hy_freq1, hy_w2, hy_b2, hy_freq2, hy_w3):
    w1 = jnp.zeros((128, HY_HIDDEN), F32).at[:hy_w1.shape[0]].set(hy_w1)
    w3 = hy_w3.reshape(HY_HIDDEN, 2, 2 * HY_WIDTH).transpose(1, 0, 2)
    deltas = np.abs(np.linspace(HY_MIN_DECAY, HY_MAX_DECAY, HY_WIDTH))
    delta = jnp.asarray(np.tile(deltas, 2)[None, :], F32)
    row = lambda a: a.reshape(1, -1)
    return dict(w1=w1, b1=row(hy_b1), f1=row(hy_freq1), w2=hy_w2, b2=row(hy_b2), f2=row(hy_freq2),
                w3=w3, delta=delta)


def _hyena_x(v, x1, x2, hspec, bias, consts):
    bsz, seq, ch = v.shape
    assert bsz == 2
    rows, cols = bsz * (seq // FFT_N1), FFT_N1 * ch
    view = lambda a: a.reshape(rows, cols)
    y = view(v)
    for o, gate in enumerate((x1, x2)):
        a = _fft_stage1(y.reshape(1, rows, cols), consts["f1_data"], consts["twid"])
        a = a.reshape(2, FFT_N1, FFT_N1, ch)
        c = _fft_stage2(a, hspec[o], consts["f2"], consts["f2c"])
        bias_row = jnp.tile(bias[o], 8).reshape(1, 8 * ch)
        y = _fft_stage3(c.reshape(2 * FFT_N1, cols), consts["twid"], consts["f3"], view(gate), y, bias_row)
    return y.reshape(bsz, seq, ch)


def _out_body(of, ob, ug, hy, yf, yb, zz, hgn, ssn, w_ref, h_ref, gate_ref, o_ref):
    o = of[0] + ob[0]
    gi = lax.broadcasted_iota(jnp.int32, (HG_WIDTH, HG_WIDTH), 0) // HG_DK
    gj = lax.broadcasted_iota(jnp.int32, (HG_WIDTH, HG_WIDTH), 1) // HG_DK
    avg = jnp.where(gi == gj, 1.0 / HG_DK, 0.0).astype(BF16)
    sq = o * o
    sh = sq.astype(BF16)
    sl = (sq - sh.astype(F32)).astype(BF16)
    ms = _dot(sh, avg) + _dot(sl, avg)
    hg = o * lax.rsqrt(ms + EPS) * hgn[...] * _silu(ug[0])
    y = (yf[0] + yb[0]) * _silu(zz[0])
    gw = SSD_WIDTH // SSD_GROUPS
    parts = []
    for g in range(SSD_GROUPS):
        yg = y[:, g * gw:(g + 1) * gw]
        parts.append(yg * lax.rsqrt(jnp.mean(yg * yg, axis=-1, keepdims=True) + EPS))
    ys = jnp.concatenate(parts, axis=1) * ssn[...]
    acc = _dot(hg.astype(BF16), w_ref[0:HG_WIDTH, :])
    acc = acc + _dot(hy[0].astype(BF16), w_ref[HG_WIDTH:HG_WIDTH + HY_WIDTH, :])
    acc = acc + _dot(ys.astype(BF16), w_ref[HG_WIDTH + HY_WIDTH:, :])
    o_ref[0] = h_ref[0] + gate_ref[0] * acc


def _out_proj(of, ob, u_hg, hy, yf, yb, u_zx, hgn, ssn, w, h, gate, tm):
    bsz, seq, d = h.shape
    tok = lambda n, col=0: pl.BlockSpec((1, tm, n), lambda b, t: (b, t, col))
    return pl.pallas_call(
        _out_body,
        grid=(bsz, seq // tm),
        in_specs=[tok(HG_WIDTH), tok(HG_WIDTH), tok(HG_WIDTH, 4), tok(HY_WIDTH),
                  tok(SSD_WIDTH), tok(SSD_WIDTH), tok(SSD_WIDTH, 2),
                  pl.BlockSpec((1, HG_WIDTH), lambda b, t: (0, 0)),
                  pl.BlockSpec((1, SSD_WIDTH), lambda b, t: (0, 0)),
                  pl.BlockSpec(w.shape, lambda b, t: (0, 0)),
                  tok(d),
                  pl.BlockSpec((1, 1, d), lambda b, t: (b, 0, 0))],
        out_specs=tok(d),
        out_shape=jax.ShapeDtypeStruct(h.shape, F32),
        compiler_params=_cparams(("parallel", "arbitrary"), big=True),
        name="mixer_out_proj",
    )(of, ob, u_hg, hy, yf, yb, u_zx, hgn.reshape(1, -1), ssn.reshape(1, -1), w, h, gate)


def _ffn_tail_body(*refs, tm, grid_w, vertical, final, cchunk):
    if vertical:
        a_ref, hp, hn, up_ref, cw, cb, wd, h_ref, gate_ref = refs[:9]
        rest = refs[9:]
    else:
        a_ref, up_ref, cw, cb, wd, h_ref, gate_ref = refs[:7]
        rest = refs[7:]
    if final:
        fg_ref, o_ref, act_ref = rest
    else:
        o_ref, act_ref = rest
    t = pl.program_id(1)
    nt = pl.num_programs(1)
    pad = grid_w if vertical else 0
    ext_rows = tm + 2 * pad
    col = lax.broadcasted_iota(jnp.int32, (ext_rows, cchunk), 0) % grid_w
    for j in range(0, D_FF, cchunk):
        cs = slice(j, j + cchunk)
        a = a_ref[0, :, cs].astype(F32)
        if vertical:
            top = jnp.where(t > 0, hp[0, :, cs].astype(F32), 0.0)
            bot = jnp.where(t < nt - 1, hn[0, :, cs].astype(F32), 0.0)
            ext = jnp.concatenate([top, a, bot], axis=0)
        else:
            ext = a
        left = jnp.where(col != 0, pltpu.roll(ext, 1, 0), 0.0)
        right = jnp.where(col != grid_w - 1, pltpu.roll(ext, ext_rows - 1, 0), 0.0)
        acc = jnp.zeros((tm, cchunk), F32) + cb[:, cs]
        for dy in ((-1, 0, 1) if vertical else (0,)):
            r0 = pad + dy * grid_w
            ky = dy + 1
            acc = acc + left[r0:r0 + tm] * cw[3 * ky:3 * ky + 1, cs]
            acc = acc + ext[r0:r0 + tm] * cw[3 * ky + 1:3 * ky + 2, cs]
            acc = acc + right[r0:r0 + tm] * cw[3 * ky + 2:3 * ky + 3, cs]
        act_ref[:, cs] = (_silu(acc) * up_ref[0, :, cs].astype(F32)).astype(BF16)
    y = _dot(act_ref[...], wd[...])
    x = h_ref[0] + gate_ref[0] * y
    if final:
        ms = jnp.mean(x * x, axis=-1, keepdims=True)
        x = x * lax.rsqrt(ms + EPS) * fg_ref[...]
    o_ref[0] = x


def _ffn_tail(gu, cw, cb, wd, h, gate, tm, grid_w, vertical, final_g=None):
    bsz, seq, d = h.shape
    nhb = tm // grid_w
    tok = lambda n, col=0: pl.BlockSpec((1, tm, n), lambda b, t: (b, t, col))
    full = lambda a: pl.BlockSpec(a.shape, lambda b, t: (0,) * a.ndim)
    in_specs = [tok(D_FF)]
    args = [gu]
    if vertical:
        nrow = seq // grid_w
        in_specs += [
            pl.BlockSpec((1, grid_w, D_FF), lambda b, t: (b, jnp.maximum(t * nhb - 1, 0), 0)),
            pl.BlockSpec((1, grid_w, D_FF), lambda b, t: (b, jnp.minimum((t + 1) * nhb, nrow - 1), 0)),
        ]
        args += [gu, gu]
    in_specs += [tok(D_FF, 1), full(cw), full(cb), full(wd), tok(d),
                 pl.BlockSpec((1, 1, d), lambda b, t: (b, 0, 0))]
    args += [gu, cw, cb, wd, h, gate]
    if final_g is not None:
        in_specs.append(pl.BlockSpec((1, d), lambda b, t: (0, 0)))
        args.append(final_g.reshape(1, d))
    return pl.pallas_call(
        functools.partial(_ffn_tail_body, tm=tm, grid_w=grid_w, vertical=vertical,
                          final=final_g is not None, cchunk=256),
        grid=(bsz, seq // tm),
        in_specs=in_specs,
        out_specs=tok(d),
        out_shape=jax.ShapeDtypeStruct(h.shape, F32),
        scratch_shapes=[pltpu.VMEM((tm, D_FF), BF16)],
        compiler_params=_cparams(("parallel", "arbitrary"), big=True),
        name="ffn_tail",
    )(*args)


def kernel(x, c, ctx, c_ctx, w_ada, b_ada, norm1_g, norm2_g, w_in, w_out, hg_lb_logits, hg_norm_g, hy_conv_w, hy_conv_b, hy_w1, hy_b1, hy_freq1, hy_w2, hy_b2, hy_freq2, hy_w3, hy_bias, ssd_conv_w, ssd_conv_b, ssd_dt_bias, ssd_a_log, ssd_d, ssd_norm_g, ffn_w_gate, ffn_w_up, ffn_conv_w, ffn_conv_b, ffn_w_down, final_norm_g):
    bsz, seq, d = x.shape
    cc = jnp.zeros((8, d), F32).at[:bsz].set(c).at[bsz].set(c_ctx)
    mods = _ada(cc, w_ada, b_ada)

    consts = _dft_constants(seq)
    cs_ctx = _ctx_dft_constants()
    lb_logits = hg_lb_logits.astype(F32)

    z0 = HG_COLS + HY_COLS
    for l in range(DEPTH):
        last = l == DEPTH - 1
        m = mods[l].reshape(8, 6, d)
        mx = m[:bsz]
        mc = jnp.broadcast_to(m[bsz:bsz + 1], (bsz, 6, d))

        wl = w_in[l]
        w_cat = jnp.concatenate([
            wl[:, :z0], wl[:, z0 + SSD_WIDTH:z0 + SSD_WIDTH + SSD_XBC], wl[:, z0:z0 + SSD_WIDTH],
            wl[:, z0 + SSD_WIDTH + SSD_XBC:], jnp.zeros((d, 128 - 2 * SSD_HEADS), F32)], axis=1).astype(BF16)
        widths = (HG_COLS, HY_COLS, SSD_XBC + SSD_WIDTH, 128)
        dts = (F32,) * 4
        uhg_x, uhy_x, uzx_x, dt_x = _nmm(x, mx[:, 0:2], norm1_g[l], w_cat, widths, dts, 512, "in_proj_x")
        uhg_c, uhy_c, uzx_c, dt_c = _nmm(ctx, mc[:, 0:2], norm1_g[l], w_cat, widths, dts, CTX_LEN, "in_proj_ctx")

        of_x, of_c = _gla(uhg_x, uhg_c, lb_logits, l, False)
        ob_x, ob_c = _gla(uhg_x, uhg_c, lb_logits, l, True)

        rep = lambda a: jnp.repeat(a, SSD_HEAD_DIM, axis=-1)
        prm = dict(cw=ssd_conv_w[l], cb=ssd_conv_b[l].reshape(1, -1),
                   dtb_e=rep(ssd_dt_bias[l]), alog_e=rep(ssd_a_log[l]),
                   dtb_t=ssd_dt_bias[l].reshape(-1, 1), alog_t=ssd_a_log[l].reshape(-1, 1),
                   dsk=rep(ssd_d[l]).reshape(1, -1))
        dtt_x = jnp.swapaxes(dt_x[:, :, :2 * SSD_HEADS], 1, 2)
        dtt_c = jnp.swapaxes(dt_c[:, :, :2 * SSD_HEADS], 1, 2)
        yf_x, yf_c = _ssd(uzx_x, uzx_c, dt_x, dt_c, dtt_x, dtt_c, prm, False)
        yb_x, yb_c = _ssd(uzx_x, uzx_c, dt_x, dt_c, dtt_x, dtt_c, prm, True)

        fp = _hyena_filter_params(hy_w1[l], hy_b1[l], hy_freq1[l], hy_w2[l], hy_b2[l], hy_freq2[l], hy_w3[l])
        cw_hy, cb_hy = hy_conv_w[l], hy_conv_b[l].reshape(1, -1)
        v_x, x1_x, x2_x = _hy_pre(uhy_x, cw_hy, cb_hy)
        buf, sumabs = _filt_gen(fp, seq, 1024)
        fa = _fft_stage1(buf.reshape(2, FFT_N1, FFT_N1 * HY_WIDTH), consts["f1_real"], consts["twid"])
        hspec = _fft_stage2_filter(fa.reshape(2, 2, FFT_N1, FFT_N1, HY_WIDTH), sumabs, consts["f2"], 2 * seq)
        ohy_x = _hyena_x(v_x, x1_x, x2_x, hspec, hy_bias[l], consts)

        wo = w_out[l].astype(BF16)
        x = _out_proj(of_x, ob_x, uhg_x, ohy_x, yf_x, yb_x, uzx_x, hg_norm_g[l], ssd_norm_g[l], wo, x,
                      mx[:, 2:3], 512)

        w_gu = jnp.concatenate([ffn_w_gate[l], ffn_w_up[l]], axis=1).astype(BF16)
        wd = ffn_w_down[l].astype(BF16)
        cw_f = ffn_conv_w[l].reshape(9, D_FF)
        cb_f = ffn_conv_b[l].reshape(1, D_FF)
        if not last:
            v_c, x1_c, x2_c = _hy_pre(uhy_c, cw_hy, cb_hy)
            buf_c, sumabs_c = _filt_gen(fp, CTX_LEN, CTX_LEN)
            ohy_c = _hy_ctx(v_c, x1_c, x2_c, buf_c, sumabs_c, cs_ctx, hy_bias[l])
            ctx = _out_proj(of_c, ob_c, uhg_c, ohy_c, yf_c, yb_c, uzx_c, hg_norm_g[l], ssd_norm_g[l], wo, ctx,
                            mc[:, 2:3], CTX_LEN)
            (gu_c,) = _nmm(ctx, mc[:, 3:5], norm2_g[l], w_gu, (2 * D_FF,), (BF16,), CTX_LEN, "ffn_up_ctx")
            ctx = _ffn_tail(gu_c, cw_f, cb_f, wd, ctx, mc[:, 5:6], CTX_LEN, CTX_LEN, False)

        (gu_x,) = _nmm(x, mx[:, 3:5], norm2_g[l], w_gu, (2 * D_FF,), (BF16,), 512, "ffn_up_x")
        x = _ffn_tail(gu_x, cw_f, cb_f, wd, x, mx[:, 5:6], 512, GRID_W, True,
                      final_g=final_norm_g if last else None)
    return x
```

```python
import functools
import math

import numpy as np
import jax
import jax.numpy as jnp
from jax import lax
from jax.experimental import pallas as pl
from jax.experimental.pallas import tpu as pltpu

F32 = jnp.float32
BF16 = jnp.bfloat16

D_MODEL = 1024
DEPTH = 2
CTX_LEN = 256
GRID_W = 64
EPS = 1e-6

HG_HEADS = 4
HG_DK = 64
HG_WIDTH = 256
HG_CHUNK = 32

HY_WIDTH = 256
HY_EMB_BANDS = 16
HY_HIDDEN = 64
HY_MIN_DECAY = math.log(1e-2) / 1.5
HY_MAX_DECAY = math.log(1e-2) / 0.3

SSD_HEADS = 8
SSD_HEAD_DIM = 64
SSD_WIDTH = 512
SSD_GROUPS = 2
SSD_STATE = 128
SSD_XBC = 1024
SSD_CHUNK = 64

D_FF = 2816
HG_COLS = 5 * HG_WIDTH
HY_COLS = 3 * HY_WIDTH

TOKEN_BLOCK = 256
FFT_N1 = 128
VMEM_LIMIT = 56 * 1024 * 1024


def _cparams(sem, big=False):
    kw = dict(dimension_semantics=sem)
    if big:
        kw["vmem_limit_bytes"] = VMEM_LIMIT
    return pltpu.CompilerParams(**kw)


def _dot(a, b):
    return lax.dot_general(a, b, (((1,), (0,)), ((), ())), preferred_element_type=F32)


def _dot_nt(a, b):
    return lax.dot_general(a, b, (((1,), (1,)), ((), ())), preferred_element_type=F32)


def _dot_tn(a, b):
    return lax.dot_general(a, b, (((0,), (0,)), ((), ())), preferred_element_type=F32)


def _split3(a):
    hi = a.astype(BF16)
    r = a - hi.astype(F32)
    mid = r.astype(BF16)
    lo = (r - mid.astype(F32)).astype(BF16)
    return hi, mid, lo


def _dot_exact_lhs(mask_bf16, a):
    h, m, l = _split3(a)
    return _dot(mask_bf16, h) + _dot(mask_bf16, m) + _dot(mask_bf16, l)


def _dot_exact_rhs(a, mask_bf16):
    h, m, l = _split3(a)
    return _dot(h, mask_bf16) + _dot(m, mask_bf16) + _dot(l, mask_bf16)


def _dot3(a, b):
    ah, am, _ = _split3(a)
    bh, bm, _ = _split3(b)
    return _dot(ah, bh) + _dot(ah, bm) + _dot(am, bh)


def _dot3_pre(fparts, b):
    fh, fm = fparts
    bh, bm, _ = _split3(b)
    return _dot(fh, bh) + _dot(fh, bm) + _dot(fm, bh)


def _sigmoid(x):
    return 1.0 / (1.0 + jnp.exp(-x))


def _silu(x):
    return x * _sigmoid(x)


def _softplus(x):
    return jnp.maximum(x, 0.0) + jnp.log(1.0 + jnp.exp(-jnp.abs(x)))


def _log_sigmoid(x):
    return jnp.minimum(x, 0.0) - jnp.log(1.0 + jnp.exp(-jnp.abs(x)))


def _conv3(blk, prev_row, next_row, w_ref, b_ref):
    rows = blk.shape[0]
    ri = lax.broadcasted_iota(jnp.int32, blk.shape, 0)
    dn = jnp.where(ri == 0, prev_row, pltpu.roll(blk, 1, 0))
    up = jnp.where(ri == rows - 1, next_row, pltpu.roll(blk, rows - 1, 0))
    return dn * w_ref[0:1, :] + blk * w_ref[1:2, :] + up * w_ref[2:3, :] + b_ref[...]


def _ada_body(c_ref, w_ref, b_ref, o_ref):
    cc = c_ref[...]
    o_ref[0] = _dot3(_silu(cc), w_ref[0]) + b_ref[0]


def _ada(cc, w_ada, b_ada):
    tn = 1536
    n = w_ada.shape[-1]
    return pl.pallas_call(
        _ada_body,
        grid=(DEPTH, n // tn),
        in_specs=[
            pl.BlockSpec((8, D_MODEL), lambda l, j: (0, 0)),
            pl.BlockSpec((1, D_MODEL, tn), lambda l, j: (l, 0, j)),
            pl.BlockSpec((1, 1, tn), lambda l, j: (l, 0, j)),
        ],
        out_specs=pl.BlockSpec((1, 8, tn), lambda l, j: (l, 0, j)),
        out_shape=jax.ShapeDtypeStruct((DEPTH, 8, n), F32),
        compiler_params=_cparams(("arbitrary", "arbitrary"), big=True),
        name="adaln",
    )(cc, w_ada, b_ada.reshape(DEPTH, 1, n))


def _nmm_body(x_ref, ss_ref, g_ref, w_ref, *o_refs, offs, chunk):
    x = x_ref[0]
    ms = jnp.mean(x * x, axis=-1, keepdims=True)
    y = x * lax.rsqrt(ms + EPS) * g_ref[...]
    ss = ss_ref[0]
    y = y * (1.0 + ss[1:2, :]) + ss[0:1, :]
    yb = y.astype(BF16)
    for o_ref, off in zip(o_refs, offs):
        n = o_ref.shape[-1]
        for j in range(0, n, chunk):
            w = min(chunk, n - j)
            o_ref[0, :, j:j + w] = _dot(yb, w_ref[:, off + j:off + j + w]).astype(o_ref.dtype)


def _nmm(x, ss, g, w, widths, dtypes, tm, name):
    bsz, seq, d = x.shape
    offs = [int(v) for v in np.cumsum([0] + list(widths[:-1]))]
    return pl.pallas_call(
        functools.partial(_nmm_body, offs=offs, chunk=512),
        grid=(bsz, seq // tm),
        in_specs=[
            pl.BlockSpec((1, tm, d), lambda b, t: (b, t, 0)),
            pl.BlockSpec((1, 2, d), lambda b, t: (b, 0, 0)),
            pl.BlockSpec((1, d), lambda b, t: (0, 0)),
            pl.BlockSpec(w.shape, lambda b, t: (0, 0)),
        ],
        out_specs=[pl.BlockSpec((1, tm, n), lambda b, t: (b, t, 0)) for n in widths],
        out_shape=[jax.ShapeDtypeStruct((bsz, seq, n), dt) for n, dt in zip(widths, dtypes)],
        compiler_params=_cparams(("parallel", "arbitrary"), big=True),
        name=name,
    )(x, ss, g.reshape(1, d), w)


def _gla_body(qx, fx, vx, qc, fc, vc, lb_ref, ox, oc, st_ref, *, layer, reverse):
    t = pl.program_id(1)
    is_ctx = t == 0
    tb, ch = TOKEN_BLOCK, HG_CHUNK
    nch = tb // ch

    @pl.when(is_ctx)
    def _():
        st_ref[...] = jnp.zeros_like(st_ref)

    q = jnp.where(is_ctx, qc[0], qx[0]) * (HG_DK ** -0.5)
    a = jnp.where(is_ctx, fc[0], fx[0])
    v = jnp.where(is_ctx, vc[0], vx[0])

    lg = lb_ref[...]
    e = jnp.exp(lg - jnp.max(lg, axis=0, keepdims=True))
    den = jnp.sum(e, axis=0, keepdims=True)
    num = jnp.zeros_like(den)
    for r in range(1, layer + 1):
        num = num + e[r:r + 1, :]
    lb = num / den
    p = jnp.log(lb)
    qv = jnp.log(1.0 - lb) + _log_sigmoid(a)
    logf = jnp.maximum(p, qv) + jnp.log(1.0 + jnp.exp(-jnp.abs(p - qv)))
    k = 1.0 - jnp.exp(logf)

    ri = lax.broadcasted_iota(jnp.int32, (tb, tb), 0)
    ci = lax.broadcasted_iota(jnp.int32, (tb, tb), 1)
    same = (ri // ch) == (ci // ch)
    tri = (ci >= ri) if reverse else (ci <= ri)
    mask = jnp.logical_and(same, tri)
    mask_b = jnp.where(mask, 1.0, 0.0).astype(BF16)
    bdiag = (ri // HG_DK) == (ci // HG_DK)

    bcs = _dot_exact_lhs(mask_b, logf)
    b3 = bcs.reshape(nch, ch, tb)
    mid = ch // 2 if reverse else ch // 2 - 1
    end = 0 if reverse else ch - 1
    b_mid = b3[:, mid:mid + 1, :]
    b_end = b3[:, end:end + 1, :]
    q3 = q.reshape(nch, ch, tb)
    k3 = k.reshape(nch, ch, tb)
    d1 = b3 - b_mid
    qd = (q3 * jnp.exp(d1)).reshape(tb, tb)
    kd = (k3 * jnp.exp(-d1)).reshape(tb, tb).astype(BF16)
    kup = (k3 * jnp.exp(b_end - b3)).astype(BF16)
    qb = (q3 * jnp.exp(b3)).astype(BF16)
    dec = jnp.exp(b_end)
    v3 = v.astype(BF16).reshape(nch, ch, tb)

    lane_head = lax.broadcasted_iota(jnp.int32, (1, tb), 1) // HG_DK
    acc = jnp.zeros((tb, tb), F32)
    for h in range(HG_HEADS):
        sel = lane_head == h
        qh = jnp.where(sel, qd, 0.0).astype(BF16)
        s = _dot_nt(qh, kd)
        s = jnp.where(mask, s, 0.0).astype(BF16)
        vh = jnp.where(sel, v, 0.0).astype(BF16)
        acc = acc + _dot(s, vh)

    inter = [None] * nch
    order = range(nch - 1, -1, -1) if reverse else range(nch)
    for c in order:
        st = st_ref[...]
        inter[c] = _dot_nt(qb[c], st.astype(BF16))
        upd = _dot_tn(v3[c], kup[c])
        st_ref[...] = st * dec[c] + jnp.where(bdiag, upd, 0.0)
    o = acc + jnp.concatenate(inter, axis=0)

    @pl.when(is_ctx)
    def _():
        oc[0] = o

    @pl.when(jnp.logical_not(is_ctx))
    def _():
        ox[0] = o


def _scan_block_maps(nxb, reverse):
    if reverse:
        return lambda t: jnp.where(t == 0, nxb - 1, nxb - t)
    return lambda t: jnp.maximum(t - 1, 0)


def _gla(u_x, u_c, lb_logits, layer, reverse):
    bsz, seq, _ = u_x.shape
    tb = TOKEN_BLOCK
    nxb = seq // tb
    xb = _scan_block_maps(nxb, reverse)
    fcol = 2 if reverse else 1

    def xs(col):
        return pl.BlockSpec((1, tb, HG_WIDTH), lambda b, t: (b, xb(t), col))

    def cs(col):
        return pl.BlockSpec((1, tb, HG_WIDTH), lambda b, t: (b, 0, col))

    return pl.pallas_call(
        functools.partial(_gla_body, layer=layer, reverse=reverse),
        grid=(bsz, nxb + 1),
        in_specs=[xs(0), xs(fcol), xs(3), cs(0), cs(fcol), cs(3),
                  pl.BlockSpec((DEPTH, HG_WIDTH), lambda b, t: (0, 0))],
        out_specs=[pl.BlockSpec((1, tb, HG_WIDTH), lambda b, t: (b, xb(t), 0)),
                   pl.BlockSpec((1, tb, HG_WIDTH), lambda b, t: (b, 0, 0))],
        out_shape=[jax.ShapeDtypeStruct((bsz, seq, HG_WIDTH), F32),
                   jax.ShapeDtypeStruct((bsz, CTX_LEN, HG_WIDTH), F32)],
        scratch_shapes=[pltpu.VMEM((HG_WIDTH, HG_WIDTH), F32)],
        compiler_params=_cparams(("parallel", "arbitrary"), big=True),
        name="hgrn2_bwd" if reverse else "hgrn2_fwd",
    )(u_x, u_x, u_x, u_c, u_c, u_c, lb_logits)


def _ssd_body(xx, hp, hn, dtx, dttx, xc, dtc, dttc, cw, cb, dtb_e, alog_e, dtb_t, alog_t, dsk,
              ox, oc, st_ref, y_ref, *, reverse, nxb):
    t = pl.program_id(1)
    is_ctx = t == 0
    tb, ch = TOKEN_BLOCK, SSD_CHUNK
    nch = tb // ch
    d = 1 if reverse else 0
    xblk = (nxb - t) if reverse else (t - 1)

    @pl.when(is_ctx)
    def _():
        st_ref[...] = jnp.zeros_like(st_ref)

    raw = jnp.where(is_ctx, xc[0], xx[0])
    prev_ok = jnp.logical_and(jnp.logical_not(is_ctx), xblk > 0)
    next_ok = jnp.logical_and(jnp.logical_not(is_ctx), xblk < nxb - 1)
    prev_row = jnp.where(prev_ok, hp[0, 7:8, :], 0.0)
    next_row = jnp.where(next_ok, hn[0, 0:1, :], 0.0)
    xbc = _silu(_conv3(raw, prev_row, next_row, cw, cb))
    xs = xbc[:, :SSD_WIDTH]
    bm = xbc[:, SSD_WIDTH:SSD_WIDTH + 256].astype(BF16)
    cm = xbc[:, SSD_WIDTH + 256:].astype(BF16)

    ri = lax.broadcasted_iota(jnp.int32, (tb, tb), 0)
    ci = lax.broadcasted_iota(jnp.int32, (tb, tb), 1)
    same = (ri // ch) == (ci // ch)
    mask = jnp.logical_and(same, (ci >= ri) if reverse else (ci <= ri))
    mask_b = jnp.where(mask, 1.0, 0.0).astype(BF16)
    mask_t = jnp.logical_and(same, (ri >= ci) if reverse else (ri <= ci))
    mask_tb = jnp.where(mask_t, 1.0, 0.0).astype(BF16)

    dt_raw = jnp.where(is_ctx, dtc[0], dtx[0])
    ej = lax.broadcasted_iota(jnp.int32, (128, SSD_WIDTH), 0)
    el = lax.broadcasted_iota(jnp.int32, (128, SSD_WIDTH), 1)
    expand = jnp.where(ej == d * SSD_HEADS + el // SSD_HEAD_DIM, 1.0, 0.0).astype(BF16)
    dte = _softplus(_dot_exact_rhs(dt_raw, expand) + dtb_e[d:d + 1, :])
    a_e = -jnp.exp(alog_e[d:d + 1, :])
    acs = _dot_exact_lhs(mask_b, dte * a_e)
    xdt = xs * dte
    acs3 = acs.reshape(nch, ch, SSD_WIDTH)
    end = 0 if reverse else ch - 1
    a_end = acs3[:, end:end + 1, :]
    xw = (xdt.reshape(nch, ch, SSD_WIDTH) * jnp.exp(a_end - acs3)).astype(BF16)
    ea = jnp.exp(acs3)
    dec = jnp.exp(a_end)

    dtt_raw = jnp.where(is_ctx, dttc[0], dttx[0])[d * SSD_HEADS:(d + 1) * SSD_HEADS, :]
    dtt = _softplus(dtt_raw + dtb_t[d * SSD_HEADS:(d + 1) * SSD_HEADS, :])
    a_t = -jnp.exp(alog_t[d * SSD_HEADS:(d + 1) * SSD_HEADS, :])
    acs_t = _dot_exact_rhs(dtt * a_t, mask_tb)

    xdt_b = xdt.astype(BF16)
    hpg = SSD_HEADS // SSD_GROUPS
    for g in range(SSD_GROUPS):
        cbm = _dot_nt(cm[:, g * SSD_STATE:(g + 1) * SSD_STATE], bm[:, g * SSD_STATE:(g + 1) * SSD_STATE])
        for hh in range(hpg):
            h = g * hpg + hh
            lo = h * SSD_HEAD_DIM
            seg = acs[:, lo:lo + 1] - acs_t[h:h + 1, :]
            lm = jnp.exp(jnp.where(mask, seg, -1e30))
            wm = (cbm * lm).astype(BF16)
            y_ref[:, lo:lo + SSD_HEAD_DIM] = _dot(wm, xdt_b[:, lo:lo + SSD_HEAD_DIM])

    gw = SSD_WIDTH // SSD_GROUPS
    inter = [[None] * SSD_GROUPS for _ in range(nch)]
    order = range(nch - 1, -1, -1) if reverse else range(nch)
    for c in order:
        rows = slice(c * ch, (c + 1) * ch)
        for g in range(SSD_GROUPS):
            st = st_ref[g]
            cols = slice(g * gw, (g + 1) * gw)
            inter[c][g] = _dot(cm[rows, g * SSD_STATE:(g + 1) * SSD_STATE], st.astype(BF16)) * ea[c][:, cols]
            upd = _dot_tn(bm[rows, g * SSD_STATE:(g + 1) * SSD_STATE], xw[c][:, cols])
            st_ref[g] = st * dec[c][:, cols] + upd
    y_inter = jnp.concatenate([jnp.concatenate(r, axis=1) for r in inter], axis=0)
    o = y_ref[...] + y_inter
    if not reverse:
        o = o + dsk[...] * xs

    @pl.when(is_ctx)
    def _():
        oc[0] = o

    @pl.when(jnp.logical_not(is_ctx))
    def _():
        ox[0] = o


def _ssd(zx_x, zx_c, dt_x, dt_c, dtt_x, dtt_c, prm, reverse):
    bsz, seq, _ = zx_x.shape
    tb = TOKEN_BLOCK
    nxb = seq // tb
    xb = _scan_block_maps(nxb, reverse)
    hb = tb // 8
    full = lambda a: pl.BlockSpec(a.shape, lambda b, t: (0,) * a.ndim)
    params = [prm["cw"], prm["cb"], prm["dtb_e"], prm["alog_e"], prm["dtb_t"], prm["alog_t"], prm["dsk"]]
    return pl.pallas_call(
        functools.partial(_ssd_body, reverse=reverse, nxb=nxb),
        grid=(bsz, nxb + 1),
        in_specs=[
            pl.BlockSpec((1, tb, SSD_XBC), lambda b, t: (b, xb(t), 0)),
            pl.BlockSpec((1, 8, SSD_XBC), lambda b, t: (b, jnp.maximum(xb(t) * hb - 1, 0), 0)),
            pl.BlockSpec((1, 8, SSD_XBC), lambda b, t: (b, jnp.minimum((xb(t) + 1) * hb, nxb * hb - 1), 0)),
            pl.BlockSpec((1, tb, 128), lambda b, t: (b, xb(t), 0)),
            pl.BlockSpec((1, 16, tb), lambda b, t: (b, 0, xb(t))),
            pl.BlockSpec((1, tb, SSD_XBC), lambda b, t: (b, 0, 0)),
            pl.BlockSpec((1, tb, 128), lambda b, t: (b, 0, 0)),
            pl.BlockSpec((1, 16, tb), lambda b, t: (b, 0, 0)),
        ] + [full(a) for a in params],
        out_specs=[pl.BlockSpec((1, tb, SSD_WIDTH), lambda b, t: (b, xb(t), 0)),
                   pl.BlockSpec((1, tb, SSD_WIDTH), lambda b, t: (b, 0, 0))],
        out_shape=[jax.ShapeDtypeStruct((bsz, seq, SSD_WIDTH), F32),
                   jax.ShapeDtypeStruct((bsz, CTX_LEN, SSD_WIDTH), F32)],
        scratch_shapes=[pltpu.VMEM((SSD_GROUPS, SSD_STATE, SSD_WIDTH // SSD_GROUPS), F32),
                        pltpu.VMEM((tb, SSD_WIDTH), F32)],
        compiler_params=_cparams(("parallel", "arbitrary"), big=True),
        name="ssd_bwd" if reverse else "ssd_fwd",
    )(zx_x, zx_x, zx_x, dt_x, dtt_x, zx_c, dt_c, dtt_c, *params)


LANES = 128
HY_HALVES = HY_WIDTH // LANES


def _hy_pre_body(u, hp, hn, cw, cb, v_ref, x1_ref, x2_ref, *, nblk):
    t = pl.program_id(1)
    prev_row = jnp.where(t > 0, hp[0, 7:8, :], 0.0)
    next_row = jnp.where(t < nblk - 1, hn[0, 0:1, :], 0.0)
    z = _conv3(u[0], prev_row, next_row, cw, cb)
    for i, ref in enumerate((v_ref, x1_ref, x2_ref)):
        for hf in range(HY_HALVES):
            lo = i * HY_WIDTH + hf * LANES
            ref[hf, 0] = z[:, lo:lo + LANES]


def _hy_pre(u, cw, cb):
    bsz, seq, _ = u.shape
    tb = TOKEN_BLOCK
    nblk = seq // tb
    hb = tb // 8
    out = jax.ShapeDtypeStruct((HY_HALVES, bsz, seq, LANES), F32)
    ospec = pl.BlockSpec((HY_HALVES, 1, tb, LANES), lambda b, t: (0, b, t, 0))
    return pl.pallas_call(
        functools.partial(_hy_pre_body, nblk=nblk),
        grid=(bsz, nblk),
        in_specs=[
            pl.BlockSpec((1, tb, HY_COLS), lambda b, t: (b, t, 0)),
            pl.BlockSpec((1, 8, HY_COLS), lambda b, t: (b, jnp.maximum(t * hb - 1, 0), 0)),
            pl.BlockSpec((1, 8, HY_COLS), lambda b, t: (b, jnp.minimum((t + 1) * hb, nblk * hb - 1), 0)),
            pl.BlockSpec((3, HY_COLS), lambda b, t: (0, 0)),
            pl.BlockSpec((1, HY_COLS), lambda b, t: (0, 0)),
        ],
        out_specs=[ospec, ospec, ospec],
        out_shape=[out, out, out],
        compiler_params=_cparams(("parallel", "arbitrary")),
        name="hyena_shortconv",
    )(u, u, u, cw, cb)


def _filt_body(w1, b1, f1, w2, b2, f2, w3, delta, buf_ref, sa_ref, *, seq_len, tr):
    i = pl.program_id(0)
    jrow = i * tr + lax.broadcasted_iota(jnp.int32, (tr, 128), 0)
    pos = jnp.where(jrow < seq_len, jrow, 2 * seq_len - jrow).astype(F32)
    tpos = pos / (seq_len - 1.0)
    wpos = pos * (2.0 * math.pi / seq_len)
    lane = lax.broadcasted_iota(jnp.int32, (tr, 128), 1)
    bidx = jnp.where(lane <= HY_EMB_BANDS, lane - 1, lane - 1 - HY_EMB_BANDS).astype(F32)
    band = 1e-4 + bidx * ((HY_EMB_BANDS - 1 - 1e-4) / (HY_EMB_BANDS - 1))
    ang = band * wpos + jnp.where(lane > HY_EMB_BANDS, 0.5 * math.pi, 0.0)
    z = jnp.where(lane == 0, tpos, jnp.where(lane <= 2 * HY_EMB_BANDS, jnp.cos(ang), 0.0))
    h = jnp.sin(f1[...] * (_dot3(z, w1[...]) + b1[...]))
    h = jnp.sin(f2[...] * (_dot3(h, w2[...]) + b2[...]))
    o = _dot3(h, w3[0])
    o = o * jnp.exp(-tpos[:, 0:1] * delta[...])
    o = jnp.where(jrow[:, 0:1] == seq_len, 0.0, o)
    for order in range(2):
        for hf in range(HY_HALVES):
            lo = order * HY_WIDTH + hf * LANES
            buf_ref[order, hf] = o[:, lo:lo + LANES]

    @pl.when(i == 0)
    def _():
        sa_ref[...] = jnp.zeros_like(sa_ref)

    sa_ref[...] = sa_ref[...] + jnp.sum(jnp.abs(o), axis=0, keepdims=True)


def _filt_gen(p, seq_len, tr):
    n = 2 * seq_len
    assert seq_len % tr == 0
    half = seq_len // tr
    full = lambda a: pl.BlockSpec(a.shape, lambda i: (0,) * a.ndim)
    small = [p["w1"], p["b1"], p["f1"], p["w2"], p["b2"], p["f2"]]
    return pl.pallas_call(
        functools.partial(_filt_body, seq_len=seq_len, tr=tr),
        grid=(n // tr,),
        in_specs=[full(a) for a in small] + [
            pl.BlockSpec((1, HY_HIDDEN, 2 * HY_WIDTH), lambda i: (i // half, 0, 0)),
            full(p["delta"]),
        ],
        out_specs=[pl.BlockSpec((2, HY_HALVES, tr, LANES), lambda i: (0, 0, i, 0)),
                   pl.BlockSpec((8, 2 * HY_WIDTH), lambda i: (0, 0))],
        out_shape=[jax.ShapeDtypeStruct((2, HY_HALVES, n, LANES), F32),
                   jax.ShapeDtypeStruct((8, 2 * HY_WIDTH), F32)],
        compiler_params=_cparams(("arbitrary",)),
        name="hyena_filter",
    )(*small, p["w3"], p["delta"])


FFT_NB = 8


def _strided_rows(ref3, j, n):
    flat = ref3.reshape(ref3.shape[0] * FFT_NB, LANES)
    return flat[pl.ds(j, n, stride=FFT_NB), :]


def _store_strided_rows(ref3, j, val):
    flat = ref3.reshape(ref3.shape[0] * FFT_NB, LANES)
    flat[pl.ds(j, val.shape[0], stride=FFT_NB), :] = val


def _s1_body(x_ref, f_ref, t_ref, o_ref):
    rows = x_ref.shape[2]
    for j in range(FFT_NB):
        x = jnp.concatenate([_strided_rows(x_ref.at[0, hf], j, rows) for hf in range(HY_HALVES)], axis=1)
        a = _dot3_pre((f_ref[0], f_ref[1]), x)
        tr = t_ref[0, 0, :, j:j + 1]
        ti = t_ref[0, 1, :, j:j + 1]
        ar, ai = a[:FFT_N1], a[FFT_N1:]
        out = jnp.concatenate([ar * tr - ai * ti, ar * ti + ai * tr], axis=0)
        for hf in range(HY_HALVES):
            _store_strided_rows(o_ref.at[0, hf], j, out[:, hf * LANES:(hf + 1) * LANES])


def _fft_stage1(x, fmat, twid):
    g, _, rows, n2, _ = x.shape
    return pl.pallas_call(
        _s1_body,
        grid=(g, n2 // FFT_NB),
        in_specs=[
            pl.BlockSpec((1, HY_HALVES, rows, FFT_NB, LANES), lambda i, j: (i, 0, 0, j, 0)),
            pl.BlockSpec(fmat.shape, lambda i, j: (0, 0, 0)),
            pl.BlockSpec((1, 2, FFT_N1, FFT_NB), lambda i, j: (j, 0, 0, 0)),
        ],
        out_specs=pl.BlockSpec((1, HY_HALVES, 2 * FFT_N1, FFT_NB, LANES), lambda i, j: (i, 0, 0, j, 0)),
        out_shape=jax.ShapeDtypeStruct((g, HY_HALVES, 2 * FFT_N1, n2, LANES), F32),
        compiler_params=_cparams(("parallel", "arbitrary"), big=True),
        name="fft_stage1",
    )(x, fmat, twid)


def _load_complex(ref, k):
    return jnp.concatenate(
        [jnp.concatenate([ref[hf, 0, k], ref[hf, 1, k]], axis=0) for hf in range(HY_HALVES)], axis=1)


def _store_complex(ref, k, val):
    for hf in range(HY_HALVES):
        ref[hf, 0, k] = val[:FFT_N1, hf * LANES:(hf + 1) * LANES]
        ref[hf, 1, k] = val[FFT_N1:, hf * LANES:(hf + 1) * LANES]


def _s2_body(a_ref, h_ref, f_ref, fc_ref, o_ref, *, kb):
    for k in range(kb):
        x = _dot3_pre((f_ref[0], f_ref[1]), _load_complex(a_ref, k))
        xr, xi = x[:FFT_N1], x[FFT_N1:]
        h = _load_complex(h_ref.at[0], k)
        hr, hi = h[:FFT_N1], h[FFT_N1:]
        y = jnp.concatenate([xr * hr - xi * hi, xr * hi + xi * hr], axis=0)
        _store_complex(o_ref, k, _dot3_pre((fc_ref[0], fc_ref[1]), y))


def _fft_stage2(a, hspec, order, fmat, fmat_c, kb=8):
    blk = pl.BlockSpec((HY_HALVES, 2, kb, FFT_N1, LANES), lambda i: (0, 0, i, 0, 0))
    return pl.pallas_call(
        functools.partial(_s2_body, kb=kb),
        grid=(FFT_N1 // kb,),
        in_specs=[blk,
                  pl.BlockSpec((1, HY_HALVES, 2, kb, FFT_N1, LANES), lambda i: (order, 0, 0, i, 0, 0)),
                  pl.BlockSpec(fmat.shape, lambda i: (0, 0, 0)),
                  pl.BlockSpec(fmat_c.shape, lambda i: (0, 0, 0))],
        out_specs=blk,
        out_shape=jax.ShapeDtypeStruct(a.shape, F32),
        compiler_params=_cparams(("arbitrary",), big=True),
        name="fft_stage2_mul",
    )(a, hspec, fmat, fmat_c)


def _s2f_body(a_ref, sa_ref, f_ref, o_ref, *, kb, n):
    o = pl.program_id(0)
    sa = jnp.where(o == 0, sa_ref[0:1, :HY_WIDTH], sa_ref[0:1, HY_WIDTH:])
    scale = 1.0 / (sa * float(n))
    for k in range(kb):
        x = _dot3_pre((f_ref[0], f_ref[1]), _load_complex(a_ref.at[0], k))
        _store_complex(o_ref.at[0], k, x * scale)


def _fft_stage2_filter(a, sumabs, fmat, n, kb=8):
    blk = pl.BlockSpec((1, HY_HALVES, 2, kb, FFT_N1, LANES), lambda o, i: (o, 0, 0, i, 0, 0))
    return pl.pallas_call(
        functools.partial(_s2f_body, kb=kb, n=n),
        grid=(2, FFT_N1 // kb),
        in_specs=[blk, pl.BlockSpec(sumabs.shape, lambda o, i: (0, 0)),
                  pl.BlockSpec(fmat.shape, lambda o, i: (0, 0, 0))],
        out_specs=blk,
        out_shape=jax.ShapeDtypeStruct(a.shape, F32),
        compiler_params=_cparams(("parallel", "arbitrary"), big=True),
        name="fft_stage2_filter",
    )(a, sumabs, fmat)


def _s3_body(c_ref, t_ref, f_ref, gate_ref, v_ref, bias_ref, o_ref):
    rows = gate_ref.shape[1]
    for j in range(FFT_NB):
        tr = t_ref[0, 0, :, j:j + 1]
        ti = t_ref[0, 1, :, j:j + 1]
        c = jnp.concatenate([_strided_rows(c_ref.at[hf], j, 2 * FFT_N1) for hf in range(HY_HALVES)], axis=1)
        cr, ci = c[:FFT_N1], c[FFT_N1:]
        dmat = jnp.concatenate([cr * tr + ci * ti, ci * tr - cr * ti], axis=0)
        w = _dot3_pre((f_ref[0], f_ref[1]), dmat)
        for hf in range(HY_HALVES):
            lanes = slice(hf * LANES, (hf + 1) * LANES)
            g = _strided_rows(gate_ref.at[hf], j, rows)
            v = _strided_rows(v_ref.at[hf], j, rows)
            _store_strided_rows(o_ref.at[hf], j, g * (w[:, lanes] + v * bias_ref[:, lanes]))


def _fft_stage3(c, twid, fmat, gate, v, bias_row):
    _, rows, n2, _ = gate.shape
    dspec = pl.BlockSpec((HY_HALVES, rows, FFT_NB, LANES), lambda j: (0, 0, j, 0))
    return pl.pallas_call(
        _s3_body,
        grid=(n2 // FFT_NB,),
        in_specs=[
            pl.BlockSpec((HY_HALVES, 2 * FFT_N1, FFT_NB, LANES), lambda j: (0, 0, j, 0)),
            pl.BlockSpec((1, 2, FFT_N1, FFT_NB), lambda j: (j, 0, 0, 0)),
            pl.BlockSpec(fmat.shape, lambda j: (0, 0, 0)),
            dspec, dspec,
            pl.BlockSpec((1, HY_WIDTH), lambda j: (0, 0)),
        ],
        out_specs=dspec,
        out_shape=jax.ShapeDtypeStruct(gate.shape, F32),
        compiler_params=_cparams(("arbitrary",), big=True),
        name="fft_stage3_gate",
    )(c, twid, fmat, gate, v, bias_row)


def _hyc_body(v_ref, x1_ref, x2_ref, buf_ref, sa_ref, cs_ref, bias_ref, o_ref):
    n = 2 * CTX_LEN
    cmat, smat = cs_ref[0], cs_ref[1]
    c_in, s_in = cmat[:, :CTX_LEN], smat[:, :CTX_LEN]
    c_out, s_out = cmat[:CTX_LEN, :], smat[:CTX_LEN, :]
    batch = lambda ref, b: jnp.concatenate([ref[hf, b] for hf in range(HY_HALVES)], axis=1)
    yr, yi = batch(v_ref, 0), batch(v_ref, 1)
    for o, gate in enumerate((x1_ref, x2_ref)):
        bufo = jnp.concatenate([buf_ref[o, hf] for hf in range(HY_HALVES)], axis=1)
        scale = 1.0 / (sa_ref[0:1, o * HY_WIDTH:(o + 1) * HY_WIDTH] * float(n))
        hr = _dot3(cmat, bufo) * scale
        hi = -_dot3(smat, bufo) * scale
        xr = _dot3(c_in, yr) + _dot3(s_in, yi)
        xi = _dot3(c_in, yi) - _dot3(s_in, yr)
        zr = xr * hr - xi * hi
        zi = xr * hi + xi * hr
        wr = _dot3(c_out, zr) - _dot3(s_out, zi)
        wi = _dot3(c_out, zi) + _dot3(s_out, zr)
        b = bias_ref[o:o + 1, :]
        yr = batch(gate, 0) * (wr + yr * b)
        yi = batch(gate, 1) * (wi + yi * b)
    for hf in range(HY_HALVES):
        o_ref[hf, 0] = yr[:, hf * LANES:(hf + 1) * LANES]
        o_ref[hf, 1] = yi[:, hf * LANES:(hf + 1) * LANES]


def _hy_ctx(v, x1, x2, buf, sumabs, cs, bias):
    return pl.pallas_call(
        _hyc_body,
        out_shape=jax.ShapeDtypeStruct(v.shape, F32),
        compiler_params=pltpu.CompilerParams(vmem_limit_bytes=VMEM_LIMIT),
        name="hyena_ctx",
    )(v, x1, x2, buf, sumabs, cs, bias)


def _dft_constants(seq_len):
    n = 2 * seq_len
    n1 = FFT_N1
    assert n == n1 * n1
    idx = np.arange(n1, dtype=np.float64)
    th = 2.0 * np.pi * np.outer(idx, idx) / n1
    fr, fi = np.cos(th), -np.sin(th)
    half = n1 // 2

    def parts(m):
        m32 = jnp.asarray(m, F32)
        hi = m32.astype(BF16)
        mid = (m32 - hi.astype(F32)).astype(BF16)
        return jnp.stack([hi, mid])

    f1_data = np.block([[fr[:, :half], -fi[:, :half]], [fi[:, :half], fr[:, :half]]])
    f1_real = np.concatenate([fr, fi], axis=0)
    f2 = np.block([[fr, -fi], [fi, fr]])
    f2c = np.block([[fr, fi], [-fi, fr]])
    f3 = np.block([[fr[:half], fi[:half]], [-fi[:half], fr[:half]]])
    tw = 2.0 * np.pi * np.outer(idx, idx) / n
    twid = np.stack([np.cos(tw), -np.sin(tw)])
    twid = twid.reshape(2, n1, n1 // FFT_NB, FFT_NB).transpose(2, 0, 1, 3)
    return dict(f1_data=parts(f1_data), f1_real=parts(f1_real), f2=parts(f2), f2c=parts(f2c),
                f3=parts(f3), twid=jnp.asarray(twid, F32))


def _ctx_dft_constants():
    n = 2 * CTX_LEN
    idx = np.arange(n, dtype=np.float64)
    th = 2.0 * np.pi * np.outer(idx, idx) / n
    return jnp.asarray(np.stack([np.cos(th), np.sin(th)]), F32)


def _hyena_filter_params(hy_w1, hy_b1, hy_freq1, hy_w2, hy_b2, hy_freq2, hy_w3):
    w1 = jnp.zeros((128, HY_HIDDEN), F32).at[:hy_w1.shape[0]].set(hy_w1)
    w3 = hy_w3.reshape(HY_HIDDEN, 2, 2 * HY_WIDTH).transpose(1, 0, 2)
    deltas = np.abs(np.linspace(HY_MIN_DECAY, HY_MAX_DECAY, HY_WIDTH))
    delta = jnp.asarray(np.tile(deltas, 2)[None, :], F32)
    row = lambda a: a.reshape(1, -1)
    return dict(w1=w1, b1=row(hy_b1), f1=row(hy_freq1), w2=hy_w2, b2=row(hy_b2), f2=row(hy_freq2),
                w3=w3, delta=delta)


def _hyena_x(v, x1, x2, hspec, bias, consts):
    nh, bsz, seq, lanes = v.shape
    assert bsz == 2
    rows = bsz * (seq // FFT_N1)
    view = lambda a: a.reshape(nh, rows, FFT_N1, lanes)
    y = view(v)
    for o, gate in enumerate((x1, x2)):
        a = _fft_stage1(y.reshape(1, nh, rows, FFT_N1, lanes), consts["f1_data"], consts["twid"])
        c = _fft_stage2(a.reshape(nh, 2, FFT_N1, FFT_N1, lanes), hspec, o, consts["f2"], consts["f2c"])
        y = _fft_stage3(c.reshape(nh, 2 * FFT_N1, FFT_N1, lanes), consts["twid"], consts["f3"], view(gate), y,
                        bias[o].reshape(1, nh * lanes))
    return y.reshape(nh, bsz, seq, lanes)


def _out_body(of, ob, ug, hy, yf, yb, zz, hgn, ssn, w_ref, h_ref, gate_ref, o_ref):
    o = of[0] + ob[0]
    gi = lax.broadcasted_iota(jnp.int32, (HG_WIDTH, HG_WIDTH), 0) // HG_DK
    gj = lax.broadcasted_iota(jnp.int32, (HG_WIDTH, HG_WIDTH), 1) // HG_DK
    avg = jnp.where(gi == gj, 1.0 / HG_DK, 0.0).astype(BF16)
    sq = o * o
    sh = sq.astype(BF16)
    sl = (sq - sh.astype(F32)).astype(BF16)
    ms = _dot(sh, avg) + _dot(sl, avg)
    hg = o * lax.rsqrt(ms + EPS) * hgn[...] * _silu(ug[0])
    y = (yf[0] + yb[0]) * _silu(zz[0])
    gw = SSD_WIDTH // SSD_GROUPS
    parts = []
    for g in range(SSD_GROUPS):
        yg = y[:, g * gw:(g + 1) * gw]
        parts.append(yg * lax.rsqrt(jnp.mean(yg * yg, axis=-1, keepdims=True) + EPS))
    ys = jnp.concatenate(parts, axis=1) * ssn[...]
    acc = _dot(hg.astype(BF16), w_ref[0:HG_WIDTH, :])
    hyv = jnp.concatenate([hy[hf, 0] for hf in range(HY_HALVES)], axis=1)
    acc = acc + _dot(hyv.astype(BF16), w_ref[HG_WIDTH:HG_WIDTH + HY_WIDTH, :])
    acc = acc + _dot(ys.astype(BF16), w_ref[HG_WIDTH + HY_WIDTH:, :])
    o_ref[0] = h_ref[0] + gate_ref[0] * acc


def _out_proj(of, ob, u_hg, hy, yf, yb, u_zx, hgn, ssn, w, h, gate, tm):
    bsz, seq, d = h.shape
    tok = lambda n, col=0: pl.BlockSpec((1, tm, n), lambda b, t: (b, t, col))
    return pl.pallas_call(
        _out_body,
        grid=(bsz, seq // tm),
        in_specs=[tok(HG_WIDTH), tok(HG_WIDTH), tok(HG_WIDTH, 4),
                  pl.BlockSpec((HY_HALVES, 1, tm, LANES), lambda b, t: (0, b, t, 0)),
                  tok(SSD_WIDTH), tok(SSD_WIDTH), tok(SSD_WIDTH, 2),
                  pl.BlockSpec((1, HG_WIDTH), lambda b, t: (0, 0)),
                  pl.BlockSpec((1, SSD_WIDTH), lambda b, t: (0, 0)),
                  pl.BlockSpec(w.shape, lambda b, t: (0, 0)),
                  tok(d),
                  pl.BlockSpec((1, 1, d), lambda b, t: (b, 0, 0))],
        out_specs=tok(d),
        out_shape=jax.ShapeDtypeStruct(h.shape, F32),
        compiler_params=_cparams(("parallel", "arbitrary"), big=True),
        name="mixer_out_proj",
    )(of, ob, u_hg, hy, yf, yb, u_zx, hgn.reshape(1, -1), ssn.reshape(1, -1), w, h, gate)


def _ffn_tail_body(*refs, tm, grid_w, vertical, final, cchunk):
    if vertical:
        a_ref, hp, hn, up_ref, cw, cb, wd, h_ref, gate_ref = refs[:9]
        rest = refs[9:]
    else:
        a_ref, up_ref, cw, cb, wd, h_ref, gate_ref = refs[:7]
        rest = refs[7:]
    if final:
        fg_ref, o_ref, act_ref = rest
    else:
        o_ref, act_ref = rest
    t = pl.program_id(1)
    nt = pl.num_programs(1)
    pad = grid_w if vertical else 0
    ext_rows = tm + 2 * pad
    col = lax.broadcasted_iota(jnp.int32, (ext_rows, cchunk), 0) % grid_w
    for j in range(0, D_FF, cchunk):
        cs = slice(j, j + cchunk)
        a = a_ref[0, :, cs].astype(F32)
        if vertical:
            top = jnp.where(t > 0, hp[0, :, cs].astype(F32), 0.0)
            bot = jnp.where(t < nt - 1, hn[0, :, cs].astype(F32), 0.0)
            ext = jnp.concatenate([top, a, bot], axis=0)
        else:
            ext = a
        left = jnp.where(col != 0, pltpu.roll(ext, 1, 0), 0.0)
        right = jnp.where(col != grid_w - 1, pltpu.roll(ext, ext_rows - 1, 0), 0.0)
        acc = jnp.zeros((tm, cchunk), F32) + cb[:, cs]
        for dy in ((-1, 0, 1) if vertical else (0,)):
            r0 = pad + dy * grid_w
            ky = dy + 1
            acc = acc + left[r0:r0 + tm] * cw[3 * ky:3 * ky + 1, cs]
            acc = acc + ext[r0:r0 + tm] * cw[3 * ky + 1:3 * ky + 2, cs]
            acc = acc + right[r0:r0 + tm] * cw[3 * ky + 2:3 * ky + 3, cs]
        act_ref[:, cs] = (_silu(acc) * up_ref[0, :, cs].astype(F32)).astype(BF16)
    y = _dot(act_ref[...], wd[...])
    x = h_ref[0] + gate_ref[0] * y
    if final:
        ms = jnp.mean(x * x, axis=-1, keepdims=True)
        x = x * lax.rsqrt(ms + EPS) * fg_ref[...]
    o_ref[0] = x


def _ffn_tail(gu, cw, cb, wd, h, gate, tm, grid_w, vertical, final_g=None):
    bsz, seq, d = h.shape
    nhb = tm // grid_w
    tok = lambda n, col=0: pl.BlockSpec((1, tm, n), lambda b, t: (b, t, col))
    full = lambda a: pl.BlockSpec(a.shape, lambda b, t: (0,) * a.ndim)
    in_specs = [tok(D_FF)]
    args = [gu]
    if vertical:
        nrow = seq // grid_w
        in_specs += [
            pl.BlockSpec((1, grid_w, D_FF), lambda b, t: (b, jnp.maximum(t * nhb - 1, 0), 0)),
            pl.BlockSpec((1, grid_w, D_FF), lambda b, t: (b, jnp.minimum((t + 1) * nhb, nrow - 1), 0)),
        ]
        args += [gu, gu]
    in_specs += [tok(D_FF, 1), full(cw), full(cb), full(wd), tok(d),
                 pl.BlockSpec((1, 1, d), lambda b, t: (b, 0, 0))]
    args += [gu, cw, cb, wd, h, gate]
    if final_g is not None:
        in_specs.append(pl.BlockSpec((1, d), lambda b, t: (0, 0)))
        args.append(final_g.reshape(1, d))
    return pl.pallas_call(
        functools.partial(_ffn_tail_body, tm=tm, grid_w=grid_w, vertical=vertical,
                          final=final_g is not None, cchunk=256),
        grid=(bsz, seq // tm),
        in_specs=in_specs,
        out_specs=tok(d),
        out_shape=jax.ShapeDtypeStruct(h.shape, F32),
        scratch_shapes=[pltpu.VMEM((tm, D_FF), BF16)],
        compiler_params=_cparams(("parallel", "arbitrary"), big=True),
        name="ffn_tail",
    )(*args)


def kernel(x, c, ctx, c_ctx, w_ada, b_ada, norm1_g, norm2_g, w_in, w_out, hg_lb_logits, hg_norm_g, hy_conv_w, hy_conv_b, hy_w1, hy_b1, hy_freq1, hy_w2, hy_b2, hy_freq2, hy_w3, hy_bias, ssd_conv_w, ssd_conv_b, ssd_dt_bias, ssd_a_log, ssd_d, ssd_norm_g, ffn_w_gate, ffn_w_up, ffn_conv_w, ffn_conv_b, ffn_w_down, final_norm_g):
    bsz, seq, d = x.shape
    cc = jnp.zeros((8, d), F32).at[:bsz].set(c).at[bsz].set(c_ctx)
    mods = _ada(cc, w_ada, b_ada)

    consts = _dft_constants(seq)
    cs_ctx = _ctx_dft_constants()
    lb_logits = hg_lb_logits.astype(F32)

    z0 = HG_COLS + HY_COLS
    for l in range(DEPTH):
        last = l == DEPTH - 1
        m = mods[l].reshape(8, 6, d)
        mx = m[:bsz]
        mc = jnp.broadcast_to(m[bsz:bsz + 1], (bsz, 6, d))

        wl = w_in[l]
        w_cat = jnp.concatenate([
            wl[:, :z0], wl[:, z0 + SSD_WIDTH:z0 + SSD_WIDTH + SSD_XBC], wl[:, z0:z0 + SSD_WIDTH],
            wl[:, z0 + SSD_WIDTH + SSD_XBC:], jnp.zeros((d, 128 - 2 * SSD_HEADS), F32)], axis=1).astype(BF16)
        widths = (HG_COLS, HY_COLS, SSD_XBC + SSD_WIDTH, 128)
        dts = (F32,) * 4
        uhg_x, uhy_x, uzx_x, dt_x = _nmm(x, mx[:, 0:2], norm1_g[l], w_cat, widths, dts, 512, "in_proj_x")
        uhg_c, uhy_c, uzx_c, dt_c = _nmm(ctx, mc[:, 0:2], norm1_g[l], w_cat, widths, dts, CTX_LEN, "in_proj_ctx")

        of_x, of_c = _gla(uhg_x, uhg_c, lb_logits, l, False)
        ob_x, ob_c = _gla(uhg_x, uhg_c, lb_logits, l, True)

        rep = lambda a: jnp.repeat(a, SSD_HEAD_DIM, axis=-1)
        prm = dict(cw=ssd_conv_w[l], cb=ssd_conv_b[l].reshape(1, -1),
                   dtb_e=rep(ssd_dt_bias[l]), alog_e=rep(ssd_a_log[l]),
                   dtb_t=ssd_dt_bias[l].reshape(-1, 1), alog_t=ssd_a_log[l].reshape(-1, 1),
                   dsk=rep(ssd_d[l]).reshape(1, -1))
        dtt_x = jnp.swapaxes(dt_x[:, :, :2 * SSD_HEADS], 1, 2)
        dtt_c = jnp.swapaxes(dt_c[:, :, :2 * SSD_HEADS], 1, 2)
        yf_x, yf_c = _ssd(uzx_x, uzx_c, dt_x, dt_c, dtt_x, dtt_c, prm, False)
        yb_x, yb_c = _ssd(uzx_x, uzx_c, dt_x, dt_c, dtt_x, dtt_c, prm, True)

        fp = _hyena_filter_params(hy_w1[l], hy_b1[l], hy_freq1[l], hy_w2[l], hy_b2[l], hy_freq2[l], hy_w3[l])
        cw_hy, cb_hy = hy_conv_w[l], hy_conv_b[l].reshape(1, -1)
        v_x, x1_x, x2_x = _hy_pre(uhy_x, cw_hy, cb_hy)
        buf, sumabs = _filt_gen(fp, seq, 1024)
        fa = _fft_stage1(buf.reshape(2, HY_HALVES, FFT_N1, FFT_N1, LANES), consts["f1_real"], consts["twid"])
        hspec = _fft_stage2_filter(fa.reshape(2, HY_HALVES, 2, FFT_N1, FFT_N1, LANES), sumabs, consts["f2"], 2 * seq)
        ohy_x = _hyena_x(v_x, x1_x, x2_x, hspec, hy_bias[l], consts)

        wo = w_out[l].astype(BF16)
        x = _out_proj(of_x, ob_x, uhg_x, ohy_x, yf_x, yb_x, uzx_x, hg_norm_g[l], ssd_norm_g[l], wo, x,
                      mx[:, 2:3], 512)

        w_gu = jnp.concatenate([ffn_w_gate[l], ffn_w_up[l]], axis=1).astype(BF16)
        wd = ffn_w_down[l].astype(BF16)
        cw_f = ffn_conv_w[l].reshape(9, D_FF)
        cb_f = ffn_conv_b[l].reshape(1, D_FF)
        if not last:
            v_c, x1_c, x2_c = _hy_pre(uhy_c, cw_hy, cb_hy)
            buf_c, sumabs_c = _filt_gen(fp, CTX_LEN, CTX_LEN)
            ohy_c = _hy_ctx(v_c, x1_c, x2_c, buf_c, sumabs_c, cs_ctx, hy_bias[l])
            ctx = _out_proj(of_c, ob_c, uhg_c, ohy_c, yf_c, yb_c, uzx_c, hg_norm_g[l], ssd_norm_g[l], wo, ctx,
                            mc[:, 2:3], CTX_LEN)
            (gu_c,) = _nmm(ctx, mc[:, 3:5], norm2_g[l], w_gu, (2 * D_FF,), (BF16,), CTX_LEN, "ffn_up_ctx")
            ctx = _ffn_tail(gu_c, cw_f, cb_f, wd, ctx, mc[:, 5:6], CTX_LEN, CTX_LEN, False)

        (gu_x,) = _nmm(x, mx[:, 3:5], norm2_g[l], w_gu, (2 * D_FF,), (BF16,), 512, "ffn_up_x")
        x = _ffn_tail(gu_x, cw_f, cb_f, wd, x, mx[:, 5:6], 512, GRID_W, True,
                      final_g=final_norm_g if last else None)
    return x
```

```python
import functools
import math

import numpy as np
import jax
import jax.numpy as jnp
from jax import lax
from jax.experimental import pallas as pl
from jax.experimental.pallas import tpu as pltpu

F32 = jnp.float32
BF16 = jnp.bfloat16

D_MODEL = 1024
DEPTH = 2
CTX_LEN = 256
GRID_W = 64
EPS = 1e-6

HG_HEADS = 4
HG_DK = 64
HG_WIDTH = 256
HG_CHUNK = 32

HY_WIDTH = 256
HY_EMB_BANDS = 16
HY_HIDDEN = 64
HY_MIN_DECAY = math.log(1e-2) / 1.5
HY_MAX_DECAY = math.log(1e-2) / 0.3

SSD_HEADS = 8
SSD_HEAD_DIM = 64
SSD_WIDTH = 512
SSD_GROUPS = 2
SSD_STATE = 128
SSD_XBC = 1024
SSD_CHUNK = 64

D_FF = 2816
HG_COLS = 5 * HG_WIDTH
HY_COLS = 3 * HY_WIDTH

LANES = 128
HY_HALVES = HY_WIDTH // LANES
TOKEN_BLOCK = 256
FFT_N1 = 128
VMEM_LIMIT = 56 * 1024 * 1024


def _cparams(sem, big=False):
    kw = dict(dimension_semantics=sem)
    if big:
        kw["vmem_limit_bytes"] = VMEM_LIMIT
    return pltpu.CompilerParams(**kw)


def _dot(a, b):
    return lax.dot_general(a, b, (((1,), (0,)), ((), ())), preferred_element_type=F32)


def _dot_nt(a, b):
    return lax.dot_general(a, b, (((1,), (1,)), ((), ())), preferred_element_type=F32)


def _dot_tn(a, b):
    return lax.dot_general(a, b, (((0,), (0,)), ((), ())), preferred_element_type=F32)


def _split3(a):
    hi = a.astype(BF16)
    r = a - hi.astype(F32)
    mid = r.astype(BF16)
    lo = (r - mid.astype(F32)).astype(BF16)
    return hi, mid, lo


def _dot_exact_lhs(mask_bf16, a):
    h, m, l = _split3(a)
    return _dot(mask_bf16, h) + _dot(mask_bf16, m) + _dot(mask_bf16, l)


def _dot_exact_rhs(a, mask_bf16):
    h, m, l = _split3(a)
    return _dot(h, mask_bf16) + _dot(m, mask_bf16) + _dot(l, mask_bf16)


def _dot3(a, b):
    ah, am, _ = _split3(a)
    bh, bm, _ = _split3(b)
    return _dot(ah, bh) + _dot(ah, bm) + _dot(am, bh)


def _dot3_pre(fparts, b):
    fh, fm = fparts
    bh, bm, _ = _split3(b)
    return _dot(fh, bh) + _dot(fh, bm) + _dot(fm, bh)


def _sigmoid(x):
    return 1.0 / (1.0 + jnp.exp(-x))


def _silu(x):
    return x * _sigmoid(x)


def _softplus(x):
    return jnp.maximum(x, 0.0) + jnp.log(1.0 + jnp.exp(-jnp.abs(x)))


def _log_sigmoid(x):
    return jnp.minimum(x, 0.0) - jnp.log(1.0 + jnp.exp(-jnp.abs(x)))


def _ada_body(c_ref, w_ref, b_ref, o_ref):
    cc = c_ref[...]
    o_ref[0] = _dot3(_silu(cc), w_ref[0]) + b_ref[0]


def _ada(cc, w_ada, b_ada):
    tn = 1536
    n = w_ada.shape[-1]
    return pl.pallas_call(
        _ada_body,
        grid=(DEPTH, n // tn),
        in_specs=[
            pl.BlockSpec((8, D_MODEL), lambda l, j: (0, 0)),
            pl.BlockSpec((1, D_MODEL, tn), lambda l, j: (l, 0, j)),
            pl.BlockSpec((1, 1, tn), lambda l, j: (l, 0, j)),
        ],
        out_specs=pl.BlockSpec((1, 8, tn), lambda l, j: (l, 0, j)),
        out_shape=jax.ShapeDtypeStruct((DEPTH, 8, n), F32),
        compiler_params=_cparams(("arbitrary", "arbitrary"), big=True),
        name="adaln",
    )(cc, w_ada, b_ada.reshape(DEPTH, 1, n))


CONV_C0 = HG_COLS
CONV_C1 = HG_COLS + HY_COLS + SSD_XBC
CONV_CHUNK = 256


def _inproj_body(x_ref, hp_ref, hn_ref, ss_ref, g_ref, w_ref, cw_ref, cb_ref,
                 uhg_ref, v_ref, x1_ref, x2_ref, xbc_ref, z_ref, dt_ref, *, nt):
    t = pl.program_id(1)
    ss = ss_ref[0]

    def normmod(x):
        ms = jnp.mean(x * x, axis=-1, keepdims=True)
        y = x * lax.rsqrt(ms + EPS) * g_ref[...]
        return (y * (1.0 + ss[1:2, :]) + ss[0:1, :]).astype(BF16)

    yb = normmod(x_ref[0])
    yh = normmod(jnp.concatenate([hp_ref[0], hn_ref[0]], axis=0))
    for j in range(0, HG_COLS, 512):
        w = min(512, HG_COLS - j)
        uhg_ref[0, :, j:j + w] = _dot(yb, w_ref[:, j:j + w])
    z_ref[0] = _dot(yb, w_ref[:, CONV_C1:CONV_C1 + SSD_WIDTH])
    dt_ref[0] = _dot(yb, w_ref[:, CONV_C1 + SSD_WIDTH:])

    rows = yb.shape[0]
    ri = lax.broadcasted_iota(jnp.int32, (rows, CONV_CHUNK), 0)
    hy_refs = (v_ref, x1_ref, x2_ref)
    for k in range((CONV_C1 - CONV_C0) // CONV_CHUNK):
        cols = slice(CONV_C0 + k * CONV_CHUNK, CONV_C0 + (k + 1) * CONV_CHUNK)
        ccols = slice(k * CONV_CHUNK, (k + 1) * CONV_CHUNK)
        u = _dot(yb, w_ref[:, cols])
        uh = _dot(yh, w_ref[:, cols])
        prev_row = jnp.where(t > 0, uh[7:8, :], 0.0)
        next_row = jnp.where(t < nt - 1, uh[8:9, :], 0.0)
        dn = jnp.where(ri == 0, prev_row, pltpu.roll(u, 1, 0))
        up = jnp.where(ri == rows - 1, next_row, pltpu.roll(u, rows - 1, 0))
        cv = dn * cw_ref[0:1, ccols] + u * cw_ref[1:2, ccols] + up * cw_ref[2:3, ccols] + cb_ref[:, ccols]
        if k < HY_COLS // CONV_CHUNK:
            for hf in range(HY_HALVES):
                hy_refs[k][hf, 0] = cv[:, hf * LANES:(hf + 1) * LANES]
        else:
            kk = k - HY_COLS // CONV_CHUNK
            xbc_ref[0, :, kk * CONV_CHUNK:(kk + 1) * CONV_CHUNK] = _silu(cv)


def _inproj(x, ss, g, w, cw, cb, tm, name):
    bsz, seq, d = x.shape
    nt = seq // tm
    hb = tm // 8
    tok = lambda n: pl.BlockSpec((1, tm, n), lambda b, t: (b, t, 0))
    hy_spec = pl.BlockSpec((HY_HALVES, 1, tm, LANES), lambda b, t: (0, b, t, 0))
    hy_shape = jax.ShapeDtypeStruct((HY_HALVES, bsz, seq, LANES), F32)
    tshape = lambda n: jax.ShapeDtypeStruct((bsz, seq, n), F32)
    return pl.pallas_call(
        functools.partial(_inproj_body, nt=nt),
        grid=(bsz, nt),
        in_specs=[
            tok(d),
            pl.BlockSpec((1, 8, d), lambda b, t: (b, jnp.maximum(t * hb - 1, 0), 0)),
            pl.BlockSpec((1, 8, d), lambda b, t: (b, jnp.minimum((t + 1) * hb, nt * hb - 1), 0)),
            pl.BlockSpec((1, 2, d), lambda b, t: (b, 0, 0)),
            pl.BlockSpec((1, d), lambda b, t: (0, 0)),
            pl.BlockSpec(w.shape, lambda b, t: (0, 0)),
            pl.BlockSpec(cw.shape, lambda b, t: (0, 0)),
            pl.BlockSpec(cb.shape, lambda b, t: (0, 0)),
        ],
        out_specs=[tok(HG_COLS), hy_spec, hy_spec, hy_spec, tok(SSD_XBC), tok(SSD_WIDTH), tok(128)],
        out_shape=[tshape(HG_COLS), hy_shape, hy_shape, hy_shape, tshape(SSD_XBC), tshape(SSD_WIDTH), tshape(128)],
        compiler_params=_cparams(("parallel", "arbitrary"), big=True),
        name=name,
    )(x, x, x, ss, g.reshape(1, d), w, cw, cb)


def _gla_dir(q, a, v, p, r1, st_ref, reverse):
    tb, ch = TOKEN_BLOCK, HG_CHUNK
    nch = tb // ch
    qv = r1 + _log_sigmoid(a)
    logf = jnp.maximum(p, qv) + jnp.log(1.0 + jnp.exp(-jnp.abs(p - qv)))
    k = 1.0 - jnp.exp(logf)

    ri = lax.broadcasted_iota(jnp.int32, (tb, tb), 0)
    ci = lax.broadcasted_iota(jnp.int32, (tb, tb), 1)
    same = (ri // ch) == (ci // ch)
    tri = (ci >= ri) if reverse else (ci <= ri)
    mask = jnp.logical_and(same, tri)
    mask_b = jnp.where(mask, 1.0, 0.0).astype(BF16)
    bdiag = (ri // HG_DK) == (ci // HG_DK)

    bcs = _dot_exact_lhs(mask_b, logf)
    b3 = bcs.reshape(nch, ch, tb)
    mid = ch // 2 if reverse else ch // 2 - 1
    end = 0 if reverse else ch - 1
    b_mid = b3[:, mid:mid + 1, :]
    b_end = b3[:, end:end + 1, :]
    q3 = q.reshape(nch, ch, tb)
    k3 = k.reshape(nch, ch, tb)
    d1 = b3 - b_mid
    qd = (q3 * jnp.exp(d1)).reshape(tb, tb)
    kd = (k3 * jnp.exp(-d1)).reshape(tb, tb).astype(BF16)
    kup = (k3 * jnp.exp(b_end - b3)).astype(BF16)
    qb = (q3 * jnp.exp(b3)).astype(BF16)
    dec = jnp.exp(b_end)
    v3 = v.astype(BF16).reshape(nch, ch, tb)

    lane_head = lax.broadcasted_iota(jnp.int32, (1, tb), 1) // HG_DK
    acc = jnp.zeros((tb, tb), F32)
    for h in range(HG_HEADS):
        sel = lane_head == h
        qh = jnp.where(sel, qd, 0.0).astype(BF16)
        s = _dot_nt(qh, kd)
        s = jnp.where(mask, s, 0.0).astype(BF16)
        vh = jnp.where(sel, v, 0.0).astype(BF16)
        acc = acc + _dot(s, vh)

    inter = [None] * nch
    order = range(nch - 1, -1, -1) if reverse else range(nch)
    st = st_ref[...]
    for c in order:
        inter[c] = _dot_nt(qb[c], st.astype(BF16))
        upd = _dot_tn(v3[c], kup[c])
        st = st * dec[c] + jnp.where(bdiag, upd, 0.0)
    st_ref[...] = st
    return acc + jnp.concatenate(inter, axis=0)


def _gla_body(qf, ff, vf, qb, fb, vb, qc, ffc, fbc, vc, lb_ref, of_x, ob_x, of_c, ob_c, stf, stb, *, layer):
    t = pl.program_id(1)
    is_ctx = t == 0

    @pl.when(is_ctx)
    def _():
        stf[...] = jnp.zeros_like(stf)
        stb[...] = jnp.zeros_like(stb)

    lg = lb_ref[...]
    e = jnp.exp(lg - jnp.max(lg, axis=0, keepdims=True))
    den = jnp.sum(e, axis=0, keepdims=True)
    num = jnp.zeros_like(den)
    for r in range(1, layer + 1):
        num = num + e[r:r + 1, :]
    lb = num / den
    p = jnp.log(lb)
    r1 = jnp.log(1.0 - lb)

    scale = HG_DK ** -0.5
    o_f = _gla_dir(jnp.where(is_ctx, qc[0], qf[0]) * scale, jnp.where(is_ctx, ffc[0], ff[0]),
                   jnp.where(is_ctx, vc[0], vf[0]), p, r1, stf, False)
    o_b = _gla_dir(jnp.where(is_ctx, qc[0], qb[0]) * scale, jnp.where(is_ctx, fbc[0], fb[0]),
                   jnp.where(is_ctx, vc[0], vb[0]), p, r1, stb, True)

    @pl.when(is_ctx)
    def _():
        of_c[0] = o_f.astype(of_c.dtype)
        ob_c[0] = o_b.astype(ob_c.dtype)

    @pl.when(jnp.logical_not(is_ctx))
    def _():
        of_x[0] = o_f.astype(of_x.dtype)
        ob_x[0] = o_b.astype(ob_x.dtype)


def _scan_block_maps(nxb):
    fwd = lambda t: jnp.maximum(t - 1, 0)
    bwd = lambda t: jnp.where(t == 0, nxb - 1, nxb - t)
    return fwd, bwd


def _gla(u_x, u_c, lb_logits, layer):
    bsz, seq, _ = u_x.shape
    tb = TOKEN_BLOCK
    nxb = seq // tb
    xf, xb = _scan_block_maps(nxb)

    def xs(blk, col):
        return pl.BlockSpec((1, tb, HG_WIDTH), lambda b, t: (b, blk(t), col))

    def cs(col):
        return pl.BlockSpec((1, tb, HG_WIDTH), lambda b, t: (b, 0, col))

    out_x = jax.ShapeDtypeStruct((bsz, seq, HG_WIDTH), BF16)
    out_c = jax.ShapeDtypeStruct((bsz, CTX_LEN, HG_WIDTH), BF16)
    return pl.pallas_call(
        functools.partial(_gla_body, layer=layer),
        grid=(bsz, nxb + 1),
        in_specs=[xs(xf, 0), xs(xf, 1), xs(xf, 3), xs(xb, 0), xs(xb, 2), xs(xb, 3),
                  cs(0), cs(1), cs(2), cs(3),
                  pl.BlockSpec((DEPTH, HG_WIDTH), lambda b, t: (0, 0))],
        out_specs=[xs(xf, 0), xs(xb, 0), cs(0), cs(0)],
        out_shape=[out_x, out_x, out_c, out_c],
        scratch_shapes=[pltpu.VMEM((HG_WIDTH, HG_WIDTH), F32), pltpu.VMEM((HG_WIDTH, HG_WIDTH), F32)],
        compiler_params=_cparams(("parallel", "arbitrary"), big=True),
        name="hgrn2_scan",
    )(u_x, u_x, u_x, u_x, u_x, u_x, u_c, u_c, u_c, u_c, lb_logits)


def _ssd_dir(xbc, dt_raw, dtt_all, dtb_e, alog_e, dtb_t, alog_t, dsk, st_ref, y_ref, reverse):
    tb, ch = TOKEN_BLOCK, SSD_CHUNK
    nch = tb // ch
    d = 1 if reverse else 0
    xs = xbc[:, :SSD_WIDTH]
    bm = xbc[:, SSD_WIDTH:SSD_WIDTH + 256].astype(BF16)
    cm = xbc[:, SSD_WIDTH + 256:].astype(BF16)

    ri = lax.broadcasted_iota(jnp.int32, (tb, tb), 0)
    ci = lax.broadcasted_iota(jnp.int32, (tb, tb), 1)
    same = (ri // ch) == (ci // ch)
    mask = jnp.logical_and(same, (ci >= ri) if reverse else (ci <= ri))
    mask_b = jnp.where(mask, 1.0, 0.0).astype(BF16)
    mask_t = jnp.logical_and(same, (ri >= ci) if reverse else (ri <= ci))
    mask_tb = jnp.where(mask_t, 1.0, 0.0).astype(BF16)

    ej = lax.broadcasted_iota(jnp.int32, (128, SSD_WIDTH), 0)
    el = lax.broadcasted_iota(jnp.int32, (128, SSD_WIDTH), 1)
    expand = jnp.where(ej == d * SSD_HEADS + el // SSD_HEAD_DIM, 1.0, 0.0).astype(BF16)
    dte = _softplus(_dot_exact_rhs(dt_raw, expand) + dtb_e[d:d + 1, :])
    a_e = -jnp.exp(alog_e[d:d + 1, :])
    acs = _dot_exact_lhs(mask_b, dte * a_e)
    xdt = xs * dte
    acs3 = acs.reshape(nch, ch, SSD_WIDTH)
    end = 0 if reverse else ch - 1
    a_end = acs3[:, end:end + 1, :]
    xw = (xdt.reshape(nch, ch, SSD_WIDTH) * jnp.exp(a_end - acs3)).astype(BF16)
    ea = jnp.exp(acs3)
    dec = jnp.exp(a_end)

    dtt_raw = dtt_all[d * SSD_HEADS:(d + 1) * SSD_HEADS, :]
    dtt = _softplus(dtt_raw + dtb_t[d * SSD_HEADS:(d + 1) * SSD_HEADS, :])
    a_t = -jnp.exp(alog_t[d * SSD_HEADS:(d + 1) * SSD_HEADS, :])
    acs_t = _dot_exact_rhs(dtt * a_t, mask_tb)

    xdt_b = xdt.astype(BF16)
    hpg = SSD_HEADS // SSD_GROUPS
    for g in range(SSD_GROUPS):
        cbm = _dot_nt(cm[:, g * SSD_STATE:(g + 1) * SSD_STATE], bm[:, g * SSD_STATE:(g + 1) * SSD_STATE])
        for hh in range(hpg):
            h = g * hpg + hh
            lo = h * SSD_HEAD_DIM
            seg = acs[:, lo:lo + 1] - acs_t[h:h + 1, :]
            lm = jnp.exp(jnp.where(mask, seg, -1e30))
            wm = (cbm * lm).astype(BF16)
            y_ref[:, lo:lo + SSD_HEAD_DIM] = _dot(wm, xdt_b[:, lo:lo + SSD_HEAD_DIM])

    gw = SSD_WIDTH // SSD_GROUPS
    inter = [[None] * SSD_GROUPS for _ in range(nch)]
    order = range(nch - 1, -1, -1) if reverse else range(nch)
    for c in order:
        rows = slice(c * ch, (c + 1) * ch)
        for g in range(SSD_GROUPS):
            st = st_ref[g]
            cols = slice(g * gw, (g + 1) * gw)
            inter[c][g] = _dot(cm[rows, g * SSD_STATE:(g + 1) * SSD_STATE], st.astype(BF16)) * ea[c][:, cols]
            upd = _dot_tn(bm[rows, g * SSD_STATE:(g + 1) * SSD_STATE], xw[c][:, cols])
            st_ref[g] = st * dec[c][:, cols] + upd
    y_inter = jnp.concatenate([jnp.concatenate(r, axis=1) for r in inter], axis=0)
    o = y_ref[...] + y_inter
    if not reverse:
        o = o + dsk[...] * xs
    return o


def _ssd_body(xf, dtf, dttf, xb, dtb, dttb, xc, dtc, dttc, dtb_e, alog_e, dtb_t, alog_t, dsk,
              of_x, ob_x, of_c, ob_c, stf, stb, y_ref):
    t = pl.program_id(1)
    is_ctx = t == 0

    @pl.when(is_ctx)
    def _():
        stf[...] = jnp.zeros_like(stf)
        stb[...] = jnp.zeros_like(stb)

    prm = (dtb_e, alog_e, dtb_t, alog_t, dsk)
    o_f = _ssd_dir(jnp.where(is_ctx, xc[0], xf[0]), jnp.where(is_ctx, dtc[0], dtf[0]),
                   jnp.where(is_ctx, dttc[0], dttf[0]), *prm, stf, y_ref, False)
    o_b = _ssd_dir(jnp.where(is_ctx, xc[0], xb[0]), jnp.where(is_ctx, dtc[0], dtb[0]),
                   jnp.where(is_ctx, dttc[0], dttb[0]), *prm, stb, y_ref, True)

    @pl.when(is_ctx)
    def _():
        of_c[0] = o_f.astype(of_c.dtype)
        ob_c[0] = o_b.astype(ob_c.dtype)

    @pl.when(jnp.logical_not(is_ctx))
    def _():
        of_x[0] = o_f.astype(of_x.dtype)
        ob_x[0] = o_b.astype(ob_x.dtype)


def _ssd(xbc_x, xbc_c, dt_x, dt_c, dtt_x, dtt_c, prm):
    bsz, seq, _ = xbc_x.shape
    tb = TOKEN_BLOCK
    nxb = seq // tb
    xf, xb = _scan_block_maps(nxb)
    full = lambda a: pl.BlockSpec(a.shape, lambda b, t: (0,) * a.ndim)
    params = [prm["dtb_e"], prm["alog_e"], prm["dtb_t"], prm["alog_t"], prm["dsk"]]

    def xspecs(blk):
        return [pl.BlockSpec((1, tb, SSD_XBC), lambda b, t: (b, blk(t), 0)),
                pl.BlockSpec((1, tb, 128), lambda b, t: (b, blk(t), 0)),
                pl.BlockSpec((1, 16, tb), lambda b, t: (b, 0, blk(t)))]

    zero = lambda t: 0
    yspec = lambda blk: pl.BlockSpec((1, tb, SSD_WIDTH), lambda b, t: (b, blk(t), 0))
    out_x = jax.ShapeDtypeStruct((bsz, seq, SSD_WIDTH), BF16)
    out_c = jax.ShapeDtypeStruct((bsz, CTX_LEN, SSD_WIDTH), BF16)
    state = pltpu.VMEM((SSD_GROUPS, SSD_STATE, SSD_WIDTH // SSD_GROUPS), F32)
    return pl.pallas_call(
        _ssd_body,
        grid=(bsz, nxb + 1),
        in_specs=xspecs(xf) + xspecs(xb) + xspecs(zero) + [full(a) for a in params],
        out_specs=[yspec(xf), yspec(xb), yspec(zero), yspec(zero)],
        out_shape=[out_x, out_x, out_c, out_c],
        scratch_shapes=[state, state, pltpu.VMEM((tb, SSD_WIDTH), F32)],
        compiler_params=_cparams(("parallel", "arbitrary"), big=True),
        name="ssd_scan",
    )(xbc_x, dt_x, dtt_x, xbc_x, dt_x, dtt_x, xbc_c, dt_c, dtt_c, *params)


def _filt_body(w1, b1, f1, w2, b2, f2, w3, delta, buf_ref, sa_ref, *, seq_len, tr):
    i = pl.program_id(0)
    jrow = i * tr + lax.broadcasted_iota(jnp.int32, (tr, 128), 0)
    pos = jnp.where(jrow < seq_len, jrow, 2 * seq_len - jrow).astype(F32)
    tpos = pos / (seq_len - 1.0)
    wpos = pos * (2.0 * math.pi / seq_len)
    lane = lax.broadcasted_iota(jnp.int32, (tr, 128), 1)
    bidx = jnp.where(lane <= HY_EMB_BANDS, lane - 1, lane - 1 - HY_EMB_BANDS).astype(F32)
    band = 1e-4 + bidx * ((HY_EMB_BANDS - 1 - 1e-4) / (HY_EMB_BANDS - 1))
    ang = band * wpos + jnp.where(lane > HY_EMB_BANDS, 0.5 * math.pi, 0.0)
    z = jnp.where(lane == 0, tpos, jnp.where(lane <= 2 * HY_EMB_BANDS, jnp.cos(ang), 0.0))
    h = jnp.sin(f1[...] * (_dot3(z, w1[...]) + b1[...]))
    h = jnp.sin(f2[...] * (_dot3(h, w2[...]) + b2[...]))
    o = _dot3(h, w3[0])
    o = o * jnp.exp(-tpos[:, 0:1] * delta[...])
    o = jnp.where(jrow[:, 0:1] == seq_len, 0.0, o)
    for order in range(2):
        for hf in range(HY_HALVES):
            lo = order * HY_WIDTH + hf * LANES
            buf_ref[order, hf] = o[:, lo:lo + LANES]

    @pl.when(i == 0)
    def _():
        sa_ref[...] = jnp.zeros_like(sa_ref)

    sa_ref[...] = sa_ref[...] + jnp.sum(jnp.abs(o), axis=0, keepdims=True)


def _filt_gen(p, seq_len, tr):
    n = 2 * seq_len
    assert seq_len % tr == 0
    half = seq_len // tr
    full = lambda a: pl.BlockSpec(a.shape, lambda i: (0,) * a.ndim)
    small = [p["w1"], p["b1"], p["f1"], p["w2"], p["b2"], p["f2"]]
    return pl.pallas_call(
        functools.partial(_filt_body, seq_len=seq_len, tr=tr),
        grid=(n // tr,),
        in_specs=[full(a) for a in small] + [
            pl.BlockSpec((1, HY_HIDDEN, 2 * HY_WIDTH), lambda i: (i // half, 0, 0)),
            full(p["delta"]),
        ],
        out_specs=[pl.BlockSpec((2, HY_HALVES, tr, LANES), lambda i: (0, 0, i, 0)),
                   pl.BlockSpec((8, 2 * HY_WIDTH), lambda i: (0, 0))],
        out_shape=[jax.ShapeDtypeStruct((2, HY_HALVES, n, LANES), F32),
                   jax.ShapeDtypeStruct((8, 2 * HY_WIDTH), F32)],
        compiler_params=_cparams(("arbitrary",)),
        name="hyena_filter",
    )(*small, p["w3"], p["delta"])


FFT_NB = 8


def _strided_rows(ref3, j, n):
    flat = ref3.reshape(ref3.shape[0] * FFT_NB, LANES)
    return flat[pl.ds(j, n, stride=FFT_NB), :]


def _store_strided_rows(ref3, j, val):
    flat = ref3.reshape(ref3.shape[0] * FFT_NB, LANES)
    flat[pl.ds(j, val.shape[0], stride=FFT_NB), :] = val


def _s1_body(x_ref, f_ref, t_ref, o_ref):
    rows = x_ref.shape[2]
    for j in range(FFT_NB):
        x = jnp.concatenate([_strided_rows(x_ref.at[0, hf], j, rows) for hf in range(HY_HALVES)], axis=1)
        a = _dot3_pre((f_ref[0], f_ref[1]), x)
        tr = t_ref[0, 0, :, j:j + 1]
        ti = t_ref[0, 1, :, j:j + 1]
        ar, ai = a[:FFT_N1], a[FFT_N1:]
        out = jnp.concatenate([ar * tr - ai * ti, ar * ti + ai * tr], axis=0)
        for hf in range(HY_HALVES):
            _store_strided_rows(o_ref.at[0, hf], j, out[:, hf * LANES:(hf + 1) * LANES])


def _fft_stage1(x, fmat, twid):
    g, _, rows, n2, _ = x.shape
    return pl.pallas_call(
        _s1_body,
        grid=(g, n2 // FFT_NB),
        in_specs=[
            pl.BlockSpec((1, HY_HALVES, rows, FFT_NB, LANES), lambda i, j: (i, 0, 0, j, 0)),
            pl.BlockSpec(fmat.shape, lambda i, j: (0, 0, 0)),
            pl.BlockSpec((1, 2, FFT_N1, FFT_NB), lambda i, j: (j, 0, 0, 0)),
        ],
        out_specs=pl.BlockSpec((1, HY_HALVES, 2 * FFT_N1, FFT_NB, LANES), lambda i, j: (i, 0, 0, j, 0)),
        out_shape=jax.ShapeDtypeStruct((g, HY_HALVES, 2 * FFT_N1, n2, LANES), F32),
        compiler_params=_cparams(("parallel", "arbitrary"), big=True),
        name="fft_stage1",
    )(x, fmat, twid)


def _load_complex(ref, k):
    return jnp.concatenate(
        [jnp.concatenate([ref[hf, 0, k], ref[hf, 1, k]], axis=0) for hf in range(HY_HALVES)], axis=1)


def _store_complex(ref, k, val):
    for hf in range(HY_HALVES):
        ref[hf, 0, k] = val[:FFT_N1, hf * LANES:(hf + 1) * LANES]
        ref[hf, 1, k] = val[FFT_N1:, hf * LANES:(hf + 1) * LANES]


def _s2_body(a_ref, h_ref, f_ref, fc_ref, o_ref, *, kb):
    for k in range(kb):
        x = _dot3_pre((f_ref[0], f_ref[1]), _load_complex(a_ref, k))
        xr, xi = x[:FFT_N1], x[FFT_N1:]
        h = _load_complex(h_ref.at[0], k)
        hr, hi = h[:FFT_N1], h[FFT_N1:]
        y = jnp.concatenate([xr * hr - xi * hi, xr * hi + xi * hr], axis=0)
        _store_complex(o_ref, k, _dot3_pre((fc_ref[0], fc_ref[1]), y))


def _fft_stage2(a, hspec, order, fmat, fmat_c, kb=8):
    blk = pl.BlockSpec((HY_HALVES, 2, kb, FFT_N1, LANES), lambda i: (0, 0, i, 0, 0))
    return pl.pallas_call(
        functools.partial(_s2_body, kb=kb),
        grid=(FFT_N1 // kb,),
        in_specs=[blk,
                  pl.BlockSpec((1, HY_HALVES, 2, kb, FFT_N1, LANES), lambda i: (order, 0, 0, i, 0, 0)),
                  pl.BlockSpec(fmat.shape, lambda i: (0, 0, 0)),
                  pl.BlockSpec(fmat_c.shape, lambda i: (0, 0, 0))],
        out_specs=blk,
        out_shape=jax.ShapeDtypeStruct(a.shape, F32),
        compiler_params=_cparams(("arbitrary",), big=True),
        name="fft_stage2_mul",
    )(a, hspec, fmat, fmat_c)


def _s2f_body(a_ref, sa_ref, f_ref, o_ref, *, kb, n):
    o = pl.program_id(0)
    sa = jnp.where(o == 0, sa_ref[0:1, :HY_WIDTH], sa_ref[0:1, HY_WIDTH:])
    scale = 1.0 / (sa * float(n))
    for k in range(kb):
        x = _dot3_pre((f_ref[0], f_ref[1]), _load_complex(a_ref.at[0], k))
        _store_complex(o_ref.at[0], k, x * scale)


def _fft_stage2_filter(a, sumabs, fmat, n, kb=8):
    blk = pl.BlockSpec((1, HY_HALVES, 2, kb, FFT_N1, LANES), lambda o, i: (o, 0, 0, i, 0, 0))
    return pl.pallas_call(
        functools.partial(_s2f_body, kb=kb, n=n),
        grid=(2, FFT_N1 // kb),
        in_specs=[blk, pl.BlockSpec(sumabs.shape, lambda o, i: (0, 0)),
                  pl.BlockSpec(fmat.shape, lambda o, i: (0, 0, 0))],
        out_specs=blk,
        out_shape=jax.ShapeDtypeStruct(a.shape, F32),
        compiler_params=_cparams(("parallel", "arbitrary"), big=True),
        name="fft_stage2_filter",
    )(a, sumabs, fmat)


def _s3_body(c_ref, t_ref, f_ref, gate_ref, v_ref, bias_ref, o_ref):
    rows = gate_ref.shape[1]
    for j in range(FFT_NB):
        tr = t_ref[0, 0, :, j:j + 1]
        ti = t_ref[0, 1, :, j:j + 1]
        c = jnp.concatenate([_strided_rows(c_ref.at[hf], j, 2 * FFT_N1) for hf in range(HY_HALVES)], axis=1)
        cr, ci = c[:FFT_N1], c[FFT_N1:]
        dmat = jnp.concatenate([cr * tr + ci * ti, ci * tr - cr * ti], axis=0)
        w = _dot3_pre((f_ref[0], f_ref[1]), dmat)
        for hf in range(HY_HALVES):
            lanes = slice(hf * LANES, (hf + 1) * LANES)
            g = _strided_rows(gate_ref.at[hf], j, rows)
            v = _strided_rows(v_ref.at[hf], j, rows)
            _store_strided_rows(o_ref.at[hf], j, g * (w[:, lanes] + v * bias_ref[:, lanes]))


def _fft_stage3(c, twid, fmat, gate, v, bias_row):
    _, rows, n2, _ = gate.shape
    dspec = pl.BlockSpec((HY_HALVES, rows, FFT_NB, LANES), lambda j: (0, 0, j, 0))
    return pl.pallas_call(
        _s3_body,
        grid=(n2 // FFT_NB,),
        in_specs=[
            pl.BlockSpec((HY_HALVES, 2 * FFT_N1, FFT_NB, LANES), lambda j: (0, 0, j, 0)),
            pl.BlockSpec((1, 2, FFT_N1, FFT_NB), lambda j: (j, 0, 0, 0)),
            pl.BlockSpec(fmat.shape, lambda j: (0, 0, 0)),
            dspec, dspec,
            pl.BlockSpec((1, HY_WIDTH), lambda j: (0, 0)),
        ],
        out_specs=dspec,
        out_shape=jax.ShapeDtypeStruct(gate.shape, F32),
        compiler_params=_cparams(("arbitrary",), big=True),
        name="fft_stage3_gate",
    )(c, twid, fmat, gate, v, bias_row)


def _hyc_body(v_ref, x1_ref, x2_ref, buf_ref, sa_ref, cs_ref, bias_ref, o_ref):
    n = 2 * CTX_LEN
    cmat, smat = cs_ref[0], cs_ref[1]
    c_in, s_in = cmat[:, :CTX_LEN], smat[:, :CTX_LEN]
    c_out, s_out = cmat[:CTX_LEN, :], smat[:CTX_LEN, :]
    batch = lambda ref, b: jnp.concatenate([ref[hf, b] for hf in range(HY_HALVES)], axis=1)
    yr, yi = batch(v_ref, 0), batch(v_ref, 1)
    for o, gate in enumerate((x1_ref, x2_ref)):
        bufo = jnp.concatenate([buf_ref[o, hf] for hf in range(HY_HALVES)], axis=1)
        scale = 1.0 / (sa_ref[0:1, o * HY_WIDTH:(o + 1) * HY_WIDTH] * float(n))
        hr = _dot3(cmat, bufo) * scale
        hi = -_dot3(smat, bufo) * scale
        xr = _dot3(c_in, yr) + _dot3(s_in, yi)
        xi = _dot3(c_in, yi) - _dot3(s_in, yr)
        zr = xr * hr - xi * hi
        zi = xr * hi + xi * hr
        wr = _dot3(c_out, zr) - _dot3(s_out, zi)
        wi = _dot3(c_out, zi) + _dot3(s_out, zr)
        b = bias_ref[o:o + 1, :]
        yr = batch(gate, 0) * (wr + yr * b)
        yi = batch(gate, 1) * (wi + yi * b)
    for hf in range(HY_HALVES):
        o_ref[hf, 0] = yr[:, hf * LANES:(hf + 1) * LANES]
        o_ref[hf, 1] = yi[:, hf * LANES:(hf + 1) * LANES]


def _hy_ctx(v, x1, x2, buf, sumabs, cs, bias):
    return pl.pallas_call(
        _hyc_body,
        out_shape=jax.ShapeDtypeStruct(v.shape, F32),
        compiler_params=pltpu.CompilerParams(vmem_limit_bytes=VMEM_LIMIT),
        name="hyena_ctx",
    )(v, x1, x2, buf, sumabs, cs, bias)


def _dft_constants(seq_len):
    n = 2 * seq_len
    n1 = FFT_N1
    assert n == n1 * n1
    idx = np.arange(n1, dtype=np.float64)
    th = 2.0 * np.pi * np.outer(idx, idx) / n1
    fr, fi = np.cos(th), -np.sin(th)
    half = n1 // 2

    def parts(m):
        m32 = jnp.asarray(m, F32)
        hi = m32.astype(BF16)
        mid = (m32 - hi.astype(F32)).astype(BF16)
        return jnp.stack([hi, mid])

    f1_data = np.block([[fr[:, :half], -fi[:, :half]], [fi[:, :half], fr[:, :half]]])
    f1_real = np.concatenate([fr, fi], axis=0)
    f2 = np.block([[fr, -fi], [fi, fr]])
    f2c = np.block([[fr, fi], [-fi, fr]])
    f3 = np.block([[fr[:half], fi[:half]], [-fi[:half], fr[:half]]])
    tw = 2.0 * np.pi * np.outer(idx, idx) / n
    twid = np.stack([np.cos(tw), -np.sin(tw)])
    twid = twid.reshape(2, n1, n1 // FFT_NB, FFT_NB).transpose(2, 0, 1, 3)
    return dict(f1_data=parts(f1_data), f1_real=parts(f1_real), f2=parts(f2), f2c=parts(f2c),
                f3=parts(f3), twid=jnp.asarray(twid, F32))


def _ctx_dft_constants():
    n = 2 * CTX_LEN
    idx = np.arange(n, dtype=np.float64)
    th = 2.0 * np.pi * np.outer(idx, idx) / n
    return jnp.asarray(np.stack([np.cos(th), np.sin(th)]), F32)


def _hyena_filter_params(hy_w1, hy_b1, hy_freq1, hy_w2, hy_b2, hy_freq2, hy_w3):
    w1 = jnp.zeros((128, HY_HIDDEN), F32).at[:hy_w1.shape[0]].set(hy_w1)
    w3 = hy_w3.reshape(HY_HIDDEN, 2, 2 * HY_WIDTH).transpose(1, 0, 2)
    deltas = np.abs(np.linspace(HY_MIN_DECAY, HY_MAX_DECAY, HY_WIDTH))
    delta = jnp.asarray(np.tile(deltas, 2)[None, :], F32)
    row = lambda a: a.reshape(1, -1)
    return dict(w1=w1, b1=row(hy_b1), f1=row(hy_freq1), w2=hy_w2, b2=row(hy_b2), f2=row(hy_freq2),
                w3=w3, delta=delta)


def _hyena_x(v, x1, x2, hspec, bias, consts):
    nh, bsz, seq, lanes = v.shape
    assert bsz == 2
    rows = bsz * (seq // FFT_N1)
    view = lambda a: a.reshape(nh, rows, FFT_N1, lanes)
    y = view(v)
    for o, gate in enumerate((x1, x2)):
        a = _fft_stage1(y.reshape(1, nh, rows, FFT_N1, lanes), consts["f1_data"], consts["twid"])
        c = _fft_stage2(a.reshape(nh, 2, FFT_N1, FFT_N1, lanes), hspec, o, consts["f2"], consts["f2c"])
        y = _fft_stage3(c.reshape(nh, 2 * FFT_N1, FFT_N1, lanes), consts["twid"], consts["f3"], view(gate), y,
                        bias[o].reshape(1, nh * lanes))
    return y.reshape(nh, bsz, seq, lanes)


def _out_body(of, ob, ug, hy, yf, yb, zz, hgn, ssn, w_ref, h_ref, gate_ref, o_ref):
    o = of[0].astype(F32) + ob[0].astype(F32)
    gi = lax.broadcasted_iota(jnp.int32, (HG_WIDTH, HG_WIDTH), 0) // HG_DK
    gj = lax.broadcasted_iota(jnp.int32, (HG_WIDTH, HG_WIDTH), 1) // HG_DK
    avg = jnp.where(gi == gj, 1.0 / HG_DK, 0.0).astype(BF16)
    sq = o * o
    sh = sq.astype(BF16)
    sl = (sq - sh.astype(F32)).astype(BF16)
    ms = _dot(sh, avg) + _dot(sl, avg)
    hg = o * lax.rsqrt(ms + EPS) * hgn[...] * _silu(ug[0])
    y = (yf[0].astype(F32) + yb[0].astype(F32)) * _silu(zz[0])
    gw = SSD_WIDTH // SSD_GROUPS
    parts = []
    for g in range(SSD_GROUPS):
        yg = y[:, g * gw:(g + 1) * gw]
        parts.append(yg * lax.rsqrt(jnp.mean(yg * yg, axis=-1, keepdims=True) + EPS))
    ys = jnp.concatenate(parts, axis=1) * ssn[...]
    acc = _dot(hg.astype(BF16), w_ref[0:HG_WIDTH, :])
    hyv = jnp.concatenate([hy[hf, 0] for hf in range(HY_HALVES)], axis=1)
    acc = acc + _dot(hyv.astype(BF16), w_ref[HG_WIDTH:HG_WIDTH + HY_WIDTH, :])
    acc = acc + _dot(ys.astype(BF16), w_ref[HG_WIDTH + HY_WIDTH:, :])
    o_ref[0] = h_ref[0] + gate_ref[0] * acc


def _out_proj(of, ob, u_hg, hy, yf, yb, zgate, hgn, ssn, w, h, gate, tm):
    bsz, seq, d = h.shape
    tok = lambda n, col=0: pl.BlockSpec((1, tm, n), lambda b, t: (b, t, col))
    return pl.pallas_call(
        _out_body,
        grid=(bsz, seq // tm),
        in_specs=[tok(HG_WIDTH), tok(HG_WIDTH), tok(HG_WIDTH, 4),
                  pl.BlockSpec((HY_HALVES, 1, tm, LANES), lambda b, t: (0, b, t, 0)),
                  tok(SSD_WIDTH), tok(SSD_WIDTH), tok(SSD_WIDTH),
                  pl.BlockSpec((1, HG_WIDTH), lambda b, t: (0, 0)),
                  pl.BlockSpec((1, SSD_WIDTH), lambda b, t: (0, 0)),
                  pl.BlockSpec(w.shape, lambda b, t: (0, 0)),
                  tok(d),
                  pl.BlockSpec((1, 1, d), lambda b, t: (b, 0, 0))],
        out_specs=tok(d),
        out_shape=jax.ShapeDtypeStruct(h.shape, F32),
        compiler_params=_cparams(("parallel", "arbitrary"), big=True),
        name="mixer_out_proj",
    )(of, ob, u_hg, hy, yf, yb, zgate, hgn.reshape(1, -1), ssn.reshape(1, -1), w, h, gate)


def _ffn_body(*refs, tm, grid_w, vertical, final, cchunk):
    if vertical:
        x_ref, hp, hn, ss_ref, g_ref, wg, wu, cw, cb, wd, gate_ref = refs[:11]
        rest = refs[11:]
    else:
        x_ref, ss_ref, g_ref, wg, wu, cw, cb, wd, gate_ref = refs[:9]
        rest = refs[9:]
    if final:
        fg_ref, o_ref, act_ref = rest
    else:
        o_ref, act_ref = rest
    t = pl.program_id(1)
    nt = pl.num_programs(1)
    ss = ss_ref[0]

    def normmod(v):
        ms = jnp.mean(v * v, axis=-1, keepdims=True)
        y = v * lax.rsqrt(ms + EPS) * g_ref[...]
        return (y * (1.0 + ss[1:2, :]) + ss[0:1, :]).astype(BF16)

    x = x_ref[0]
    yb = normmod(x)
    pad = grid_w if vertical else 0
    ext_rows = tm + 2 * pad
    ye = jnp.concatenate([normmod(hp[0]), yb, normmod(hn[0])], axis=0) if vertical else yb
    col = lax.broadcasted_iota(jnp.int32, (ext_rows, cchunk), 0) % grid_w
    for j in range(0, D_FF, cchunk):
        cs = slice(j, j + cchunk)
        ext = _dot(ye, wg[:, cs])
        if vertical:
            ext = jnp.concatenate([jnp.where(t > 0, ext[:pad], 0.0), ext[pad:pad + tm],
                                   jnp.where(t < nt - 1, ext[pad + tm:], 0.0)], axis=0)
        left = jnp.where(col != 0, pltpu.roll(ext, 1, 0), 0.0)
        right = jnp.where(col != grid_w - 1, pltpu.roll(ext, ext_rows - 1, 0), 0.0)
        acc = jnp.zeros((tm, cchunk), F32) + cb[:, cs]
        for dy in ((-1, 0, 1) if vertical else (0,)):
            r0 = pad + dy * grid_w
            ky = dy + 1
            acc = acc + left[r0:r0 + tm] * cw[3 * ky:3 * ky + 1, cs]
            acc = acc + ext[r0:r0 + tm] * cw[3 * ky + 1:3 * ky + 2, cs]
            acc = acc + right[r0:r0 + tm] * cw[3 * ky + 2:3 * ky + 3, cs]
        act_ref[:, cs] = (_silu(acc) * _dot(yb, wu[:, cs])).astype(BF16)
    y = _dot(act_ref[...], wd[...])
    x = x + gate_ref[0] * y
    if final:
        ms = jnp.mean(x * x, axis=-1, keepdims=True)
        x = x * lax.rsqrt(ms + EPS) * fg_ref[...]
    o_ref[0] = x


def _ffn(h, ss, g, wg, wu, cw, cb, wd, gate, tm, grid_w, vertical, final_g=None):
    bsz, seq, d = h.shape
    nhb = tm // grid_w
    tok = lambda n: pl.BlockSpec((1, tm, n), lambda b, t: (b, t, 0))
    full = lambda a: pl.BlockSpec(a.shape, lambda b, t: (0,) * a.ndim)
    resident = lambda a: pl.BlockSpec(a.shape, lambda b, t: (0,) * a.ndim, pipeline_mode=pl.Buffered(1))
    in_specs = [tok(d)]
    args = [h]
    if vertical:
        nrow = seq // grid_w
        in_specs += [
            pl.BlockSpec((1, grid_w, d), lambda b, t: (b, jnp.maximum(t * nhb - 1, 0), 0)),
            pl.BlockSpec((1, grid_w, d), lambda b, t: (b, jnp.minimum((t + 1) * nhb, nrow - 1), 0)),
        ]
        args += [h, h]
    in_specs += [pl.BlockSpec((1, 2, d), lambda b, t: (b, 0, 0)), pl.BlockSpec((1, d), lambda b, t: (0, 0)),
                 resident(wg), resident(wu), full(cw), full(cb), resident(wd),
                 pl.BlockSpec((1, 1, d), lambda b, t: (b, 0, 0))]
    args += [ss, g.reshape(1, d), wg, wu, cw, cb, wd, gate]
    if final_g is not None:
        in_specs.append(pl.BlockSpec((1, d), lambda b, t: (0, 0)))
        args.append(final_g.reshape(1, d))
    return pl.pallas_call(
        functools.partial(_ffn_body, tm=tm, grid_w=grid_w, vertical=vertical,
                          final=final_g is not None, cchunk=256),
        grid=(bsz, seq // tm),
        in_specs=in_specs,
        out_specs=tok(d),
        out_shape=jax.ShapeDtypeStruct(h.shape, F32),
        scratch_shapes=[pltpu.VMEM((tm, D_FF), BF16)],
        compiler_params=_cparams(("parallel", "arbitrary"), big=True),
        name="ffn",
    )(*args)


def kernel(x, c, ctx, c_ctx, w_ada, b_ada, norm1_g, norm2_g, w_in, w_out, hg_lb_logits, hg_norm_g, hy_conv_w, hy_conv_b, hy_w1, hy_b1, hy_freq1, hy_w2, hy_b2, hy_freq2, hy_w3, hy_bias, ssd_conv_w, ssd_conv_b, ssd_dt_bias, ssd_a_log, ssd_d, ssd_norm_g, ffn_w_gate, ffn_w_up, ffn_conv_w, ffn_conv_b, ffn_w_down, final_norm_g):
    bsz, seq, d = x.shape
    cc = jnp.zeros((8, d), F32).at[:bsz].set(c).at[bsz].set(c_ctx)
    mods = _ada(cc, w_ada, b_ada)

    consts = _dft_constants(seq)
    cs_ctx = _ctx_dft_constants()
    lb_logits = hg_lb_logits.astype(F32)

    z0 = HG_COLS + HY_COLS
    for l in range(DEPTH):
        last = l == DEPTH - 1
        m = mods[l].reshape(8, 6, d)
        mx = m[:bsz]
        mc = jnp.broadcast_to(m[bsz:bsz + 1], (bsz, 6, d))

        wl = w_in[l]
        w_cat = jnp.concatenate([
            wl[:, :z0], wl[:, z0 + SSD_WIDTH:z0 + SSD_WIDTH + SSD_XBC], wl[:, z0:z0 + SSD_WIDTH],
            wl[:, z0 + SSD_WIDTH + SSD_XBC:], jnp.zeros((d, 128 - 2 * SSD_HEADS), F32)], axis=1).astype(BF16)
        cw_cat = jnp.concatenate([hy_conv_w[l], ssd_conv_w[l]], axis=1)
        cb_cat = jnp.concatenate([hy_conv_b[l], ssd_conv_b[l]]).reshape(1, -1)
        uhg_x, v_x, x1_x, x2_x, xbc_x, z_x, dt_x = _inproj(x, mx[:, 0:2], norm1_g[l], w_cat, cw_cat, cb_cat,
                                                           512, "in_proj_x")
        uhg_c, v_c, x1_c, x2_c, xbc_c, z_c, dt_c = _inproj(ctx, mc[:, 0:2], norm1_g[l], w_cat, cw_cat, cb_cat,
                                                           CTX_LEN, "in_proj_ctx")

        of_x, ob_x, of_c, ob_c = _gla(uhg_x, uhg_c, lb_logits, l)

        rep = lambda a: jnp.repeat(a, SSD_HEAD_DIM, axis=-1)
        prm = dict(dtb_e=rep(ssd_dt_bias[l]), alog_e=rep(ssd_a_log[l]),
                   dtb_t=ssd_dt_bias[l].reshape(-1, 1), alog_t=ssd_a_log[l].reshape(-1, 1),
                   dsk=rep(ssd_d[l]).reshape(1, -1))
        dtt_x = jnp.swapaxes(dt_x[:, :, :2 * SSD_HEADS], 1, 2)
        dtt_c = jnp.swapaxes(dt_c[:, :, :2 * SSD_HEADS], 1, 2)
        yf_x, yb_x, yf_c, yb_c = _ssd(xbc_x, xbc_c, dt_x, dt_c, dtt_x, dtt_c, prm)

        fp = _hyena_filter_params(hy_w1[l], hy_b1[l], hy_freq1[l], hy_w2[l], hy_b2[l], hy_freq2[l], hy_w3[l])
        buf, sumabs = _filt_gen(fp, seq, 1024)
        fa = _fft_stage1(buf.reshape(2, HY_HALVES, FFT_N1, FFT_N1, LANES), consts["f1_real"], consts["twid"])
        hspec = _fft_stage2_filter(fa.reshape(2, HY_HALVES, 2, FFT_N1, FFT_N1, LANES), sumabs, consts["f2"], 2 * seq)
        ohy_x = _hyena_x(v_x, x1_x, x2_x, hspec, hy_bias[l], consts)

        wo = w_out[l].astype(BF16)
        x = _out_proj(of_x, ob_x, uhg_x, ohy_x, yf_x, yb_x, z_x, hg_norm_g[l], ssd_norm_g[l], wo, x,
                      mx[:, 2:3], 512)

        wg, wu = ffn_w_gate[l].astype(BF16), ffn_w_up[l].astype(BF16)
        wd = ffn_w_down[l].astype(BF16)
        cw_f = ffn_conv_w[l].reshape(9, D_FF)
        cb_f = ffn_conv_b[l].reshape(1, D_FF)
        if not last:
            buf_c, sumabs_c = _filt_gen(fp, CTX_LEN, CTX_LEN)
            ohy_c = _hy_ctx(v_c, x1_c, x2_c, buf_c, sumabs_c, cs_ctx, hy_bias[l])
            ctx = _out_proj(of_c, ob_c, uhg_c, ohy_c, yf_c, yb_c, z_c, hg_norm_g[l], ssd_norm_g[l], wo, ctx,
                            mc[:, 2:3], CTX_LEN)
            ctx = _ffn(ctx, mc[:, 3:5], norm2_g[l], wg, wu, cw_f, cb_f, wd, mc[:, 5:6], CTX_LEN, CTX_LEN, False)

        x = _ffn(x, mx[:, 3:5], norm2_g[l], wg, wu, cw_f, cb_f, wd, mx[:, 5:6], 512, GRID_W, True,
                 final_g=final_norm_g if last else None)
    return x
```

```python
import functools
import math

import numpy as np
import jax
import jax.numpy as jnp
from jax import lax
from jax.experimental import pallas as pl
from jax.experimental.pallas import tpu as pltpu

F32 = jnp.float32
BF16 = jnp.bfloat16

D_MODEL = 1024
DEPTH = 2
CTX_LEN = 256
GRID_W = 64
EPS = 1e-6

HG_HEADS = 4
HG_DK = 64
HG_WIDTH = 256
HG_CHUNK = 32

HY_WIDTH = 256
HY_EMB_BANDS = 16
HY_HIDDEN = 64
HY_MIN_DECAY = math.log(1e-2) / 1.5
HY_MAX_DECAY = math.log(1e-2) / 0.3

SSD_HEADS = 8
SSD_HEAD_DIM = 64
SSD_WIDTH = 512
SSD_GROUPS = 2
SSD_STATE = 128
SSD_XBC = 1024
SSD_CHUNK = 64

D_FF = 2816
HG_COLS = 5 * HG_WIDTH
HY_COLS = 3 * HY_WIDTH

LANES = 128
HY_HALVES = HY_WIDTH // LANES
TOKEN_BLOCK = 256
FFT_N1 = 128
VMEM_LIMIT = 56 * 1024 * 1024


def _cparams(sem, big=False):
    kw = dict(dimension_semantics=sem)
    if big:
        kw["vmem_limit_bytes"] = VMEM_LIMIT
    return pltpu.CompilerParams(**kw)


def _dot(a, b):
    return lax.dot_general(a, b, (((1,), (0,)), ((), ())), preferred_element_type=F32)


def _dot_nt(a, b):
    return lax.dot_general(a, b, (((1,), (1,)), ((), ())), preferred_element_type=F32)


def _dot_tn(a, b):
    return lax.dot_general(a, b, (((0,), (0,)), ((), ())), preferred_element_type=F32)


def _split3(a):
    hi = a.astype(BF16)
    r = a - hi.astype(F32)
    mid = r.astype(BF16)
    lo = (r - mid.astype(F32)).astype(BF16)
    return hi, mid, lo


def _dot_exact_lhs(mask_bf16, a):
    h, m, l = _split3(a)
    return _dot(mask_bf16, h) + _dot(mask_bf16, m) + _dot(mask_bf16, l)


def _dot_exact_rhs(a, mask_bf16):
    h, m, l = _split3(a)
    return _dot(h, mask_bf16) + _dot(m, mask_bf16) + _dot(l, mask_bf16)


def _dot3(a, b):
    ah, am, _ = _split3(a)
    bh, bm, _ = _split3(b)
    return _dot(ah, bh) + _dot(ah, bm) + _dot(am, bh)


def _dot3_pre(fparts, b):
    fh, fm = fparts
    bh, bm, _ = _split3(b)
    return _dot(fh, bh) + _dot(fh, bm) + _dot(fm, bh)


def _sigmoid(x):
    return 1.0 / (1.0 + jnp.exp(-x))


def _silu(x):
    return x * _sigmoid(x)


def _softplus(x):
    return jnp.maximum(x, 0.0) + jnp.log(1.0 + jnp.exp(-jnp.abs(x)))


def _log_sigmoid(x):
    return jnp.minimum(x, 0.0) - jnp.log(1.0 + jnp.exp(-jnp.abs(x)))


def _ada_body(c_ref, w_ref, b_ref, o_ref):
    cc = c_ref[...]
    o_ref[0] = _dot3(_silu(cc), w_ref[0]) + b_ref[0]


def _ada(cc, w_ada, b_ada):
    tn = 1536
    n = w_ada.shape[-1]
    return pl.pallas_call(
        _ada_body,
        grid=(DEPTH, n // tn),
        in_specs=[
            pl.BlockSpec((8, D_MODEL), lambda l, j: (0, 0)),
            pl.BlockSpec((1, D_MODEL, tn), lambda l, j: (l, 0, j)),
            pl.BlockSpec((1, 1, tn), lambda l, j: (l, 0, j)),
        ],
        out_specs=pl.BlockSpec((1, 8, tn), lambda l, j: (l, 0, j)),
        out_shape=jax.ShapeDtypeStruct((DEPTH, 8, n), F32),
        compiler_params=_cparams(("arbitrary", "arbitrary"), big=True),
        name="adaln",
    )(cc, w_ada, b_ada.reshape(DEPTH, 1, n))


CONV_C0 = HG_COLS
CONV_C1 = HG_COLS + HY_COLS + SSD_XBC
CONV_CHUNK = 256


def _inproj_body(x_ref, hp_ref, hn_ref, ss_ref, g_ref, w_ref, cw_ref, cb_ref,
                 uhg_ref, v_ref, x1_ref, x2_ref, xbc_ref, z_ref, dt_ref, *, nt):
    t = pl.program_id(1)
    ss = ss_ref[0]

    def normmod(x):
        ms = jnp.mean(x * x, axis=-1, keepdims=True)
        y = x * lax.rsqrt(ms + EPS) * g_ref[...]
        return (y * (1.0 + ss[1:2, :]) + ss[0:1, :]).astype(BF16)

    yb = normmod(x_ref[0])
    yh = normmod(jnp.concatenate([hp_ref[0], hn_ref[0]], axis=0))
    for j in range(0, HG_COLS, 512):
        w = min(512, HG_COLS - j)
        uhg_ref[0, :, j:j + w] = _dot(yb, w_ref[:, j:j + w])
    z_ref[0] = _dot(yb, w_ref[:, CONV_C1:CONV_C1 + SSD_WIDTH])
    dt_ref[0] = _dot(yb, w_ref[:, CONV_C1 + SSD_WIDTH:])

    rows = yb.shape[0]
    ri = lax.broadcasted_iota(jnp.int32, (rows, CONV_CHUNK), 0)
    hy_refs = (v_ref, x1_ref, x2_ref)
    for k in range((CONV_C1 - CONV_C0) // CONV_CHUNK):
        cols = slice(CONV_C0 + k * CONV_CHUNK, CONV_C0 + (k + 1) * CONV_CHUNK)
        ccols = slice(k * CONV_CHUNK, (k + 1) * CONV_CHUNK)
        u = _dot(yb, w_ref[:, cols])
        uh = _dot(yh, w_ref[:, cols])
        prev_row = jnp.where(t > 0, uh[7:8, :], 0.0)
        next_row = jnp.where(t < nt - 1, uh[8:9, :], 0.0)
        dn = jnp.where(ri == 0, prev_row, pltpu.roll(u, 1, 0))
        up = jnp.where(ri == rows - 1, next_row, pltpu.roll(u, rows - 1, 0))
        cv = dn * cw_ref[0:1, ccols] + u * cw_ref[1:2, ccols] + up * cw_ref[2:3, ccols] + cb_ref[:, ccols]
        if k < HY_COLS // CONV_CHUNK:
            for hf in range(HY_HALVES):
                hy_refs[k][hf, 0] = cv[:, hf * LANES:(hf + 1) * LANES]
        else:
            kk = k - HY_COLS // CONV_CHUNK
            xbc_ref[0, :, kk * CONV_CHUNK:(kk + 1) * CONV_CHUNK] = _silu(cv).astype(xbc_ref.dtype)


def _inproj(x, ss, g, w, cw, cb, tm, name):
    bsz, seq, d = x.shape
    nt = seq // tm
    hb = tm // 8
    tok = lambda n: pl.BlockSpec((1, tm, n), lambda b, t: (b, t, 0))
    hy_spec = pl.BlockSpec((HY_HALVES, 1, tm, LANES), lambda b, t: (0, b, t, 0))
    hy_shape = jax.ShapeDtypeStruct((HY_HALVES, bsz, seq, LANES), F32)
    tshape = lambda n: jax.ShapeDtypeStruct((bsz, seq, n), F32)
    return pl.pallas_call(
        functools.partial(_inproj_body, nt=nt),
        grid=(bsz, nt),
        in_specs=[
            tok(d),
            pl.BlockSpec((1, 8, d), lambda b, t: (b, jnp.maximum(t * hb - 1, 0), 0)),
            pl.BlockSpec((1, 8, d), lambda b, t: (b, jnp.minimum((t + 1) * hb, nt * hb - 1), 0)),
            pl.BlockSpec((1, 2, d), lambda b, t: (b, 0, 0)),
            pl.BlockSpec((1, d), lambda b, t: (0, 0)),
            pl.BlockSpec(w.shape, lambda b, t: (0, 0)),
            pl.BlockSpec(cw.shape, lambda b, t: (0, 0)),
            pl.BlockSpec(cb.shape, lambda b, t: (0, 0)),
        ],
        out_specs=[tok(HG_COLS), hy_spec, hy_spec, hy_spec, tok(SSD_XBC), tok(SSD_WIDTH), tok(128)],
        out_shape=[tshape(HG_COLS), hy_shape, hy_shape, hy_shape,
                   jax.ShapeDtypeStruct((bsz, seq, SSD_XBC), BF16), tshape(SSD_WIDTH), tshape(128)],
        compiler_params=_cparams(("parallel", "arbitrary"), big=True),
        name=name,
    )(x, x, x, ss, g.reshape(1, d), w, cw, cb)


def _gla_dir(q, a, v, p, r1, st_ref, reverse):
    tb, ch = TOKEN_BLOCK, HG_CHUNK
    nch = tb // ch
    qv = r1 + _log_sigmoid(a)
    logf = jnp.maximum(p, qv) + jnp.log(1.0 + jnp.exp(-jnp.abs(p - qv)))
    k = 1.0 - jnp.exp(logf)

    ri = lax.broadcasted_iota(jnp.int32, (tb, tb), 0)
    ci = lax.broadcasted_iota(jnp.int32, (tb, tb), 1)
    same = (ri // ch) == (ci // ch)
    tri = (ci >= ri) if reverse else (ci <= ri)
    mask = jnp.logical_and(same, tri)
    mask_b = jnp.where(mask, 1.0, 0.0).astype(BF16)
    bdiag = (ri // HG_DK) == (ci // HG_DK)

    lh, lm, ll = _split3(logf)
    bcs3 = _dot(mask_b, jnp.concatenate([lh, lm, ll], axis=1))
    bcs = bcs3[:, :tb] + bcs3[:, tb:2 * tb] + bcs3[:, 2 * tb:]
    b3 = bcs.reshape(nch, ch, tb)
    mid = ch // 2 if reverse else ch // 2 - 1
    end = 0 if reverse else ch - 1
    b_mid = b3[:, mid:mid + 1, :]
    b_end = b3[:, end:end + 1, :]
    q3 = q.reshape(nch, ch, tb)
    k3 = k.reshape(nch, ch, tb)
    d1 = b3 - b_mid
    qd = (q3 * jnp.exp(d1)).reshape(tb, tb)
    kd = (k3 * jnp.exp(-d1)).reshape(tb, tb).astype(BF16)
    kup = (k3 * jnp.exp(b_end - b3)).astype(BF16)
    qb = (q3 * jnp.exp(b3)).astype(BF16)
    dec = jnp.exp(b_end)

    lane_head = lax.broadcasted_iota(jnp.int32, (1, tb), 1) // HG_DK
    acc = jnp.zeros((tb, tb), F32)
    for h in range(HG_HEADS):
        sel = lane_head == h
        qh = jnp.where(sel, qd, 0.0).astype(BF16)
        s = _dot_nt(qh, kd)
        s = jnp.where(mask, s, 0.0).astype(BF16)
        vh = jnp.where(sel, v, 0.0).astype(BF16)
        acc = acc + _dot(s, vh)

    v3 = v.astype(BF16).reshape(nch, ch, tb)
    inter = [None] * nch
    order = range(nch - 1, -1, -1) if reverse else range(nch)
    st = st_ref[...]
    for c in order:
        inter[c] = _dot_nt(qb[c], st.astype(BF16))
        upd = _dot_tn(v3[c], kup[c])
        st = st * dec[c] + jnp.where(bdiag, upd, 0.0)
    st_ref[...] = st
    return acc + jnp.concatenate(inter, axis=0)


def _gla_body(qf, ff, vf, qb, fb, vb, qc, ffc, fbc, vc, lb_ref, of_x, ob_x, of_c, ob_c, stf, stb, *, layer):
    t = pl.program_id(1)
    is_ctx = t == 0

    @pl.when(is_ctx)
    def _():
        stf[...] = jnp.zeros_like(stf)
        stb[...] = jnp.zeros_like(stb)

    lg = lb_ref[...]
    e = jnp.exp(lg - jnp.max(lg, axis=0, keepdims=True))
    den = jnp.sum(e, axis=0, keepdims=True)
    num = jnp.zeros_like(den)
    for r in range(1, layer + 1):
        num = num + e[r:r + 1, :]
    lb = num / den
    p = jnp.log(lb)
    r1 = jnp.log(1.0 - lb)

    scale = HG_DK ** -0.5
    o_f, o_b = [], []
    for b in range(qf.shape[0]):
        o_f.append(_gla_dir(jnp.where(is_ctx, qc[b], qf[b]) * scale, jnp.where(is_ctx, ffc[b], ff[b]),
                            jnp.where(is_ctx, vc[b], vf[b]), p, r1, stf.at[b], False))
        o_b.append(_gla_dir(jnp.where(is_ctx, qc[b], qb[b]) * scale, jnp.where(is_ctx, fbc[b], fb[b]),
                            jnp.where(is_ctx, vc[b], vb[b]), p, r1, stb.at[b], True))

    @pl.when(is_ctx)
    def _():
        for b in range(len(o_f)):
            of_c[b] = o_f[b].astype(of_c.dtype)
            ob_c[b] = o_b[b].astype(ob_c.dtype)

    @pl.when(jnp.logical_not(is_ctx))
    def _():
        for b in range(len(o_f)):
            of_x[b] = o_f[b].astype(of_x.dtype)
            ob_x[b] = o_b[b].astype(ob_x.dtype)


def _scan_block_maps(nxb):
    fwd = lambda t: jnp.maximum(t - 1, 0)
    bwd = lambda t: jnp.where(t == 0, nxb - 1, nxb - t)
    return fwd, bwd


def _gla(u_x, u_c, lb_logits, layer):
    bsz, seq, _ = u_x.shape
    tb = TOKEN_BLOCK
    nxb = seq // tb
    xf, xb = _scan_block_maps(nxb)

    def xs(blk, col):
        return pl.BlockSpec((1, tb, HG_WIDTH), lambda b, t: (b, blk(t), col))

    def cs(col):
        return pl.BlockSpec((1, tb, HG_WIDTH), lambda b, t: (b, 0, col))

    out_x = jax.ShapeDtypeStruct((bsz, seq, HG_WIDTH), BF16)
    out_c = jax.ShapeDtypeStruct((bsz, CTX_LEN, HG_WIDTH), BF16)
    state = pltpu.VMEM((1, HG_WIDTH, HG_WIDTH), F32)
    return pl.pallas_call(
        functools.partial(_gla_body, layer=layer),
        grid=(bsz, nxb + 1),
        in_specs=[xs(xf, 0), xs(xf, 1), xs(xf, 3), xs(xb, 0), xs(xb, 2), xs(xb, 3),
                  cs(0), cs(1), cs(2), cs(3),
                  pl.BlockSpec((DEPTH, HG_WIDTH), lambda b, t: (0, 0))],
        out_specs=[xs(xf, 0), xs(xb, 0), cs(0), cs(0)],
        out_shape=[out_x, out_x, out_c, out_c],
        scratch_shapes=[state, state],
        compiler_params=_cparams(("parallel", "arbitrary"), big=True),
        name="hgrn2_scan",
    )(u_x, u_x, u_x, u_x, u_x, u_x, u_c, u_c, u_c, u_c, lb_logits)


def _ssd_dir(xbc, dt_raw, dtt_all, dtb_e, alog_e, dtb_t, alog_t, dsk, st_ref, reverse):
    tb, ch = TOKEN_BLOCK, SSD_CHUNK
    nch = tb // ch
    d = 1 if reverse else 0
    xs = xbc[:, :SSD_WIDTH].astype(F32)
    bm = xbc[:, SSD_WIDTH:SSD_WIDTH + 256].astype(BF16)
    cm = xbc[:, SSD_WIDTH + 256:].astype(BF16)

    ri = lax.broadcasted_iota(jnp.int32, (tb, tb), 0)
    ci = lax.broadcasted_iota(jnp.int32, (tb, tb), 1)
    same = (ri // ch) == (ci // ch)
    mask = jnp.logical_and(same, (ci >= ri) if reverse else (ci <= ri))
    mask_b = jnp.where(mask, 1.0, 0.0).astype(BF16)
    mask_t = jnp.logical_and(same, (ri >= ci) if reverse else (ri <= ci))
    mask_tb = jnp.where(mask_t, 1.0, 0.0).astype(BF16)

    ej = lax.broadcasted_iota(jnp.int32, (128, SSD_WIDTH), 0)
    el = lax.broadcasted_iota(jnp.int32, (128, SSD_WIDTH), 1)
    expand = jnp.where(ej == d * SSD_HEADS + el // SSD_HEAD_DIM, 1.0, 0.0).astype(BF16)
    dh, dm, dl = _split3(dt_raw)
    dte_raw = _dot(jnp.concatenate([dh, dm, dl], axis=1), jnp.concatenate([expand] * 3, axis=0))
    dte = _softplus(dte_raw + dtb_e[d:d + 1, :])
    a_e = -jnp.exp(alog_e[d:d + 1, :])
    ah, am, al = _split3(dte * a_e)
    acs3p = _dot(mask_b, jnp.concatenate([ah, am, al], axis=1))
    acs = acs3p[:, :SSD_WIDTH] + acs3p[:, SSD_WIDTH:2 * SSD_WIDTH] + acs3p[:, 2 * SSD_WIDTH:]
    xdt = xs * dte
    acs3 = acs.reshape(nch, ch, SSD_WIDTH)
    end = 0 if reverse else ch - 1
    a_end = acs3[:, end:end + 1, :]
    xw = (xdt.reshape(nch, ch, SSD_WIDTH) * jnp.exp(a_end - acs3)).astype(BF16).reshape(tb, SSD_WIDTH)
    ea = jnp.exp(acs)
    dec = jnp.exp(a_end)

    dtt_raw = dtt_all[d * SSD_HEADS:(d + 1) * SSD_HEADS, :]
    dtt = _softplus(dtt_raw + dtb_t[d * SSD_HEADS:(d + 1) * SSD_HEADS, :])
    a_t = -jnp.exp(alog_t[d * SSD_HEADS:(d + 1) * SSD_HEADS, :])
    th, tm_, tl = _split3(dtt * a_t)
    acs_t3 = _dot(jnp.concatenate([th, tm_, tl], axis=0), mask_tb)
    acs_t = acs_t3[:SSD_HEADS] + acs_t3[SSD_HEADS:2 * SSD_HEADS] + acs_t3[2 * SSD_HEADS:]

    def chunk_diag(m):
        z = jnp.zeros((ch, m.shape[1]), m.dtype)
        return jnp.concatenate(
            [jnp.concatenate([m[c * ch:(c + 1) * ch] if k == c else z for k in range(nch)], axis=1)
             for c in range(nch)], axis=0)

    xdt_b = xdt.astype(BF16)
    hd = SSD_HEAD_DIM
    hpg = SSD_HEADS // SSD_GROUPS
    gw = SSD_WIDTH // SSD_GROUPS
    rr = lax.broadcasted_iota(jnp.int32, (tb, gw), 0)
    ll = lax.broadcasted_iota(jnp.int32, (tb, gw), 1)
    tri4 = ((ll % hd) >= (rr % ch)) if reverse else ((ll % hd) <= (rr % ch))
    head_diag = (rr // hd) == (ll // hd)
    order = range(nch - 1, -1, -1) if reverse else range(nch)
    outs = []
    for g in range(SSD_GROUPS):
        st_cols = slice(g * SSD_STATE, (g + 1) * SSD_STATE)
        cols = slice(g * gw, (g + 1) * gw)
        cm_d = chunk_diag(cm[:, st_cols])
        bm_cat = jnp.concatenate([bm[c * ch:(c + 1) * ch, st_cols] for c in range(nch)], axis=1)
        cb4 = _dot_nt(cm_d, jnp.concatenate([bm_cat] * hpg, axis=0))
        rowm = jnp.concatenate(
            [jnp.broadcast_to(jnp.concatenate([acs_t[g * hpg + h:g * hpg + h + 1, c * ch:(c + 1) * ch]
                                               for h in range(hpg)], axis=1), (ch, gw))
             for c in range(nch)], axis=0)
        w4 = (cb4 * jnp.exp(jnp.where(tri4, acs[:, cols] - rowm, -1e30))).astype(BF16)
        xg = xdt_b[:, cols]
        zero = jnp.zeros((hpg * ch, gw), BF16)
        x4 = jnp.concatenate(
            [jnp.where(head_diag, jnp.concatenate([xg[c * ch:(c + 1) * ch]] * hpg, axis=0), zero)
             for c in range(nch)], axis=0)
        y_intra = _dot(chunk_diag(w4), x4)

        upd = _dot_tn(bm[:, st_cols], chunk_diag(xw[:, cols]))
        st = st_ref[g]
        entering = [None] * nch
        for c in order:
            entering[c] = st.astype(BF16)
            st = st * dec[c][:, cols] + upd[:, c * gw:(c + 1) * gw]
        st_ref[g] = st
        y_inter = _dot(cm_d, jnp.concatenate(entering, axis=0)) * ea[:, cols]
        outs.append(y_intra + y_inter)
    o = jnp.concatenate(outs, axis=1)
    if not reverse:
        o = o + dsk[...] * xs
    return o


def _ssd_body(xf, dtf, dttf, xb, dtb, dttb, xc, dtc, dttc, dtb_e, alog_e, dtb_t, alog_t, dsk,
              of_x, ob_x, of_c, ob_c, stf, stb):
    t = pl.program_id(1)
    is_ctx = t == 0

    @pl.when(is_ctx)
    def _():
        stf[...] = jnp.zeros_like(stf)
        stb[...] = jnp.zeros_like(stb)

    prm = (dtb_e, alog_e, dtb_t, alog_t, dsk)
    o_f = _ssd_dir(jnp.where(is_ctx, xc[0], xf[0]), jnp.where(is_ctx, dtc[0], dtf[0]),
                   jnp.where(is_ctx, dttc[0], dttf[0]), *prm, stf, False)
    o_b = _ssd_dir(jnp.where(is_ctx, xc[0], xb[0]), jnp.where(is_ctx, dtc[0], dtb[0]),
                   jnp.where(is_ctx, dttc[0], dttb[0]), *prm, stb, True)

    @pl.when(is_ctx)
    def _():
        of_c[0] = o_f.astype(of_c.dtype)
        ob_c[0] = o_b.astype(ob_c.dtype)

    @pl.when(jnp.logical_not(is_ctx))
    def _():
        of_x[0] = o_f.astype(of_x.dtype)
        ob_x[0] = o_b.astype(ob_x.dtype)


def _ssd(xbc_x, xbc_c, dt_x, dt_c, dtt_x, dtt_c, prm):
    bsz, seq, _ = xbc_x.shape
    tb = TOKEN_BLOCK
    nxb = seq // tb
    xf, xb = _scan_block_maps(nxb)
    full = lambda a: pl.BlockSpec(a.shape, lambda b, t: (0,) * a.ndim)
    params = [prm["dtb_e"], prm["alog_e"], prm["dtb_t"], prm["alog_t"], prm["dsk"]]

    def xspecs(blk):
        return [pl.BlockSpec((1, tb, SSD_XBC), lambda b, t: (b, blk(t), 0)),
                pl.BlockSpec((1, tb, 128), lambda b, t: (b, blk(t), 0)),
                pl.BlockSpec((1, 16, tb), lambda b, t: (b, 0, blk(t)))]

    zero = lambda t: 0
    yspec = lambda blk: pl.BlockSpec((1, tb, SSD_WIDTH), lambda b, t: (b, blk(t), 0))
    out_x = jax.ShapeDtypeStruct((bsz, seq, SSD_WIDTH), BF16)
    out_c = jax.ShapeDtypeStruct((bsz, CTX_LEN, SSD_WIDTH), BF16)
    state = pltpu.VMEM((SSD_GROUPS, SSD_STATE, SSD_WIDTH // SSD_GROUPS), F32)
    return pl.pallas_call(
        _ssd_body,
        grid=(bsz, nxb + 1),
        in_specs=xspecs(xf) + xspecs(xb) + xspecs(zero) + [full(a) for a in params],
        out_specs=[yspec(xf), yspec(xb), yspec(zero), yspec(zero)],
        out_shape=[out_x, out_x, out_c, out_c],
        scratch_shapes=[state, state],
        compiler_params=_cparams(("parallel", "arbitrary"), big=True),
        name="ssd_scan",
    )(xbc_x, dt_x, dtt_x, xbc_x, dt_x, dtt_x, xbc_c, dt_c, dtt_c, *params)


def _filt_body(w1, b1, f1, w2, b2, f2, w3, delta, buf_ref, sa_ref, *, seq_len, tr):
    i = pl.program_id(0)
    hr = tr // 2
    lane = lax.broadcasted_iota(jnp.int32, (hr, 128), 1)
    grp, sub = lane // HY_HIDDEN, lane % HY_HIDDEN
    jrow = i * tr + grp * hr + lax.broadcasted_iota(jnp.int32, (hr, 128), 0)
    pos = jnp.where(jrow < seq_len, jrow, 2 * seq_len - jrow).astype(F32)
    tpos = pos / (seq_len - 1.0)
    wpos = pos * (2.0 * math.pi / seq_len)
    bidx = jnp.where(sub <= HY_EMB_BANDS, sub - 1, sub - 1 - HY_EMB_BANDS).astype(F32)
    band = 1e-4 + bidx * ((HY_EMB_BANDS - 1 - 1e-4) / (HY_EMB_BANDS - 1))
    ang = band * wpos + jnp.where(sub > HY_EMB_BANDS, 0.5 * math.pi, 0.0)
    z = jnp.where(sub == 0, tpos, jnp.where(sub <= 2 * HY_EMB_BANDS, jnp.cos(ang), 0.0))
    h = jnp.sin(f1[...] * (_dot3(z, w1[...]) + b1[...]))
    h = jnp.sin(f2[...] * (_dot3(h, w2[...]) + b2[...]))

    @pl.when(i == 0)
    def _():
        sa_ref[...] = jnp.zeros_like(sa_ref)

    for g in range(2):
        o = _dot3(h, w3[0, g])
        o = o * jnp.exp(-tpos[:, g * HY_HIDDEN:g * HY_HIDDEN + 1] * delta[...])
        o = jnp.where(jrow[:, g * HY_HIDDEN:g * HY_HIDDEN + 1] == seq_len, 0.0, o)
        for order in range(2):
            for hf in range(HY_HALVES):
                lo = order * HY_WIDTH + hf * LANES
                buf_ref[order, hf, g * hr:(g + 1) * hr, :] = o[:, lo:lo + LANES]
        sa_ref[...] = sa_ref[...] + jnp.sum(jnp.abs(o), axis=0, keepdims=True)


def _filt_gen(p, seq_len, tr):
    n = 2 * seq_len
    assert seq_len % tr == 0
    half = seq_len // tr
    full = lambda a: pl.BlockSpec(a.shape, lambda i: (0,) * a.ndim)
    small = [p["w1"], p["b1"], p["f1"], p["w2"], p["b2"], p["f2"]]
    return pl.pallas_call(
        functools.partial(_filt_body, seq_len=seq_len, tr=tr),
        grid=(n // tr,),
        in_specs=[full(a) for a in small] + [
            pl.BlockSpec((1, 2, 2 * HY_HIDDEN, 2 * HY_WIDTH), lambda i: (i // half, 0, 0, 0)),
            full(p["delta"]),
        ],
        out_specs=[pl.BlockSpec((2, HY_HALVES, tr, LANES), lambda i: (0, 0, i, 0)),
                   pl.BlockSpec((8, 2 * HY_WIDTH), lambda i: (0, 0))],
        out_shape=[jax.ShapeDtypeStruct((2, HY_HALVES, n, LANES), F32),
                   jax.ShapeDtypeStruct((8, 2 * HY_WIDTH), F32)],
        compiler_params=_cparams(("arbitrary",)),
        name="hyena_filter",
    )(*small, p["w3"], p["delta"])


FFT_NB = 8


def _strided_rows(ref3, j, n):
    flat = ref3.reshape(ref3.shape[0] * FFT_NB, LANES)
    return flat[pl.ds(j, n, stride=FFT_NB), :]


def _store_strided_rows(ref3, j, val):
    flat = ref3.reshape(ref3.shape[0] * FFT_NB, LANES)
    flat[pl.ds(j, val.shape[0], stride=FFT_NB), :] = val


def _s1_body(x_ref, f_ref, t_ref, o_ref):
    rows = x_ref.shape[2]
    for j in range(FFT_NB):
        x = jnp.concatenate([_strided_rows(x_ref.at[0, hf], j, rows) for hf in range(HY_HALVES)], axis=1)
        a = _dot3_pre((f_ref[0], f_ref[1]), x)
        tr = t_ref[0, 0, :, j:j + 1]
        ti = t_ref[0, 1, :, j:j + 1]
        ar, ai = a[:FFT_N1], a[FFT_N1:]
        out = jnp.concatenate([ar * tr - ai * ti, ar * ti + ai * tr], axis=0)
        for hf in range(HY_HALVES):
            _store_strided_rows(o_ref.at[0, hf], j, out[:, hf * LANES:(hf + 1) * LANES])


def _fft_stage1(x, fmat, twid):
    g, _, rows, n2, _ = x.shape
    return pl.pallas_call(
        _s1_body,
        grid=(g, n2 // FFT_NB),
        in_specs=[
            pl.BlockSpec((1, HY_HALVES, rows, FFT_NB, LANES), lambda i, j: (i, 0, 0, j, 0)),
            pl.BlockSpec(fmat.shape, lambda i, j: (0, 0, 0)),
            pl.BlockSpec((1, 2, FFT_N1, FFT_NB), lambda i, j: (j, 0, 0, 0)),
        ],
        out_specs=pl.BlockSpec((1, HY_HALVES, 2 * FFT_N1, FFT_NB, LANES), lambda i, j: (i, 0, 0, j, 0)),
        out_shape=jax.ShapeDtypeStruct((g, HY_HALVES, 2 * FFT_N1, n2, LANES), F32),
        compiler_params=_cparams(("parallel", "arbitrary"), big=True),
        name="fft_stage1",
    )(x, fmat, twid)


def _load_complex(ref, k):
    return jnp.concatenate(
        [jnp.concatenate([ref[hf, 0, k], ref[hf, 1, k]], axis=0) for hf in range(HY_HALVES)], axis=1)


def _store_complex(ref, k, val):
    for hf in range(HY_HALVES):
        ref[hf, 0, k] = val[:FFT_N1, hf * LANES:(hf + 1) * LANES].astype(ref.dtype)
        ref[hf, 1, k] = val[FFT_N1:, hf * LANES:(hf + 1) * LANES].astype(ref.dtype)


def _s2_body(a_ref, h_ref, f_ref, fc_ref, o_ref, *, kb):
    for k in range(kb):
        x = _dot3_pre((f_ref[0], f_ref[1]), _load_complex(a_ref, k))
        xr, xi = x[:FFT_N1], x[FFT_N1:]
        h = _load_complex(h_ref.at[0], k).astype(F32)
        hr, hi = h[:FFT_N1], h[FFT_N1:]
        y = jnp.concatenate([xr * hr - xi * hi, xr * hi + xi * hr], axis=0)
        _store_complex(o_ref, k, _dot3_pre((fc_ref[0], fc_ref[1]), y))


def _fft_stage2(a, hspec, order, fmat, fmat_c, kb=8):
    blk = pl.BlockSpec((HY_HALVES, 2, kb, FFT_N1, LANES), lambda i: (0, 0, i, 0, 0))
    return pl.pallas_call(
        functools.partial(_s2_body, kb=kb),
        grid=(FFT_N1 // kb,),
        in_specs=[blk,
                  pl.BlockSpec((1, HY_HALVES, 2, kb, FFT_N1, LANES), lambda i: (order, 0, 0, i, 0, 0)),
                  pl.BlockSpec(fmat.shape, lambda i: (0, 0, 0)),
                  pl.BlockSpec(fmat_c.shape, lambda i: (0, 0, 0))],
        out_specs=blk,
        out_shape=jax.ShapeDtypeStruct(a.shape, F32),
        compiler_params=_cparams(("arbitrary",), big=True),
        name="fft_stage2_mul",
    )(a, hspec, fmat, fmat_c)


def _s2f_body(a_ref, sa_ref, f_ref, o_ref, *, kb, n):
    o = pl.program_id(0)
    sa = jnp.where(o == 0, sa_ref[0:1, :HY_WIDTH], sa_ref[0:1, HY_WIDTH:])
    scale = 1.0 / (sa * float(n))
    for k in range(kb):
        x = _dot3_pre((f_ref[0], f_ref[1]), _load_complex(a_ref.at[0], k))
        _store_complex(o_ref.at[0], k, x * scale)


def _fft_stage2_filter(a, sumabs, fmat, n, kb=8):
    blk = pl.BlockSpec((1, HY_HALVES, 2, kb, FFT_N1, LANES), lambda o, i: (o, 0, 0, i, 0, 0))
    return pl.pallas_call(
        functools.partial(_s2f_body, kb=kb, n=n),
        grid=(2, FFT_N1 // kb),
        in_specs=[blk, pl.BlockSpec(sumabs.shape, lambda o, i: (0, 0)),
                  pl.BlockSpec(fmat.shape, lambda o, i: (0, 0, 0))],
        out_specs=blk,
        out_shape=jax.ShapeDtypeStruct(a.shape, BF16),
        compiler_params=_cparams(("parallel", "arbitrary"), big=True),
        name="fft_stage2_filter",
    )(a, sumabs, fmat)


def _s3_body(c_ref, t_ref, f_ref, gate_ref, v_ref, bias_ref, o_ref):
    rows = gate_ref.shape[1]
    for j in range(FFT_NB):
        tr = t_ref[0, 0, :, j:j + 1]
        ti = t_ref[0, 1, :, j:j + 1]
        c = jnp.concatenate([_strided_rows(c_ref.at[hf], j, 2 * FFT_N1) for hf in range(HY_HALVES)], axis=1)
        cr, ci = c[:FFT_N1], c[FFT_N1:]
        dmat = jnp.concatenate([cr * tr + ci * ti, ci * tr - cr * ti], axis=0)
        w = _dot3_pre((f_ref[0], f_ref[1]), dmat)
        for hf in range(HY_HALVES):
            lanes = slice(hf * LANES, (hf + 1) * LANES)
            g = _strided_rows(gate_ref.at[hf], j, rows)
            v = _strided_rows(v_ref.at[hf], j, rows)
            _store_strided_rows(o_ref.at[hf], j, g * (w[:, lanes] + v * bias_ref[:, lanes]))


def _fft_stage3(c, twid, fmat, gate, v, bias_row):
    _, rows, n2, _ = gate.shape
    dspec = pl.BlockSpec((HY_HALVES, rows, FFT_NB, LANES), lambda j: (0, 0, j, 0))
    return pl.pallas_call(
        _s3_body,
        grid=(n2 // FFT_NB,),
        in_specs=[
            pl.BlockSpec((HY_HALVES, 2 * FFT_N1, FFT_NB, LANES), lambda j: (0, 0, j, 0)),
            pl.BlockSpec((1, 2, FFT_N1, FFT_NB), lambda j: (j, 0, 0, 0)),
            pl.BlockSpec(fmat.shape, lambda j: (0, 0, 0)),
            dspec, dspec,
            pl.BlockSpec((1, HY_WIDTH), lambda j: (0, 0)),
        ],
        out_specs=dspec,
        out_shape=jax.ShapeDtypeStruct(gate.shape, F32),
        compiler_params=_cparams(("arbitrary",), big=True),
        name="fft_stage3_gate",
    )(c, twid, fmat, gate, v, bias_row)


def _hyc_body(v_ref, x1_ref, x2_ref, buf_ref, sa_ref, cs_ref, bias_ref, o_ref):
    n = 2 * CTX_LEN
    cmat, smat = cs_ref[0], cs_ref[1]
    c_in, s_in = cmat[:, :CTX_LEN], smat[:, :CTX_LEN]
    c_out, s_out = cmat[:CTX_LEN, :], smat[:CTX_LEN, :]
    batch = lambda ref, b: jnp.concatenate([ref[hf, b] for hf in range(HY_HALVES)], axis=1)
    yr, yi = batch(v_ref, 0), batch(v_ref, 1)
    for o, gate in enumerate((x1_ref, x2_ref)):
        bufo = jnp.concatenate([buf_ref[o, hf] for hf in range(HY_HALVES)], axis=1)
        scale = 1.0 / (sa_ref[0:1, o * HY_WIDTH:(o + 1) * HY_WIDTH] * float(n))
        hr = _dot3(cmat, bufo) * scale
        hi = -_dot3(smat, bufo) * scale
        xr = _dot3(c_in, yr) + _dot3(s_in, yi)
        xi = _dot3(c_in, yi) - _dot3(s_in, yr)
        zr = xr * hr - xi * hi
        zi = xr * hi + xi * hr
        wr = _dot3(c_out, zr) - _dot3(s_out, zi)
        wi = _dot3(c_out, zi) + _dot3(s_out, zr)
        b = bias_ref[o:o + 1, :]
        yr = batch(gate, 0) * (wr + yr * b)
        yi = batch(gate, 1) * (wi + yi * b)
    for hf in range(HY_HALVES):
        o_ref[hf, 0] = yr[:, hf * LANES:(hf + 1) * LANES]
        o_ref[hf, 1] = yi[:, hf * LANES:(hf + 1) * LANES]


def _hy_ctx(v, x1, x2, buf, sumabs, cs, bias):
    return pl.pallas_call(
        _hyc_body,
        out_shape=jax.ShapeDtypeStruct(v.shape, F32),
        compiler_params=pltpu.CompilerParams(vmem_limit_bytes=VMEM_LIMIT),
        name="hyena_ctx",
    )(v, x1, x2, buf, sumabs, cs, bias)


def _dft_constants(seq_len):
    n = 2 * seq_len
    n1 = FFT_N1
    assert n == n1 * n1
    idx = np.arange(n1, dtype=np.float64)
    th = 2.0 * np.pi * np.outer(idx, idx) / n1
    fr, fi = np.cos(th), -np.sin(th)
    half = n1 // 2

    def parts(m):
        m32 = jnp.asarray(m, F32)
        hi = m32.astype(BF16)
        mid = (m32 - hi.astype(F32)).astype(BF16)
        return jnp.stack([hi, mid])

    f1_data = np.block([[fr[:, :half], -fi[:, :half]], [fi[:, :half], fr[:, :half]]])
    f1_real = np.concatenate([fr, fi], axis=0)
    f2 = np.block([[fr, -fi], [fi, fr]])
    f2c = np.block([[fr, fi], [-fi, fr]])
    f3 = np.block([[fr[:half], fi[:half]], [-fi[:half], fr[:half]]])
    tw = 2.0 * np.pi * np.outer(idx, idx) / n
    twid = np.stack([np.cos(tw), -np.sin(tw)])
    twid = twid.reshape(2, n1, n1 // FFT_NB, FFT_NB).transpose(2, 0, 1, 3)
    return dict(f1_data=parts(f1_data), f1_real=parts(f1_real), f2=parts(f2), f2c=parts(f2c),
                f3=parts(f3), twid=jnp.asarray(twid, F32))


def _ctx_dft_constants():
    n = 2 * CTX_LEN
    idx = np.arange(n, dtype=np.float64)
    th = 2.0 * np.pi * np.outer(idx, idx) / n
    return jnp.asarray(np.stack([np.cos(th), np.sin(th)]), F32)


def _hyena_filter_params(hy_w1, hy_b1, hy_freq1, hy_w2, hy_b2, hy_freq2, hy_w3):
    hid = HY_HIDDEN
    zeros = jnp.zeros((hid, hid), F32)
    w1 = jnp.zeros((hid, hid), F32).at[:hy_w1.shape[0]].set(hy_w1)
    w1 = jnp.block([[w1, zeros], [zeros, w1]])
    w2 = jnp.block([[hy_w2, zeros], [zeros, hy_w2]])
    w3 = hy_w3.reshape(hid, 2, 2 * HY_WIDTH).transpose(1, 0, 2)
    z3 = jnp.zeros_like(w3)
    w3 = jnp.stack([jnp.concatenate([w3, z3], axis=1), jnp.concatenate([z3, w3], axis=1)], axis=1)
    deltas = np.abs(np.linspace(HY_MIN_DECAY, HY_MAX_DECAY, HY_WIDTH))
    delta = jnp.asarray(np.tile(deltas, 2)[None, :], F32)
    row = lambda a: jnp.tile(a, 2).reshape(1, -1)
    return dict(w1=w1, b1=row(hy_b1), f1=row(hy_freq1), w2=w2, b2=row(hy_b2), f2=row(hy_freq2),
                w3=w3, delta=delta)


def _hyena_x(v, x1, x2, hspec, bias, consts):
    nh, bsz, seq, lanes = v.shape
    assert bsz == 2
    rows = bsz * (seq // FFT_N1)
    view = lambda a: a.reshape(nh, rows, FFT_N1, lanes)
    y = view(v)
    for o, gate in enumerate((x1, x2)):
        a = _fft_stage1(y.reshape(1, nh, rows, FFT_N1, lanes), consts["f1_data"], consts["twid"])
        c = _fft_stage2(a.reshape(nh, 2, FFT_N1, FFT_N1, lanes), hspec, o, consts["f2"], consts["f2c"])
        y = _fft_stage3(c.reshape(nh, 2 * FFT_N1, FFT_N1, lanes), consts["twid"], consts["f3"], view(gate), y,
                        bias[o].reshape(1, nh * lanes))
    return y.reshape(nh, bsz, seq, lanes)


def _out_body(of, ob, ug, hy, yf, yb, zz, hgn, ssn, w_ref, h_ref, gate_ref, o_ref):
    o = of[0].astype(F32) + ob[0].astype(F32)
    gi = lax.broadcasted_iota(jnp.int32, (HG_WIDTH, HG_WIDTH), 0) // HG_DK
    gj = lax.broadcasted_iota(jnp.int32, (HG_WIDTH, HG_WIDTH), 1) // HG_DK
    avg = jnp.where(gi == gj, 1.0 / HG_DK, 0.0).astype(BF16)
    sq = o * o
    sh = sq.astype(BF16)
    sl = (sq - sh.astype(F32)).astype(BF16)
    ms = _dot(sh, avg) + _dot(sl, avg)
    hg = o * lax.rsqrt(ms + EPS) * hgn[...] * _silu(ug[0])
    y = (yf[0].astype(F32) + yb[0].astype(F32)) * _silu(zz[0])
    gw = SSD_WIDTH // SSD_GROUPS
    parts = []
    for g in range(SSD_GROUPS):
        yg = y[:, g * gw:(g + 1) * gw]
        parts.append(yg * lax.rsqrt(jnp.mean(yg * yg, axis=-1, keepdims=True) + EPS))
    ys = jnp.concatenate(parts, axis=1) * ssn[...]
    acc = _dot(hg.astype(BF16), w_ref[0:HG_WIDTH, :])
    hyv = jnp.concatenate([hy[hf, 0] for hf in range(HY_HALVES)], axis=1)
    acc = acc + _dot(hyv.astype(BF16), w_ref[HG_WIDTH:HG_WIDTH + HY_WIDTH, :])
    acc = acc + _dot(ys.astype(BF16), w_ref[HG_WIDTH + HY_WIDTH:, :])
    o_ref[0] = h_ref[0] + gate_ref[0] * acc


def _out_proj(of, ob, u_hg, hy, yf, yb, zgate, hgn, ssn, w, h, gate, tm):
    bsz, seq, d = h.shape
    tok = lambda n, col=0: pl.BlockSpec((1, tm, n), lambda b, t: (b, t, col))
    return pl.pallas_call(
        _out_body,
        grid=(bsz, seq // tm),
        in_specs=[tok(HG_WIDTH), tok(HG_WIDTH), tok(HG_WIDTH, 4),
                  pl.BlockSpec((HY_HALVES, 1, tm, LANES), lambda b, t: (0, b, t, 0)),
                  tok(SSD_WIDTH), tok(SSD_WIDTH), tok(SSD_WIDTH),
                  pl.BlockSpec((1, HG_WIDTH), lambda b, t: (0, 0)),
                  pl.BlockSpec((1, SSD_WIDTH), lambda b, t: (0, 0)),
                  pl.BlockSpec(w.shape, lambda b, t: (0, 0)),
                  tok(d),
                  pl.BlockSpec((1, 1, d), lambda b, t: (b, 0, 0))],
        out_specs=tok(d),
        out_shape=jax.ShapeDtypeStruct(h.shape, F32),
        compiler_params=_cparams(("parallel", "arbitrary"), big=True),
        name="mixer_out_proj",
    )(of, ob, u_hg, hy, yf, yb, zgate, hgn.reshape(1, -1), ssn.reshape(1, -1), w, h, gate)


def _ffn_body(*refs, tm, grid_w, vertical, final, cchunk):
    if vertical:
        x_ref, hp, hn, ss_ref, g_ref, wg, wu, cw, cb, wd, gate_ref = refs[:11]
        rest = refs[11:]
    else:
        x_ref, ss_ref, g_ref, wg, wu, cw, cb, wd, gate_ref = refs[:9]
        rest = refs[9:]
    if final:
        fg_ref, o_ref, act_ref = rest
    else:
        o_ref, act_ref = rest
    t = pl.program_id(1)
    nt = pl.num_programs(1)
    ss = ss_ref[0]

    def normmod(v):
        ms = jnp.mean(v * v, axis=-1, keepdims=True)
        y = v * lax.rsqrt(ms + EPS) * g_ref[...]
        return (y * (1.0 + ss[1:2, :]) + ss[0:1, :]).astype(BF16)

    x = x_ref[0]
    yb = normmod(x)
    pad = grid_w if vertical else 0
    ext_rows = tm + 2 * pad
    ye = jnp.concatenate([normmod(hp[0]), yb, normmod(hn[0])], axis=0) if vertical else yb
    col = lax.broadcasted_iota(jnp.int32, (ext_rows, cchunk), 0) % grid_w
    for j in range(0, D_FF, cchunk):
        cs = slice(j, j + cchunk)
        ext = _dot(ye, wg[:, cs])
        if vertical:
            ext = jnp.concatenate([jnp.where(t > 0, ext[:pad], 0.0), ext[pad:pad + tm],
                                   jnp.where(t < nt - 1, ext[pad + tm:], 0.0)], axis=0)
        left = jnp.where(col != 0, pltpu.roll(ext, 1, 0), 0.0)
        right = jnp.where(col != grid_w - 1, pltpu.roll(ext, ext_rows - 1, 0), 0.0)
        acc = jnp.zeros((tm, cchunk), F32) + cb[:, cs]
        for dy in ((-1, 0, 1) if vertical else (0,)):
            r0 = pad + dy * grid_w
            ky = dy + 1
            acc = acc + left[r0:r0 + tm] * cw[3 * ky:3 * ky + 1, cs]
            acc = acc + ext[r0:r0 + tm] * cw[3 * ky + 1:3 * ky + 2, cs]
            acc = acc + right[r0:r0 + tm] * cw[3 * ky + 2:3 * ky + 3, cs]
        act_ref[:, cs] = (_silu(acc) * _dot(yb, wu[:, cs])).astype(BF16)
    y = _dot(act_ref[...], wd[...])
    x = x + gate_ref[0] * y
    if final:
        ms = jnp.mean(x * x, axis=-1, keepdims=True)
        x = x * lax.rsqrt(ms + EPS) * fg_ref[...]
    o_ref[0] = x


def _ffn(h, ss, g, wg, wu, cw, cb, wd, gate, tm, grid_w, vertical, final_g=None):
    bsz, seq, d = h.shape
    nhb = tm // grid_w
    tok = lambda n: pl.BlockSpec((1, tm, n), lambda b, t: (b, t, 0))
    full = lambda a: pl.BlockSpec(a.shape, lambda b, t: (0,) * a.ndim)
    resident = lambda a: pl.BlockSpec(a.shape, lambda b, t: (0,) * a.ndim, pipeline_mode=pl.Buffered(1))
    in_specs = [tok(d)]
    args = [h]
    if vertical:
        nrow = seq // grid_w
        in_specs += [
            pl.BlockSpec((1, grid_w, d), lambda b, t: (b, jnp.maximum(t * nhb - 1, 0), 0)),
            pl.BlockSpec((1, grid_w, d), lambda b, t: (b, jnp.minimum((t + 1) * nhb, nrow - 1), 0)),
        ]
        args += [h, h]
    in_specs += [pl.BlockSpec((1, 2, d), lambda b, t: (b, 0, 0)), pl.BlockSpec((1, d), lambda b, t: (0, 0)),
                 resident(wg), resident(wu), full(cw), full(cb), resident(wd),
                 pl.BlockSpec((1, 1, d), lambda b, t: (b, 0, 0))]
    args += [ss, g.reshape(1, d), wg, wu, cw, cb, wd, gate]
    if final_g is not None:
        in_specs.append(pl.BlockSpec((1, d), lambda b, t: (0, 0)))
        args.append(final_g.reshape(1, d))
    return pl.pallas_call(
        functools.partial(_ffn_body, tm=tm, grid_w=grid_w, vertical=vertical,
                          final=final_g is not None, cchunk=256),
        grid=(bsz, seq // tm),
        in_specs=in_specs,
        out_specs=tok(d),
        out_shape=jax.ShapeDtypeStruct(h.shape, F32),
        scratch_shapes=[pltpu.VMEM((tm, D_FF), BF16)],
        compiler_params=_cparams(("parallel", "arbitrary"), big=True),
        name="ffn",
    )(*args)


def kernel(x, c, ctx, c_ctx, w_ada, b_ada, norm1_g, norm2_g, w_in, w_out, hg_lb_logits, hg_norm_g, hy_conv_w, hy_conv_b, hy_w1, hy_b1, hy_freq1, hy_w2, hy_b2, hy_freq2, hy_w3, hy_bias, ssd_conv_w, ssd_conv_b, ssd_dt_bias, ssd_a_log, ssd_d, ssd_norm_g, ffn_w_gate, ffn_w_up, ffn_conv_w, ffn_conv_b, ffn_w_down, final_norm_g):
    bsz, seq, d = x.shape
    cc = jnp.zeros((8, d), F32).at[:bsz].set(c).at[bsz].set(c_ctx)
    mods = _ada(cc, w_ada, b_ada)

    consts = _dft_constants(seq)
    cs_ctx = _ctx_dft_constants()
    lb_logits = hg_lb_logits.astype(F32)

    z0 = HG_COLS + HY_COLS
    for l in range(DEPTH):
        last = l == DEPTH - 1
        m = mods[l].reshape(8, 6, d)
        mx = m[:bsz]
        mc = jnp.broadcast_to(m[bsz:bsz + 1], (bsz, 6, d))

        wl = w_in[l]
        w_cat = jnp.concatenate([
            wl[:, :z0], wl[:, z0 + SSD_WIDTH:z0 + SSD_WIDTH + SSD_XBC], wl[:, z0:z0 + SSD_WIDTH],
            wl[:, z0 + SSD_WIDTH + SSD_XBC:], jnp.zeros((d, 128 - 2 * SSD_HEADS), F32)], axis=1).astype(BF16)
        cw_cat = jnp.concatenate([hy_conv_w[l], ssd_conv_w[l]], axis=1)
        cb_cat = jnp.concatenate([hy_conv_b[l], ssd_conv_b[l]]).reshape(1, -1)
        uhg_x, v_x, x1_x, x2_x, xbc_x, z_x, dt_x = _inproj(x, mx[:, 0:2], norm1_g[l], w_cat, cw_cat, cb_cat,
                                                           512, "in_proj_x")
        uhg_c, v_c, x1_c, x2_c, xbc_c, z_c, dt_c = _inproj(ctx, mc[:, 0:2], norm1_g[l], w_cat, cw_cat, cb_cat,
                                                           CTX_LEN, "in_proj_ctx")

        of_x, ob_x, of_c, ob_c = _gla(uhg_x, uhg_c, lb_logits, l)

        rep = lambda a: jnp.repeat(a, SSD_HEAD_DIM, axis=-1)
        prm = dict(dtb_e=rep(ssd_dt_bias[l]), alog_e=rep(ssd_a_log[l]),
                   dtb_t=ssd_dt_bias[l].reshape(-1, 1), alog_t=ssd_a_log[l].reshape(-1, 1),
                   dsk=rep(ssd_d[l]).reshape(1, -1))
        dtt_x = jnp.swapaxes(dt_x[:, :, :2 * SSD_HEADS], 1, 2)
        dtt_c = jnp.swapaxes(dt_c[:, :, :2 * SSD_HEADS], 1, 2)
        yf_x, yb_x, yf_c, yb_c = _ssd(xbc_x, xbc_c, dt_x, dt_c, dtt_x, dtt_c, prm)

        fp = _hyena_filter_params(hy_w1[l], hy_b1[l], hy_freq1[l], hy_w2[l], hy_b2[l], hy_freq2[l], hy_w3[l])
        buf, sumabs = _filt_gen(fp, seq, 1024)
        fa = _fft_stage1(buf.reshape(2, HY_HALVES, FFT_N1, FFT_N1, LANES), consts["f1_real"], consts["twid"])
        hspec = _fft_stage2_filter(fa.reshape(2, HY_HALVES, 2, FFT_N1, FFT_N1, LANES), sumabs, consts["f2"], 2 * seq)
        ohy_x = _hyena_x(v_x, x1_x, x2_x, hspec, hy_bias[l], consts)

        wo = w_out[l].astype(BF16)
        x = _out_proj(of_x, ob_x, uhg_x, ohy_x, yf_x, yb_x, z_x, hg_norm_g[l], ssd_norm_g[l], wo, x,
                      mx[:, 2:3], 512)

        wg, wu = ffn_w_gate[l].astype(BF16), ffn_w_up[l].astype(BF16)
        wd = ffn_w_down[l].astype(BF16)
        cw_f = ffn_conv_w[l].reshape(9, D_FF)
        cb_f = ffn_conv_b[l].reshape(1, D_FF)
        if not last:
            buf_c, sumabs_c = _filt_gen(fp, CTX_LEN, CTX_LEN)
            ohy_c = _hy_ctx(v_c, x1_c, x2_c, buf_c, sumabs_c, cs_ctx, hy_bias[l])
            ctx = _out_proj(of_c, ob_c, uhg_c, ohy_c, yf_c, yb_c, z_c, hg_norm_g[l], ssd_norm_g[l], wo, ctx,
                            mc[:, 2:3], CTX_LEN)
            ctx = _ffn(ctx, mc[:, 3:5], norm2_g[l], wg, wu, cw_f, cb_f, wd, mc[:, 5:6], CTX_LEN, CTX_LEN, False)

        x = _ffn(x, mx[:, 3:5], norm2_g[l], wg, wu, cw_f, cb_f, wd, mx[:, 5:6], 512, GRID_W, True,
                 final_g=final_norm_g if last else None)
    return x
```

```python
import functools
import math

import numpy as np
import jax
import jax.numpy as jnp
from jax import lax
from jax.experimental import pallas as pl
from jax.experimental.pallas import tpu as pltpu

F32 = jnp.float32
BF16 = jnp.bfloat16

D_MODEL = 1024
DEPTH = 2
CTX_LEN = 256
GRID_W = 64
EPS = 1e-6

HG_HEADS = 4
HG_DK = 64
HG_WIDTH = 256
HG_CHUNK = 32

HY_WIDTH = 256
HY_EMB_BANDS = 16
HY_HIDDEN = 64
HY_MIN_DECAY = math.log(1e-2) / 1.5
HY_MAX_DECAY = math.log(1e-2) / 0.3

SSD_HEADS = 8
SSD_HEAD_DIM = 64
SSD_WIDTH = 512
SSD_GROUPS = 2
SSD_STATE = 128
SSD_XBC = 1024
SSD_CHUNK = 64

D_FF = 2816
HG_COLS = 5 * HG_WIDTH
HY_COLS = 3 * HY_WIDTH

LANES = 128
HY_HALVES = HY_WIDTH // LANES
TOKEN_BLOCK = 256
FFT_N1 = 128
VMEM_LIMIT = 56 * 1024 * 1024


def _cparams(sem, big=False):
    kw = dict(dimension_semantics=sem)
    if big:
        kw["vmem_limit_bytes"] = VMEM_LIMIT
    return pltpu.CompilerParams(**kw)


def _dot(a, b):
    return lax.dot_general(a, b, (((1,), (0,)), ((), ())), preferred_element_type=F32)


def _dot_nt(a, b):
    return lax.dot_general(a, b, (((1,), (1,)), ((), ())), preferred_element_type=F32)


def _dot_tn(a, b):
    return lax.dot_general(a, b, (((0,), (0,)), ((), ())), preferred_element_type=F32)


def _split3(a):
    hi = a.astype(BF16)
    r = a - hi.astype(F32)
    mid = r.astype(BF16)
    lo = (r - mid.astype(F32)).astype(BF16)
    return hi, mid, lo


def _dot_exact_lhs(mask_bf16, a):
    h, m, l = _split3(a)
    return _dot(mask_bf16, h) + _dot(mask_bf16, m) + _dot(mask_bf16, l)


def _dot_exact_rhs(a, mask_bf16):
    h, m, l = _split3(a)
    return _dot(h, mask_bf16) + _dot(m, mask_bf16) + _dot(l, mask_bf16)


def _dot3(a, b):
    ah, am, _ = _split3(a)
    bh, bm, _ = _split3(b)
    return _dot(ah, bh) + _dot(ah, bm) + _dot(am, bh)


def _dot3_pre(fparts, b):
    fh, fm = fparts
    bh, bm, _ = _split3(b)
    return _dot(fh, bh) + _dot(fh, bm) + _dot(fm, bh)


def _sigmoid(x):
    return 1.0 / (1.0 + jnp.exp(-x))


def _silu(x):
    return x * _sigmoid(x)


def _softplus(x):
    return jnp.maximum(x, 0.0) + jnp.log(1.0 + jnp.exp(-jnp.abs(x)))


def _log_sigmoid(x):
    return jnp.minimum(x, 0.0) - jnp.log(1.0 + jnp.exp(-jnp.abs(x)))


def _ada_body(c_ref, w_ref, b_ref, o_ref):
    cc = c_ref[...]
    o_ref[0] = _dot3(_silu(cc), w_ref[0]) + b_ref[0]


def _ada(cc, w_ada, b_ada):
    tn = 1536
    n = w_ada.shape[-1]
    return pl.pallas_call(
        _ada_body,
        grid=(DEPTH, n // tn),
        in_specs=[
            pl.BlockSpec((8, D_MODEL), lambda l, j: (0, 0)),
            pl.BlockSpec((1, D_MODEL, tn), lambda l, j: (l, 0, j)),
            pl.BlockSpec((1, 1, tn), lambda l, j: (l, 0, j)),
        ],
        out_specs=pl.BlockSpec((1, 8, tn), lambda l, j: (l, 0, j)),
        out_shape=jax.ShapeDtypeStruct((DEPTH, 8, n), F32),
        compiler_params=_cparams(("arbitrary", "arbitrary"), big=True),
        name="adaln",
    )(cc, w_ada, b_ada.reshape(DEPTH, 1, n))


IN_COLS = HG_COLS + HY_COLS + SSD_WIDTH + SSD_XBC + 2 * SSD_HEADS
COL_HY = HG_COLS
COL_Z = COL_HY + HY_COLS
COL_XBC = COL_Z + SSD_WIDTH
COL_DT = COL_XBC + SSD_XBC
IN_COLS_PAD = COL_DT + LANES
DT_LANE0 = 0
CONV_CHUNK = 256

MOD_SHIFT1, MOD_SCALE1, MOD_GATE1, MOD_SHIFT2, MOD_SCALE2, MOD_GATE2 = range(6)


def _layer_spec(a, l):
    nd = a.ndim - 1
    return pl.BlockSpec((None,) + tuple(a.shape[1:]), lambda b, t: (l,) + (0,) * nd)


def _mod_spec(l, ctx_row, idx):
    row = (lambda b: b) if ctx_row is None else (lambda b: ctx_row)
    return pl.BlockSpec((None, None, None, 1, D_MODEL), lambda b, t: (l, row(b), idx, 0, 0))


def _inproj_body(x_ref, hp_ref, hn_ref, sh_ref, sc_ref, g_ref, w_ref, hcw, hcb, scw, scb,
                 uhg_ref, v_ref, x1_ref, x2_ref, xbc_ref, z_ref, dt_ref, *, nt):
    t = pl.program_id(1)

    def normmod(x):
        ms = jnp.mean(x * x, axis=-1, keepdims=True)
        y = x * lax.rsqrt(ms + EPS) * g_ref[...]
        return (y * (1.0 + sc_ref[...]) + sh_ref[...]).astype(BF16)

    yb = normmod(x_ref[0])
    yh = normmod(jnp.concatenate([hp_ref[0], hn_ref[0]], axis=0))
    for j in range(0, HG_COLS, 512):
        w = min(512, HG_COLS - j)
        uhg_ref[0, :, j:j + w] = _dot(yb, w_ref[:, j:j + w])
    z_ref[0] = _dot(yb, w_ref[:, COL_Z:COL_Z + SSD_WIDTH])
    dt_ref[0] = _dot(yb, w_ref[:, COL_DT:])

    rows = yb.shape[0]
    ri = lax.broadcasted_iota(jnp.int32, (rows, CONV_CHUNK), 0)

    def conv_chunk(col0, cw_ref, cb_ref, k):
        cols = slice(col0 + k * CONV_CHUNK, col0 + (k + 1) * CONV_CHUNK)
        ccols = slice(k * CONV_CHUNK, (k + 1) * CONV_CHUNK)
        u = _dot(yb, w_ref[:, cols])
        uh = _dot(yh, w_ref[:, cols])
        prev_row = jnp.where(t > 0, uh[7:8, :], 0.0)
        next_row = jnp.where(t < nt - 1, uh[8:9, :], 0.0)
        dn = jnp.where(ri == 0, prev_row, pltpu.roll(u, 1, 0))
        up = jnp.where(ri == rows - 1, next_row, pltpu.roll(u, rows - 1, 0))
        return dn * cw_ref[0:1, ccols] + u * cw_ref[1:2, ccols] + up * cw_ref[2:3, ccols] + cb_ref[:, ccols]

    for k, ref in enumerate((v_ref, x1_ref, x2_ref)):
        cv = conv_chunk(COL_HY, hcw, hcb, k)
        for hf in range(HY_HALVES):
            ref[hf, 0] = cv[:, hf * LANES:(hf + 1) * LANES]
    for k in range(SSD_XBC // CONV_CHUNK):
        cv = conv_chunk(COL_XBC, scw, scb, k)
        xbc_ref[0, :, k * CONV_CHUNK:(k + 1) * CONV_CHUNK] = _silu(cv).astype(xbc_ref.dtype)


def _inproj(x, mods, l, ctx_row, g, w, hcw, hcb, scw, scb, tm, name):
    bsz, seq, d = x.shape
    nt = seq // tm
    hb = tm // 8
    tok = lambda n: pl.BlockSpec((1, tm, n), lambda b, t: (b, t, 0))
    hy_spec = pl.BlockSpec((HY_HALVES, 1, tm, LANES), lambda b, t: (0, b, t, 0))
    hy_shape = jax.ShapeDtypeStruct((HY_HALVES, bsz, seq, LANES), F32)
    tshape = lambda n: jax.ShapeDtypeStruct((bsz, seq, n), F32)
    return pl.pallas_call(
        functools.partial(_inproj_body, nt=nt),
        grid=(bsz, nt),
        in_specs=[
            tok(d),
            pl.BlockSpec((1, 8, d), lambda b, t: (b, jnp.maximum(t * hb - 1, 0), 0)),
            pl.BlockSpec((1, 8, d), lambda b, t: (b, jnp.minimum((t + 1) * hb, nt * hb - 1), 0)),
            _mod_spec(l, ctx_row, MOD_SHIFT1), _mod_spec(l, ctx_row, MOD_SCALE1),
            _layer_spec(g, l), _layer_spec(w, l),
            _layer_spec(hcw, l), _layer_spec(hcb, l), _layer_spec(scw, l), _layer_spec(scb, l),
        ],
        out_specs=[tok(HG_COLS), hy_spec, hy_spec, hy_spec, tok(SSD_XBC), tok(SSD_WIDTH), tok(LANES)],
        out_shape=[tshape(HG_COLS), hy_shape, hy_shape, hy_shape,
                   jax.ShapeDtypeStruct((bsz, seq, SSD_XBC), BF16), tshape(SSD_WIDTH), tshape(LANES)],
        compiler_params=_cparams(("parallel", "arbitrary"), big=True),
        name=name,
    )(x, x, x, mods, mods, g, w, hcw, hcb, scw, scb)


def _gla_dir(q, a, v, p, r1, st_ref, reverse):
    tb, ch = TOKEN_BLOCK, HG_CHUNK
    nch = tb // ch
    qv = r1 + _log_sigmoid(a)
    logf = jnp.maximum(p, qv) + jnp.log(1.0 + jnp.exp(-jnp.abs(p - qv)))
    k = 1.0 - jnp.exp(logf)

    ri = lax.broadcasted_iota(jnp.int32, (tb, tb), 0)
    ci = lax.broadcasted_iota(jnp.int32, (tb, tb), 1)
    same = (ri // ch) == (ci // ch)
    tri = (ci >= ri) if reverse else (ci <= ri)
    mask = jnp.logical_and(same, tri)
    mask_b = jnp.where(mask, 1.0, 0.0).astype(BF16)
    bdiag = (ri // HG_DK) == (ci // HG_DK)

    lh, lm, ll = _split3(logf)
    bcs3 = _dot(mask_b, jnp.concatenate([lh, lm, ll], axis=1))
    bcs = bcs3[:, :tb] + bcs3[:, tb:2 * tb] + bcs3[:, 2 * tb:]
    b3 = bcs.reshape(nch, ch, tb)
    mid = ch // 2 if reverse else ch // 2 - 1
    end = 0 if reverse else ch - 1
    b_mid = b3[:, mid:mid + 1, :]
    b_end = b3[:, end:end + 1, :]
    q3 = q.reshape(nch, ch, tb)
    k3 = k.reshape(nch, ch, tb)
    d1 = b3 - b_mid
    qd = (q3 * jnp.exp(d1)).reshape(tb, tb)
    kd = (k3 * jnp.exp(-d1)).reshape(tb, tb).astype(BF16)
    kup = (k3 * jnp.exp(b_end - b3)).astype(BF16)
    qb = (q3 * jnp.exp(b3)).astype(BF16)
    dec = jnp.exp(b_end)

    lane_head = lax.broadcasted_iota(jnp.int32, (1, tb), 1) // HG_DK
    acc = jnp.zeros((tb, tb), F32)
    for h in range(HG_HEADS):
        sel = lane_head == h
        qh = jnp.where(sel, qd, 0.0).astype(BF16)
        s = _dot_nt(qh, kd)
        s = jnp.where(mask, s, 0.0).astype(BF16)
        vh = jnp.where(sel, v, 0.0).astype(BF16)
        acc = acc + _dot(s, vh)

    v3 = v.astype(BF16).reshape(nch, ch, tb)
    inter = [None] * nch
    order = range(nch - 1, -1, -1) if reverse else range(nch)
    st = st_ref[...]
    for c in order:
        inter[c] = _dot_nt(qb[c], st.astype(BF16))
        upd = _dot_tn(v3[c], kup[c])
        st = st * dec[c] + jnp.where(bdiag, upd, 0.0)
    st_ref[...] = st
    return acc + jnp.concatenate(inter, axis=0)


def _gla_body(qf, ff, vf, qb, fb, vb, qc, ffc, fbc, vc, lb_ref, of_x, ob_x, of_c, ob_c, stf, stb, *, layer):
    t = pl.program_id(1)
    is_ctx = t == 0

    @pl.when(is_ctx)
    def _():
        stf[...] = jnp.zeros_like(stf)
        stb[...] = jnp.zeros_like(stb)

    lg = lb_ref[...]
    e = jnp.exp(lg - jnp.max(lg, axis=0, keepdims=True))
    den = jnp.sum(e, axis=0, keepdims=True)
    num = jnp.zeros_like(den)
    for r in range(1, layer + 1):
        num = num + e[r:r + 1, :]
    lb = num / den
    p = jnp.log(lb)
    r1 = jnp.log(1.0 - lb)

    scale = HG_DK ** -0.5
    o_f, o_b = [], []
    for b in range(qf.shape[0]):
        o_f.append(_gla_dir(jnp.where(is_ctx, qc[b], qf[b]) * scale, jnp.where(is_ctx, ffc[b], ff[b]),
                            jnp.where(is_ctx, vc[b], vf[b]), p, r1, stf.at[b], False))
        o_b.append(_gla_dir(jnp.where(is_ctx, qc[b], qb[b]) * scale, jnp.where(is_ctx, fbc[b], fb[b]),
                            jnp.where(is_ctx, vc[b], vb[b]), p, r1, stb.at[b], True))

    @pl.when(is_ctx)
    def _():
        for b in range(len(o_f)):
            of_c[b] = o_f[b].astype(of_c.dtype)
            ob_c[b] = o_b[b].astype(ob_c.dtype)

    @pl.when(jnp.logical_not(is_ctx))
    def _():
        for b in range(len(o_f)):
            of_x[b] = o_f[b].astype(of_x.dtype)
            ob_x[b] = o_b[b].astype(ob_x.dtype)


def _scan_block_maps(nxb):
    fwd = lambda t: jnp.maximum(t - 1, 0)
    bwd = lambda t: jnp.where(t == 0, nxb - 1, nxb - t)
    return fwd, bwd


def _gla(u_x, u_c, lb_logits, layer):
    bsz, seq, _ = u_x.shape
    tb = TOKEN_BLOCK
    nxb = seq // tb
    xf, xb = _scan_block_maps(nxb)

    def xs(blk, col):
        return pl.BlockSpec((1, tb, HG_WIDTH), lambda b, t: (b, blk(t), col))

    def cs(col):
        return pl.BlockSpec((1, tb, HG_WIDTH), lambda b, t: (b, 0, col))

    out_x = jax.ShapeDtypeStruct((bsz, seq, HG_WIDTH), BF16)
    out_c = jax.ShapeDtypeStruct((bsz, CTX_LEN, HG_WIDTH), BF16)
    state = pltpu.VMEM((1, HG_WIDTH, HG_WIDTH), F32)
    return pl.pallas_call(
        functools.partial(_gla_body, layer=layer),
        grid=(bsz, nxb + 1),
        in_specs=[xs(xf, 0), xs(xf, 1), xs(xf, 3), xs(xb, 0), xs(xb, 2), xs(xb, 3),
                  cs(0), cs(1), cs(2), cs(3),
                  pl.BlockSpec((DEPTH, HG_WIDTH), lambda b, t: (0, 0))],
        out_specs=[xs(xf, 0), xs(xb, 0), cs(0), cs(0)],
        out_shape=[out_x, out_x, out_c, out_c],
        scratch_shapes=[state, state],
        compiler_params=_cparams(("parallel", "arbitrary"), big=True),
        name="hgrn2_scan",
    )(u_x, u_x, u_x, u_x, u_x, u_x, u_c, u_c, u_c, u_c, lb_logits)


def _ssd_dir(xbc, dt_raw, dtt_all, dtb_e, alog_e, dtb_t, alog_t, dsk, st_ref, reverse):
    tb, ch = TOKEN_BLOCK, SSD_CHUNK
    nch = tb // ch
    d = 1 if reverse else 0
    xs = xbc[:, :SSD_WIDTH].astype(F32)
    bm = xbc[:, SSD_WIDTH:SSD_WIDTH + 256].astype(BF16)
    cm = xbc[:, SSD_WIDTH + 256:].astype(BF16)

    ri = lax.broadcasted_iota(jnp.int32, (tb, tb), 0)
    ci = lax.broadcasted_iota(jnp.int32, (tb, tb), 1)
    same = (ri // ch) == (ci // ch)
    mask = jnp.logical_and(same, (ci >= ri) if reverse else (ci <= ri))
    mask_b = jnp.where(mask, 1.0, 0.0).astype(BF16)
    mask_t = jnp.logical_and(same, (ri >= ci) if reverse else (ri <= ci))
    mask_tb = jnp.where(mask_t, 1.0, 0.0).astype(BF16)

    ej = lax.broadcasted_iota(jnp.int32, (128, SSD_WIDTH), 0)
    el = lax.broadcasted_iota(jnp.int32, (128, SSD_WIDTH), 1)
    expand = jnp.where(ej == DT_LANE0 + d * SSD_HEADS + el // SSD_HEAD_DIM, 1.0, 0.0).astype(BF16)
    dh, dm, dl = _split3(dt_raw)
    dte_raw = _dot(jnp.concatenate([dh, dm, dl], axis=1), jnp.concatenate([expand] * 3, axis=0))
    dte = _softplus(dte_raw + dtb_e[d:d + 1, :])
    a_e = -jnp.exp(alog_e[d:d + 1, :])
    ah, am, al = _split3(dte * a_e)
    acs3p = _dot(mask_b, jnp.concatenate([ah, am, al], axis=1))
    acs = acs3p[:, :SSD_WIDTH] + acs3p[:, SSD_WIDTH:2 * SSD_WIDTH] + acs3p[:, 2 * SSD_WIDTH:]
    xdt = xs * dte
    acs3 = acs.reshape(nch, ch, SSD_WIDTH)
    end = 0 if reverse else ch - 1
    a_end = acs3[:, end:end + 1, :]
    xw = (xdt.reshape(nch, ch, SSD_WIDTH) * jnp.exp(a_end - acs3)).astype(BF16).reshape(tb, SSD_WIDTH)
    ea = jnp.exp(acs)
    dec = jnp.exp(a_end)

    dtt_raw = dtt_all[d * SSD_HEADS:(d + 1) * SSD_HEADS, :]
    dtt = _softplus(dtt_raw + dtb_t[d * SSD_HEADS:(d + 1) * SSD_HEADS, :])
    a_t = -jnp.exp(alog_t[d * SSD_HEADS:(d + 1) * SSD_HEADS, :])
    th, tm_, tl = _split3(dtt * a_t)
    acs_t3 = _dot(jnp.concatenate([th, tm_, tl], axis=0), mask_tb)
    acs_t = acs_t3[:SSD_HEADS] + acs_t3[SSD_HEADS:2 * SSD_HEADS] + acs_t3[2 * SSD_HEADS:]

    def chunk_diag(m):
        z = jnp.zeros((ch, m.shape[1]), m.dtype)
        return jnp.concatenate(
            [jnp.concatenate([m[c * ch:(c + 1) * ch] if k == c else z for k in range(nch)], axis=1)
             for c in range(nch)], axis=0)

    xdt_b = xdt.astype(BF16)
    hd = SSD_HEAD_DIM
    hpg = SSD_HEADS // SSD_GROUPS
    gw = SSD_WIDTH // SSD_GROUPS
    rr = lax.broadcasted_iota(jnp.int32, (tb, gw), 0)
    ll = lax.broadcasted_iota(jnp.int32, (tb, gw), 1)
    tri4 = ((ll % hd) >= (rr % ch)) if reverse else ((ll % hd) <= (rr % ch))
    head_diag = (rr // hd) == (ll // hd)
    order = range(nch - 1, -1, -1) if reverse else range(nch)
    outs = []
    for g in range(SSD_GROUPS):
        st_cols = slice(g * SSD_STATE, (g + 1) * SSD_STATE)
        cols = slice(g * gw, (g + 1) * gw)
        cm_d = chunk_diag(cm[:, st_cols])
        bm_cat = jnp.concatenate([bm[c * ch:(c + 1) * ch, st_cols] for c in range(nch)], axis=1)
        cb4 = _dot_nt(cm_d, jnp.concatenate([bm_cat] * hpg, axis=0))
        rowm = jnp.concatenate(
            [jnp.broadcast_to(jnp.concatenate([acs_t[g * hpg + h:g * hpg + h + 1, c * ch:(c + 1) * ch]
                                               for h in range(hpg)], axis=1), (ch, gw))
             for c in range(nch)], axis=0)
        w4 = (cb4 * jnp.exp(jnp.where(tri4, acs[:, cols] - rowm, -1e30))).astype(BF16)
        xg = xdt_b[:, cols]
        zero = jnp.zeros((hpg * ch, gw), BF16)
        x4 = jnp.concatenate(
            [jnp.where(head_diag, jnp.concatenate([xg[c * ch:(c + 1) * ch]] * hpg, axis=0), zero)
             for c in range(nch)], axis=0)
        y_intra = _dot(chunk_diag(w4), x4)

        upd = _dot_tn(bm[:, st_cols], chunk_diag(xw[:, cols]))
        st = st_ref[g]
        entering = [None] * nch
        for c in order:
            entering[c] = st.astype(BF16)
            st = st * dec[c][:, cols] + upd[:, c * gw:(c + 1) * gw]
        st_ref[g] = st
        y_inter = _dot(cm_d, jnp.concatenate(entering, axis=0)) * ea[:, cols]
        outs.append(y_intra + y_inter)
    o = jnp.concatenate(outs, axis=1)
    if not reverse:
        o = o + dsk[...] * xs
    return o


def _ssd_body(xf, dtf, dttf, xb, dtb, dttb, xc, dtc, dttc, dtb_e, alog_e, dtb_t, alog_t, dsk,
              of_x, ob_x, of_c, ob_c, stf, stb):
    t = pl.program_id(1)
    is_ctx = t == 0

    @pl.when(is_ctx)
    def _():
        stf[...] = jnp.zeros_like(stf)
        stb[...] = jnp.zeros_like(stb)

    prm = (dtb_e, alog_e, dtb_t, alog_t, dsk)
    o_f = _ssd_dir(jnp.where(is_ctx, xc[0], xf[0]), jnp.where(is_ctx, dtc[0], dtf[0]),
                   jnp.where(is_ctx, dttc[0], dttf[0]), *prm, stf, False)
    o_b = _ssd_dir(jnp.where(is_ctx, xc[0], xb[0]), jnp.where(is_ctx, dtc[0], dtb[0]),
                   jnp.where(is_ctx, dttc[0], dttb[0]), *prm, stb, True)

    @pl.when(is_ctx)
    def _():
        of_c[0] = o_f.astype(of_c.dtype)
        ob_c[0] = o_b.astype(ob_c.dtype)

    @pl.when(jnp.logical_not(is_ctx))
    def _():
        of_x[0] = o_f.astype(of_x.dtype)
        ob_x[0] = o_b.astype(ob_x.dtype)


def _ssd(xbc_x, xbc_c, dt_x, dt_c, dtt_x, dtt_c, prm, l):
    bsz, seq, _ = xbc_x.shape
    tb = TOKEN_BLOCK
    nxb = seq // tb
    xf, xb = _scan_block_maps(nxb)
    full = lambda a: _layer_spec(a, l)
    params = [prm["dtb_e"], prm["alog_e"], prm["dtb_t"], prm["alog_t"], prm["dsk"]]

    def xspecs(blk):
        return [pl.BlockSpec((1, tb, SSD_XBC), lambda b, t: (b, blk(t), 0)),
                pl.BlockSpec((1, tb, 128), lambda b, t: (b, blk(t), 0)),
                pl.BlockSpec((1, 16, tb), lambda b, t: (b, 0, blk(t)))]

    zero = lambda t: 0
    yspec = lambda blk: pl.BlockSpec((1, tb, SSD_WIDTH), lambda b, t: (b, blk(t), 0))
    out_x = jax.ShapeDtypeStruct((bsz, seq, SSD_WIDTH), BF16)
    out_c = jax.ShapeDtypeStruct((bsz, CTX_LEN, SSD_WIDTH), BF16)
    state = pltpu.VMEM((SSD_GROUPS, SSD_STATE, SSD_WIDTH // SSD_GROUPS), F32)
    return pl.pallas_call(
        _ssd_body,
        grid=(bsz, nxb + 1),
        in_specs=xspecs(xf) + xspecs(xb) + xspecs(zero) + [full(a) for a in params],
        out_specs=[yspec(xf), yspec(xb), yspec(zero), yspec(zero)],
        out_shape=[out_x, out_x, out_c, out_c],
        scratch_shapes=[state, state],
        compiler_params=_cparams(("parallel", "arbitrary"), big=True),
        name="ssd_scan",
    )(xbc_x, dt_x, dtt_x, xbc_x, dt_x, dtt_x, xbc_c, dt_c, dtt_c, *params)


def _filt_body(w1, b1, f1, w2, b2, f2, w3, delta, buf_ref, sa_ref, *, seq_len, tr):
    i = pl.program_id(0)
    hr = tr // 2
    lane = lax.broadcasted_iota(jnp.int32, (hr, 128), 1)
    grp, sub = lane // HY_HIDDEN, lane % HY_HIDDEN
    jrow = i * tr + grp * hr + lax.broadcasted_iota(jnp.int32, (hr, 128), 0)
    pos = jnp.where(jrow < seq_len, jrow, 2 * seq_len - jrow).astype(F32)
    tpos = pos / (seq_len - 1.0)
    wpos = pos * (2.0 * math.pi / seq_len)
    bidx = jnp.where(sub <= HY_EMB_BANDS, sub - 1, sub - 1 - HY_EMB_BANDS).astype(F32)
    band = 1e-4 + bidx * ((HY_EMB_BANDS - 1 - 1e-4) / (HY_EMB_BANDS - 1))
    ang = band * wpos + jnp.where(sub > HY_EMB_BANDS, 0.5 * math.pi, 0.0)
    z = jnp.where(sub == 0, tpos, jnp.where(sub <= 2 * HY_EMB_BANDS, jnp.cos(ang), 0.0))
    h = jnp.sin(f1[...] * (_dot3(z, w1[...]) + b1[...]))
    h = jnp.sin(f2[...] * (_dot3(h, w2[...]) + b2[...]))

    @pl.when(i == 0)
    def _():
        sa_ref[...] = jnp.zeros_like(sa_ref)

    for g in range(2):
        o = _dot3(h, w3[0, g])
        o = o * jnp.exp(-tpos[:, g * HY_HIDDEN:g * HY_HIDDEN + 1] * delta[...])
        o = jnp.where(jrow[:, g * HY_HIDDEN:g * HY_HIDDEN + 1] == seq_len, 0.0, o)
        for order in range(2):
            for hf in range(HY_HALVES):
                lo = order * HY_WIDTH + hf * LANES
                buf_ref[order, hf, g * hr:(g + 1) * hr, :] = o[:, lo:lo + LANES]
        sa_ref[...] = sa_ref[...] + jnp.sum(jnp.abs(o), axis=0, keepdims=True)


def _filt_gen(p, seq_len, tr):
    n = 2 * seq_len
    assert seq_len % tr == 0
    half = seq_len // tr
    full = lambda a: pl.BlockSpec(a.shape, lambda i: (0,) * a.ndim)
    small = [p["w1"], p["b1"], p["f1"], p["w2"], p["b2"], p["f2"]]
    return pl.pallas_call(
        functools.partial(_filt_body, seq_len=seq_len, tr=tr),
        grid=(n // tr,),
        in_specs=[full(a) for a in small] + [
            pl.BlockSpec((1, 2, 2 * HY_HIDDEN, 2 * HY_WIDTH), lambda i: (i // half, 0, 0, 0)),
            full(p["delta"]),
        ],
        out_specs=[pl.BlockSpec((2, HY_HALVES, tr, LANES), lambda i: (0, 0, i, 0)),
                   pl.BlockSpec((8, 2 * HY_WIDTH), lambda i: (0, 0))],
        out_shape=[jax.ShapeDtypeStruct((2, HY_HALVES, n, LANES), F32),
                   jax.ShapeDtypeStruct((8, 2 * HY_WIDTH), F32)],
        compiler_params=_cparams(("arbitrary",)),
        name="hyena_filter",
    )(*small, p["w3"], p["delta"])


FFT_NB = 8


def _strided_rows(ref3, j, n):
    flat = ref3.reshape(ref3.shape[0] * FFT_NB, LANES)
    return flat[pl.ds(j, n, stride=FFT_NB), :]


def _store_strided_rows(ref3, j, val):
    flat = ref3.reshape(ref3.shape[0] * FFT_NB, LANES)
    flat[pl.ds(j, val.shape[0], stride=FFT_NB), :] = val


def _s1_body(x_ref, f_ref, t_ref, o_ref):
    rows = x_ref.shape[2]
    for j in range(FFT_NB):
        x = jnp.concatenate([_strided_rows(x_ref.at[0, hf], j, rows) for hf in range(HY_HALVES)], axis=1)
        a = _dot3_pre((f_ref[0], f_ref[1]), x)
        tr = t_ref[0, 0, :, j:j + 1]
        ti = t_ref[0, 1, :, j:j + 1]
        ar, ai = a[:FFT_N1], a[FFT_N1:]
        out = jnp.concatenate([ar * tr - ai * ti, ar * ti + ai * tr], axis=0)
        for hf in range(HY_HALVES):
            _store_strided_rows(o_ref.at[0, hf], j, out[:, hf * LANES:(hf + 1) * LANES])


def _fft_stage1(x, fmat, twid):
    g, _, rows, n2, _ = x.shape
    return pl.pallas_call(
        _s1_body,
        grid=(g, n2 // FFT_NB),
        in_specs=[
            pl.BlockSpec((1, HY_HALVES, rows, FFT_NB, LANES), lambda i, j: (i, 0, 0, j, 0)),
            pl.BlockSpec(fmat.shape, lambda i, j: (0, 0, 0)),
            pl.BlockSpec((1, 2, FFT_N1, FFT_NB), lambda i, j: (j, 0, 0, 0)),
        ],
        out_specs=pl.BlockSpec((1, HY_HALVES, 2 * FFT_N1, FFT_NB, LANES), lambda i, j: (i, 0, 0, j, 0)),
        out_shape=jax.ShapeDtypeStruct((g, HY_HALVES, 2 * FFT_N1, n2, LANES), F32),
        compiler_params=_cparams(("parallel", "arbitrary"), big=True),
        name="fft_stage1",
    )(x, fmat, twid)


def _load_complex(ref, k):
    return jnp.concatenate(
        [jnp.concatenate([ref[hf, 0, k], ref[hf, 1, k]], axis=0) for hf in range(HY_HALVES)], axis=1)


def _store_complex(ref, k, val):
    for hf in range(HY_HALVES):
        ref[hf, 0, k] = val[:FFT_N1, hf * LANES:(hf + 1) * LANES].astype(ref.dtype)
        ref[hf, 1, k] = val[FFT_N1:, hf * LANES:(hf + 1) * LANES].astype(ref.dtype)


def _s2_body(a_ref, h_ref, f_ref, fc_ref, o_ref, *, kb):
    for k in range(kb):
        x = _dot3_pre((f_ref[0], f_ref[1]), _load_complex(a_ref, k))
        xr, xi = x[:FFT_N1], x[FFT_N1:]
        h = _load_complex(h_ref.at[0], k).astype(F32)
        hr, hi = h[:FFT_N1], h[FFT_N1:]
        y = jnp.concatenate([xr * hr - xi * hi, xr * hi + xi * hr], axis=0)
        _store_complex(o_ref, k, _dot3_pre((fc_ref[0], fc_ref[1]), y))


def _fft_stage2(a, hspec, order, fmat, fmat_c, kb=8):
    blk = pl.BlockSpec((HY_HALVES, 2, kb, FFT_N1, LANES), lambda i: (0, 0, i, 0, 0))
    return pl.pallas_call(
        functools.partial(_s2_body, kb=kb),
        grid=(FFT_N1 // kb,),
        in_specs=[blk,
                  pl.BlockSpec((1, HY_HALVES, 2, kb, FFT_N1, LANES), lambda i: (order, 0, 0, i, 0, 0)),
                  pl.BlockSpec(fmat.shape, lambda i: (0, 0, 0)),
                  pl.BlockSpec(fmat_c.shape, lambda i: (0, 0, 0))],
        out_specs=blk,
        out_shape=jax.ShapeDtypeStruct(a.shape, F32),
        compiler_params=_cparams(("arbitrary",), big=True),
        name="fft_stage2_mul",
    )(a, hspec, fmat, fmat_c)


def _s2f_body(a_ref, sa_ref, f_ref, o_ref, *, kb, n):
    o = pl.program_id(0)
    sa = jnp.where(o == 0, sa_ref[0:1, :HY_WIDTH], sa_ref[0:1, HY_WIDTH:])
    scale = 1.0 / (sa * float(n))
    for k in range(kb):
        x = _dot3_pre((f_ref[0], f_ref[1]), _load_complex(a_ref.at[0], k))
        _store_complex(o_ref.at[0], k, x * scale)


def _fft_stage2_filter(a, sumabs, fmat, n, kb=8):
    blk = pl.BlockSpec((1, HY_HALVES, 2, kb, FFT_N1, LANES), lambda o, i: (o, 0, 0, i, 0, 0))
    return pl.pallas_call(
        functools.partial(_s2f_body, kb=kb, n=n),
        grid=(2, FFT_N1 // kb),
        in_specs=[blk, pl.BlockSpec(sumabs.shape, lambda o, i: (0, 0)),
                  pl.BlockSpec(fmat.shape, lambda o, i: (0, 0, 0))],
        out_specs=blk,
        out_shape=jax.ShapeDtypeStruct(a.shape, BF16),
        compiler_params=_cparams(("parallel", "arbitrary"), big=True),
        name="fft_stage2_filter",
    )(a, sumabs, fmat)


def _s3_body(c_ref, t_ref, f_ref, gate_ref, v_ref, bias_ref, o_ref):
    rows = gate_ref.shape[1]
    for j in range(FFT_NB):
        tr = t_ref[0, 0, :, j:j + 1]
        ti = t_ref[0, 1, :, j:j + 1]
        c = jnp.concatenate([_strided_rows(c_ref.at[hf], j, 2 * FFT_N1) for hf in range(HY_HALVES)], axis=1)
        cr, ci = c[:FFT_N1], c[FFT_N1:]
        dmat = jnp.concatenate([cr * tr + ci * ti, ci * tr - cr * ti], axis=0)
        w = _dot3_pre((f_ref[0], f_ref[1]), dmat)
        for hf in range(HY_HALVES):
            lanes = slice(hf * LANES, (hf + 1) * LANES)
            g = _strided_rows(gate_ref.at[hf], j, rows)
            v = _strided_rows(v_ref.at[hf], j, rows)
            _store_strided_rows(o_ref.at[hf], j, g * (w[:, lanes] + v * bias_ref[:, lanes]))


def _fft_stage3(c, twid, fmat, gate, v, bias, l, order):
    _, rows, n2, _ = gate.shape
    dspec = pl.BlockSpec((HY_HALVES, rows, FFT_NB, LANES), lambda j: (0, 0, j, 0))
    return pl.pallas_call(
        _s3_body,
        grid=(n2 // FFT_NB,),
        in_specs=[
            pl.BlockSpec((HY_HALVES, 2 * FFT_N1, FFT_NB, LANES), lambda j: (0, 0, j, 0)),
            pl.BlockSpec((1, 2, FFT_N1, FFT_NB), lambda j: (j, 0, 0, 0)),
            pl.BlockSpec(fmat.shape, lambda j: (0, 0, 0)),
            dspec, dspec,
            pl.BlockSpec((None, None, 1, HY_WIDTH), lambda j: (l, order, 0, 0)),
        ],
        out_specs=dspec,
        out_shape=jax.ShapeDtypeStruct(gate.shape, F32),
        compiler_params=_cparams(("arbitrary",), big=True),
        name="fft_stage3_gate",
    )(c, twid, fmat, gate, v, bias)


def _hyc_body(v_ref, x1_ref, x2_ref, buf_ref, sa_ref, cs_ref, bias_ref, o_ref):
    n = 2 * CTX_LEN
    cmat, smat = cs_ref[0], cs_ref[1]
    c_in, s_in = cmat[:, :CTX_LEN], smat[:, :CTX_LEN]
    c_out, s_out = cmat[:CTX_LEN, :], smat[:CTX_LEN, :]
    batch = lambda ref, b: jnp.concatenate([ref[hf, b] for hf in range(HY_HALVES)], axis=1)
    yr, yi = batch(v_ref, 0), batch(v_ref, 1)
    for o, gate in enumerate((x1_ref, x2_ref)):
        bufo = jnp.concatenate([buf_ref[o, hf] for hf in range(HY_HALVES)], axis=1)
        scale = 1.0 / (sa_ref[0:1, o * HY_WIDTH:(o + 1) * HY_WIDTH] * float(n))
        hr = _dot3(cmat, bufo) * scale
        hi = -_dot3(smat, bufo) * scale
        xr = _dot3(c_in, yr) + _dot3(s_in, yi)
        xi = _dot3(c_in, yi) - _dot3(s_in, yr)
        zr = xr * hr - xi * hi
        zi = xr * hi + xi * hr
        wr = _dot3(c_out, zr) - _dot3(s_out, zi)
        wi = _dot3(c_out, zi) + _dot3(s_out, zr)
        b = bias_ref[o:o + 1, :]
        yr = batch(gate, 0) * (wr + yr * b)
        yi = batch(gate, 1) * (wi + yi * b)
    for hf in range(HY_HALVES):
        o_ref[hf, 0] = yr[:, hf * LANES:(hf + 1) * LANES]
        o_ref[hf, 1] = yi[:, hf * LANES:(hf + 1) * LANES]


def _hy_ctx(v, x1, x2, buf, sumabs, cs, bias):
    return pl.pallas_call(
        _hyc_body,
        out_shape=jax.ShapeDtypeStruct(v.shape, F32),
        compiler_params=pltpu.CompilerParams(vmem_limit_bytes=VMEM_LIMIT),
        name="hyena_ctx",
    )(v, x1, x2, buf, sumabs, cs, bias)


def _dft_constants(seq_len):
    n = 2 * seq_len
    n1 = FFT_N1
    assert n == n1 * n1
    idx = np.arange(n1, dtype=np.float64)
    th = 2.0 * np.pi * np.outer(idx, idx) / n1
    fr, fi = np.cos(th), -np.sin(th)
    half = n1 // 2

    def parts(m):
        m32 = jnp.asarray(m, F32)
        hi = m32.astype(BF16)
        mid = (m32 - hi.astype(F32)).astype(BF16)
        return jnp.stack([hi, mid])

    f1_data = np.block([[fr[:, :half], -fi[:, :half]], [fi[:, :half], fr[:, :half]]])
    f1_real = np.concatenate([fr, fi], axis=0)
    f2 = np.block([[fr, -fi], [fi, fr]])
    f2c = np.block([[fr, fi], [-fi, fr]])
    f3 = np.block([[fr[:half], fi[:half]], [-fi[:half], fr[:half]]])
    tw = 2.0 * np.pi * np.outer(idx, idx) / n
    twid = np.stack([np.cos(tw), -np.sin(tw)])
    twid = twid.reshape(2, n1, n1 // FFT_NB, FFT_NB).transpose(2, 0, 1, 3)
    return dict(f1_data=parts(f1_data), f1_real=parts(f1_real), f2=parts(f2), f2c=parts(f2c),
                f3=parts(f3), twid=jnp.asarray(twid, F32))


def _ctx_dft_constants():
    n = 2 * CTX_LEN
    idx = np.arange(n, dtype=np.float64)
    th = 2.0 * np.pi * np.outer(idx, idx) / n
    return jnp.asarray(np.stack([np.cos(th), np.sin(th)]), F32)


def _hyena_filter_params(hy_w1, hy_b1, hy_freq1, hy_w2, hy_b2, hy_freq2, hy_w3):
    hid = HY_HIDDEN
    zeros = jnp.zeros((hid, hid), F32)
    w1 = jnp.zeros((hid, hid), F32).at[:hy_w1.shape[0]].set(hy_w1)
    w1 = jnp.block([[w1, zeros], [zeros, w1]])
    w2 = jnp.block([[hy_w2, zeros], [zeros, hy_w2]])
    w3 = hy_w3.reshape(hid, 2, 2 * HY_WIDTH).transpose(1, 0, 2)
    z3 = jnp.zeros_like(w3)
    w3 = jnp.stack([jnp.concatenate([w3, z3], axis=1), jnp.concatenate([z3, w3], axis=1)], axis=1)
    deltas = np.abs(np.linspace(HY_MIN_DECAY, HY_MAX_DECAY, HY_WIDTH))
    delta = jnp.asarray(np.tile(deltas, 2)[None, :], F32)
    row = lambda a: jnp.tile(a, 2).reshape(1, -1)
    return dict(w1=w1, b1=row(hy_b1), f1=row(hy_freq1), w2=w2, b2=row(hy_b2), f2=row(hy_freq2),
                w3=w3, delta=delta)


def _hyena_x(v, x1, x2, hspec, bias, l, consts):
    nh, bsz, seq, lanes = v.shape
    assert bsz == 2
    rows = bsz * (seq // FFT_N1)
    view = lambda a: a.reshape(nh, rows, FFT_N1, lanes)
    y = view(v)
    for o, gate in enumerate((x1, x2)):
        a = _fft_stage1(y.reshape(1, nh, rows, FFT_N1, lanes), consts["f1_data"], consts["twid"])
        c = _fft_stage2(a.reshape(nh, 2, FFT_N1, FFT_N1, lanes), hspec, o, consts["f2"], consts["f2c"])
        y = _fft_stage3(c.reshape(nh, 2 * FFT_N1, FFT_N1, lanes), consts["twid"], consts["f3"], view(gate), y,
                        bias, l, o)
    return y.reshape(nh, bsz, seq, lanes)


def _out_body(of, ob, ug, hy, yf, yb, zz, hgn, ssn, w_ref, h_ref, gate_ref, o_ref):
    o = of[0].astype(F32) + ob[0].astype(F32)
    gi = lax.broadcasted_iota(jnp.int32, (HG_WIDTH, HG_WIDTH), 0) // HG_DK
    gj = lax.broadcasted_iota(jnp.int32, (HG_WIDTH, HG_WIDTH), 1) // HG_DK
    avg = jnp.where(gi == gj, 1.0 / HG_DK, 0.0).astype(BF16)
    sq = o * o
    sh = sq.astype(BF16)
    sl = (sq - sh.astype(F32)).astype(BF16)
    ms = _dot(sh, avg) + _dot(sl, avg)
    hg = o * lax.rsqrt(ms + EPS) * hgn[...] * _silu(ug[0])
    y = (yf[0].astype(F32) + yb[0].astype(F32)) * _silu(zz[0])
    gw = SSD_WIDTH // SSD_GROUPS
    parts = []
    for g in range(SSD_GROUPS):
        yg = y[:, g * gw:(g + 1) * gw]
        parts.append(yg * lax.rsqrt(jnp.mean(yg * yg, axis=-1, keepdims=True) + EPS))
    ys = jnp.concatenate(parts, axis=1) * ssn[...]
    acc = _dot(hg.astype(BF16), w_ref[0:HG_WIDTH, :])
    hyv = jnp.concatenate([hy[hf, 0] for hf in range(HY_HALVES)], axis=1)
    acc = acc + _dot(hyv.astype(BF16), w_ref[HG_WIDTH:HG_WIDTH + HY_WIDTH, :])
    acc = acc + _dot(ys.astype(BF16), w_ref[HG_WIDTH + HY_WIDTH:, :])
    o_ref[0] = h_ref[0] + gate_ref[...] * acc


def _out_proj(of, ob, u_hg, hy, yf, yb, zgate, hgn, ssn, w, h, mods, l, ctx_row, tm):
    bsz, seq, d = h.shape
    tok = lambda n, col=0: pl.BlockSpec((1, tm, n), lambda b, t: (b, t, col))
    return pl.pallas_call(
        _out_body,
        grid=(bsz, seq // tm),
        in_specs=[tok(HG_WIDTH), tok(HG_WIDTH), tok(HG_WIDTH, 4),
                  pl.BlockSpec((HY_HALVES, 1, tm, LANES), lambda b, t: (0, b, t, 0)),
                  tok(SSD_WIDTH), tok(SSD_WIDTH), tok(SSD_WIDTH),
                  _layer_spec(hgn, l), _layer_spec(ssn, l), _layer_spec(w, l),
                  tok(d),
                  _mod_spec(l, ctx_row, MOD_GATE1)],
        out_specs=tok(d),
        out_shape=jax.ShapeDtypeStruct(h.shape, F32),
        compiler_params=_cparams(("parallel", "arbitrary"), big=True),
        name="mixer_out_proj",
    )(of, ob, u_hg, hy, yf, yb, zgate, hgn, ssn, w, h, mods)


def _ffn_body(*refs, tm, grid_w, vertical, final, cchunk):
    if vertical:
        x_ref, hp, hn, sh_ref, sc_ref, g_ref, wg, wu, cw, cb, wd, gate_ref = refs[:12]
        rest = refs[12:]
    else:
        x_ref, sh_ref, sc_ref, g_ref, wg, wu, cw, cb, wd, gate_ref = refs[:10]
        rest = refs[10:]
    if final:
        fg_ref, o_ref, act_ref = rest
    else:
        o_ref, act_ref = rest
    t = pl.program_id(1)
    nt = pl.num_programs(1)

    def normmod(v):
        ms = jnp.mean(v * v, axis=-1, keepdims=True)
        y = v * lax.rsqrt(ms + EPS) * g_ref[...]
        return (y * (1.0 + sc_ref[...]) + sh_ref[...]).astype(BF16)

    x = x_ref[0]
    yb = normmod(x)
    pad = grid_w if vertical else 0
    ext_rows = tm + 2 * pad
    ye = jnp.concatenate([normmod(hp[0]), yb, normmod(hn[0])], axis=0) if vertical else yb
    col = lax.broadcasted_iota(jnp.int32, (ext_rows, cchunk), 0) % grid_w
    for j in range(0, D_FF, cchunk):
        cs = slice(j, j + cchunk)
        ext = _dot(ye, wg[:, cs])
        if vertical:
            ext = jnp.concatenate([jnp.where(t > 0, ext[:pad], 0.0), ext[pad:pad + tm],
                                   jnp.where(t < nt - 1, ext[pad + tm:], 0.0)], axis=0)
        left = jnp.where(col != 0, pltpu.roll(ext, 1, 0), 0.0)
        right = jnp.where(col != grid_w - 1, pltpu.roll(ext, ext_rows - 1, 0), 0.0)
        acc = jnp.zeros((tm, cchunk), F32) + cb[:, cs]
        for dy in ((-1, 0, 1) if vertical else (0,)):
            r0 = pad + dy * grid_w
            ky = dy + 1
            acc = acc + left[r0:r0 + tm] * cw[3 * ky:3 * ky + 1, cs]
            acc = acc + ext[r0:r0 + tm] * cw[3 * ky + 1:3 * ky + 2, cs]
            acc = acc + right[r0:r0 + tm] * cw[3 * ky + 2:3 * ky + 3, cs]
        act_ref[:, cs] = (_silu(acc) * _dot(yb, wu[:, cs])).astype(BF16)
    y = _dot(act_ref[...], wd[...])
    x = x + gate_ref[...] * y
    if final:
        ms = jnp.mean(x * x, axis=-1, keepdims=True)
        x = x * lax.rsqrt(ms + EPS) * fg_ref[...]
    o_ref[0] = x


def _ffn(h, mods, l, ctx_row, g, wg, wu, cw, cb, wd, tm, grid_w, vertical, final_g=None):
    bsz, seq, d = h.shape
    nhb = tm // grid_w
    tok = lambda n: pl.BlockSpec((1, tm, n), lambda b, t: (b, t, 0))
    full = lambda a: _layer_spec(a, l)
    resident = lambda a: pl.BlockSpec((None,) + tuple(a.shape[1:]), lambda b, t: (l, 0, 0),
                                      pipeline_mode=pl.Buffered(1))
    in_specs = [tok(d)]
    args = [h]
    if vertical:
        nrow = seq // grid_w
        in_specs += [
            pl.BlockSpec((1, grid_w, d), lambda b, t: (b, jnp.maximum(t * nhb - 1, 0), 0)),
            pl.BlockSpec((1, grid_w, d), lambda b, t: (b, jnp.minimum((t + 1) * nhb, nrow - 1), 0)),
        ]
        args += [h, h]
    in_specs += [_mod_spec(l, ctx_row, MOD_SHIFT2), _mod_spec(l, ctx_row, MOD_SCALE2), full(g),
                 resident(wg), resident(wu), full(cw), full(cb), resident(wd),
                 _mod_spec(l, ctx_row, MOD_GATE2)]
    args += [mods, mods, g, wg, wu, cw, cb, wd, mods]
    if final_g is not None:
        in_specs.append(pl.BlockSpec((1, d), lambda b, t: (0, 0)))
        args.append(final_g.reshape(1, d))
    return pl.pallas_call(
        functools.partial(_ffn_body, tm=tm, grid_w=grid_w, vertical=vertical,
                          final=final_g is not None, cchunk=256),
        grid=(bsz, seq // tm),
        in_specs=in_specs,
        out_specs=tok(d),
        out_shape=jax.ShapeDtypeStruct(h.shape, F32),
        scratch_shapes=[pltpu.VMEM((tm, D_FF), BF16)],
        compiler_params=_cparams(("parallel", "arbitrary"), big=True),
        name="ffn",
    )(*args)


def kernel(x, c, ctx, c_ctx, w_ada, b_ada, norm1_g, norm2_g, w_in, w_out, hg_lb_logits, hg_norm_g, hy_conv_w, hy_conv_b, hy_w1, hy_b1, hy_freq1, hy_w2, hy_b2, hy_freq2, hy_w3, hy_bias, ssd_conv_w, ssd_conv_b, ssd_dt_bias, ssd_a_log, ssd_d, ssd_norm_g, ffn_w_gate, ffn_w_up, ffn_conv_w, ffn_conv_b, ffn_w_down, final_norm_g):
    bsz, seq, d = x.shape
    depth = w_in.shape[0]
    cc = jnp.zeros((8, d), F32).at[:bsz].set(c).at[bsz].set(c_ctx)
    mods = _ada(cc, w_ada, b_ada).reshape(depth, 8, 6, 1, d)
    ctx_row = bsz

    consts = _dft_constants(seq)
    cs_ctx = _ctx_dft_constants()
    lb_logits = hg_lb_logits.astype(F32)

    row3 = lambda a: a.reshape(depth, 1, -1)
    rep = lambda a: jnp.repeat(a, SSD_HEAD_DIM, axis=-1)
    w_in_b = jnp.pad(w_in, ((0, 0), (0, 0), (0, IN_COLS_PAD - w_in.shape[-1]))).astype(BF16)
    w_out_b = w_out.astype(BF16)
    wg_b, wu_b, wd_b = ffn_w_gate.astype(BF16), ffn_w_up.astype(BF16), ffn_w_down.astype(BF16)
    g1, g2 = row3(norm1_g), row3(norm2_g)
    hcb, scb = row3(hy_conv_b), row3(ssd_conv_b)
    hgn, ssn = row3(hg_norm_g), row3(ssd_norm_g)
    cw_f, cb_f = ffn_conv_w.reshape(depth, 9, D_FF), row3(ffn_conv_b)
    prm = dict(dtb_e=rep(ssd_dt_bias), alog_e=rep(ssd_a_log),
               dtb_t=ssd_dt_bias.reshape(depth, -1, 1), alog_t=ssd_a_log.reshape(depth, -1, 1),
               dsk=row3(rep(ssd_d)))

    for l in range(depth):
        last = l == depth - 1
        proj = lambda h, row, tm, name: _inproj(h, mods, l, row, g1, w_in_b, hy_conv_w, hcb, ssd_conv_w, scb,
                                                tm, name)
        uhg_x, v_x, x1_x, x2_x, xbc_x, z_x, dt_x = proj(x, None, 512, "in_proj_x")
        uhg_c, v_c, x1_c, x2_c, xbc_c, z_c, dt_c = proj(ctx, ctx_row, CTX_LEN, "in_proj_ctx")

        of_x, ob_x, of_c, ob_c = _gla(uhg_x, uhg_c, lb_logits, l)

        dtt_x = jnp.swapaxes(dt_x[:, :, DT_LANE0:DT_LANE0 + 2 * SSD_HEADS], 1, 2)
        dtt_c = jnp.swapaxes(dt_c[:, :, DT_LANE0:DT_LANE0 + 2 * SSD_HEADS], 1, 2)
        yf_x, yb_x, yf_c, yb_c = _ssd(xbc_x, xbc_c, dt_x, dt_c, dtt_x, dtt_c, prm, l)

        fp = _hyena_filter_params(hy_w1[l], hy_b1[l], hy_freq1[l], hy_w2[l], hy_b2[l], hy_freq2[l], hy_w3[l])
        buf, sumabs = _filt_gen(fp, seq, 1024)
        fa = _fft_stage1(buf.reshape(2, HY_HALVES, FFT_N1, FFT_N1, LANES), consts["f1_real"], consts["twid"])
        hspec = _fft_stage2_filter(fa.reshape(2, HY_HALVES, 2, FFT_N1, FFT_N1, LANES), sumabs, consts["f2"], 2 * seq)
        ohy_x = _hyena_x(v_x, x1_x, x2_x, hspec, hy_bias.reshape(depth, 2, 1, HY_WIDTH), l, consts)

        x = _out_proj(of_x, ob_x, uhg_x, ohy_x, yf_x, yb_x, z_x, hgn, ssn, w_out_b, x, mods, l, None, 512)

        if not last:
            buf_c, sumabs_c = _filt_gen(fp, CTX_LEN, CTX_LEN)
            ohy_c = _hy_ctx(v_c, x1_c, x2_c, buf_c, sumabs_c, cs_ctx, hy_bias[l])
            ctx = _out_proj(of_c, ob_c, uhg_c, ohy_c, yf_c, yb_c, z_c, hgn, ssn, w_out_b, ctx, mods, l, ctx_row,
                            CTX_LEN)
            ctx = _ffn(ctx, mods, l, ctx_row, g2, wg_b, wu_b, cw_f, cb_f, wd_b, CTX_LEN, CTX_LEN, False)

        x = _ffn(x, mods, l, None, g2, wg_b, wu_b, cw_f, cb_f, wd_b, 1024, GRID_W, True,
                 final_g=final_norm_g if last else None)
    return x
```

```python
import functools
import math

import numpy as np
import jax
import jax.numpy as jnp
from jax import lax
from jax.experimental import pallas as pl
from jax.experimental.pallas import tpu as pltpu

F32 = jnp.float32
BF16 = jnp.bfloat16

D_MODEL = 1024
DEPTH = 2
CTX_LEN = 256
GRID_W = 64
EPS = 1e-6

HG_HEADS = 4
HG_DK = 64
HG_WIDTH = 256
HG_CHUNK = 32

HY_WIDTH = 256
HY_EMB_BANDS = 16
HY_HIDDEN = 64
HY_MIN_DECAY = math.log(1e-2) / 1.5
HY_MAX_DECAY = math.log(1e-2) / 0.3

SSD_HEADS = 8
SSD_HEAD_DIM = 64
SSD_WIDTH = 512
SSD_GROUPS = 2
SSD_STATE = 128
SSD_XBC = 1024
SSD_CHUNK = 64

D_FF = 2816
HG_COLS = 5 * HG_WIDTH
HY_COLS = 3 * HY_WIDTH

LANES = 128
HY_HALVES = HY_WIDTH // LANES
TOKEN_BLOCK = 256
FFT_N1 = 128
VMEM_LIMIT = 56 * 1024 * 1024


def _cparams(sem, big=False):
    kw = dict(dimension_semantics=sem)
    if big:
        kw["vmem_limit_bytes"] = VMEM_LIMIT
    return pltpu.CompilerParams(**kw)


def _dot(a, b):
    return lax.dot_general(a, b, (((1,), (0,)), ((), ())), preferred_element_type=F32)


def _dot_nt(a, b):
    return lax.dot_general(a, b, (((1,), (1,)), ((), ())), preferred_element_type=F32)


def _dot_tn(a, b):
    return lax.dot_general(a, b, (((0,), (0,)), ((), ())), preferred_element_type=F32)


def _split3(a):
    hi = a.astype(BF16)
    r = a - hi.astype(F32)
    mid = r.astype(BF16)
    lo = (r - mid.astype(F32)).astype(BF16)
    return hi, mid, lo


def _dot_exact_lhs(mask_bf16, a):
    h, m, l = _split3(a)
    return _dot(mask_bf16, h) + _dot(mask_bf16, m) + _dot(mask_bf16, l)


def _dot_exact_rhs(a, mask_bf16):
    h, m, l = _split3(a)
    return _dot(h, mask_bf16) + _dot(m, mask_bf16) + _dot(l, mask_bf16)


def _dot3(a, b):
    ah, am, _ = _split3(a)
    bh, bm, _ = _split3(b)
    return _dot(ah, bh) + _dot(ah, bm) + _dot(am, bh)


def _dot3_pre(fparts, b):
    fh, fm = fparts
    bh, bm, _ = _split3(b)
    return _dot(fh, bh) + _dot(fh, bm) + _dot(fm, bh)


def _sigmoid(x):
    return 1.0 / (1.0 + jnp.exp(-x))


def _silu(x):
    return x * _sigmoid(x)


def _softplus(x):
    return jnp.maximum(x, 0.0) + jnp.log(1.0 + jnp.exp(-jnp.abs(x)))


def _log_sigmoid(x):
    return jnp.minimum(x, 0.0) - jnp.log(1.0 + jnp.exp(-jnp.abs(x)))


def _ada_body(c_ref, w_ref, b_ref, o_ref):
    cc = c_ref[...]
    o_ref[0] = _dot3(_silu(cc), w_ref[0]) + b_ref[0]


def _ada(cc, w_ada, b_ada):
    tn = 1536
    n = w_ada.shape[-1]
    return pl.pallas_call(
        _ada_body,
        grid=(DEPTH, n // tn),
        in_specs=[
            pl.BlockSpec((8, D_MODEL), lambda l, j: (0, 0)),
            pl.BlockSpec((1, D_MODEL, tn), lambda l, j: (l, 0, j)),
            pl.BlockSpec((1, 1, tn), lambda l, j: (l, 0, j)),
        ],
        out_specs=pl.BlockSpec((1, 8, tn), lambda l, j: (l, 0, j)),
        out_shape=jax.ShapeDtypeStruct((DEPTH, 8, n), F32),
        compiler_params=_cparams(("arbitrary", "arbitrary"), big=True),
        name="adaln",
    )(cc, w_ada, b_ada.reshape(DEPTH, 1, n))


IN_COLS = HG_COLS + HY_COLS + SSD_WIDTH + SSD_XBC + 2 * SSD_HEADS
COL_HY = HG_COLS
COL_Z = COL_HY + HY_COLS
COL_XBC = COL_Z + SSD_WIDTH
COL_DT = COL_XBC + SSD_XBC
IN_COLS_PAD = COL_DT + LANES
DT_LANE0 = 0
CONV_CHUNK = 256

MOD_SHIFT1, MOD_SCALE1, MOD_GATE1, MOD_SHIFT2, MOD_SCALE2, MOD_GATE2 = range(6)


def _layer_spec(a, l):
    nd = a.ndim - 1
    return pl.BlockSpec((None,) + tuple(a.shape[1:]), lambda b, t: (l,) + (0,) * nd)


def _mod_spec(l, ctx_row, idx):
    row = (lambda b: b) if ctx_row is None else (lambda b: ctx_row)
    return pl.BlockSpec((None, None, None, 1, D_MODEL), lambda b, t: (l, row(b), idx, 0, 0))


def _inproj_body(x_ref, hp_ref, hn_ref, sh_ref, sc_ref, g_ref, w_ref, hcw, hcb, scw, scb,
                 uhg_ref, v_ref, x1_ref, x2_ref, xbc_ref, z_ref, dt_ref, *, nt):
    t = pl.program_id(1)

    def normmod(x):
        ms = jnp.mean(x * x, axis=-1, keepdims=True)
        y = x * lax.rsqrt(ms + EPS) * g_ref[...]
        return (y * (1.0 + sc_ref[...]) + sh_ref[...]).astype(BF16)

    yb = normmod(x_ref[0])
    rows = yb.shape[0]
    ye = jnp.concatenate([yb, normmod(jnp.concatenate([hp_ref[0], hn_ref[0]], axis=0))], axis=0)
    for j in range(0, HG_COLS, 512):
        w = min(512, HG_COLS - j)
        uhg_ref[0, :, j:j + w] = _dot(yb, w_ref[:, j:j + w])
    z_ref[0] = _dot(yb, w_ref[:, COL_Z:COL_Z + SSD_WIDTH])
    dt_ref[0] = _dot(yb, w_ref[:, COL_DT:])

    ri = lax.broadcasted_iota(jnp.int32, (rows, CONV_CHUNK), 0)

    def conv_chunk(col0, cw_ref, cb_ref, k):
        cols = slice(col0 + k * CONV_CHUNK, col0 + (k + 1) * CONV_CHUNK)
        ccols = slice(k * CONV_CHUNK, (k + 1) * CONV_CHUNK)
        ue = _dot(ye, w_ref[:, cols])
        u = ue[:rows]
        prev_row = jnp.where(t > 0, ue[rows + 7:rows + 8, :], 0.0)
        next_row = jnp.where(t < nt - 1, ue[rows + 8:rows + 9, :], 0.0)
        dn = jnp.where(ri == 0, prev_row, pltpu.roll(u, 1, 0))
        up = jnp.where(ri == rows - 1, next_row, pltpu.roll(u, rows - 1, 0))
        return dn * cw_ref[0:1, ccols] + u * cw_ref[1:2, ccols] + up * cw_ref[2:3, ccols] + cb_ref[:, ccols]

    for k, ref in enumerate((v_ref, x1_ref, x2_ref)):
        cv = conv_chunk(COL_HY, hcw, hcb, k)
        for hf in range(HY_HALVES):
            ref[hf, 0] = cv[:, hf * LANES:(hf + 1) * LANES]
    for k in range(SSD_XBC // CONV_CHUNK):
        cv = conv_chunk(COL_XBC, scw, scb, k)
        xbc_ref[0, :, k * CONV_CHUNK:(k + 1) * CONV_CHUNK] = _silu(cv).astype(xbc_ref.dtype)


def _inproj(x, mods, l, ctx_row, g, w, hcw, hcb, scw, scb, tm, name):
    bsz, seq, d = x.shape
    nt = seq // tm
    hb = tm // 8
    tok = lambda n: pl.BlockSpec((1, tm, n), lambda b, t: (b, t, 0))
    hy_spec = pl.BlockSpec((HY_HALVES, 1, tm, LANES), lambda b, t: (0, b, t, 0))
    hy_shape = jax.ShapeDtypeStruct((HY_HALVES, bsz, seq, LANES), F32)
    tshape = lambda n: jax.ShapeDtypeStruct((bsz, seq, n), F32)
    return pl.pallas_call(
        functools.partial(_inproj_body, nt=nt),
        grid=(bsz, nt),
        in_specs=[
            tok(d),
            pl.BlockSpec((1, 8, d), lambda b, t: (b, jnp.maximum(t * hb - 1, 0), 0)),
            pl.BlockSpec((1, 8, d), lambda b, t: (b, jnp.minimum((t + 1) * hb, nt * hb - 1), 0)),
            _mod_spec(l, ctx_row, MOD_SHIFT1), _mod_spec(l, ctx_row, MOD_SCALE1),
            _layer_spec(g, l), _layer_spec(w, l),
            _layer_spec(hcw, l), _layer_spec(hcb, l), _layer_spec(scw, l), _layer_spec(scb, l),
        ],
        out_specs=[tok(HG_COLS), hy_spec, hy_spec, hy_spec, tok(SSD_XBC), tok(SSD_WIDTH), tok(LANES)],
        out_shape=[tshape(HG_COLS), hy_shape, hy_shape, hy_shape,
                   jax.ShapeDtypeStruct((bsz, seq, SSD_XBC), BF16), tshape(SSD_WIDTH), tshape(LANES)],
        compiler_params=_cparams(("parallel", "arbitrary"), big=True),
        name=name,
    )(x, x, x, mods, mods, g, w, hcw, hcb, scw, scb)


def _gla_dir(q, a, v, p, r1, st_ref, reverse):
    tb, ch = TOKEN_BLOCK, HG_CHUNK
    nch = tb // ch
    qv = r1 + _log_sigmoid(a)
    logf = jnp.maximum(p, qv) + jnp.log(1.0 + jnp.exp(-jnp.abs(p - qv)))
    k = 1.0 - jnp.exp(logf)

    ri = lax.broadcasted_iota(jnp.int32, (tb, tb), 0)
    ci = lax.broadcasted_iota(jnp.int32, (tb, tb), 1)
    same = (ri // ch) == (ci // ch)
    tri = (ci >= ri) if reverse else (ci <= ri)
    mask = jnp.logical_and(same, tri)
    mask_b = jnp.where(mask, 1.0, 0.0).astype(BF16)
    bdiag = (ri // HG_DK) == (ci // HG_DK)

    lh, lm, ll = _split3(logf)
    bcs3 = _dot(mask_b, jnp.concatenate([lh, lm, ll], axis=1))
    bcs = bcs3[:, :tb] + bcs3[:, tb:2 * tb] + bcs3[:, 2 * tb:]
    b3 = bcs.reshape(nch, ch, tb)
    mid = ch // 2 if reverse else ch // 2 - 1
    end = 0 if reverse else ch - 1
    b_mid = b3[:, mid:mid + 1, :]
    b_end = b3[:, end:end + 1, :]
    q3 = q.reshape(nch, ch, tb)
    k3 = k.reshape(nch, ch, tb)
    d1 = b3 - b_mid
    qd = (q3 * jnp.exp(d1)).reshape(tb, tb)
    kd = (k3 * jnp.exp(-d1)).reshape(tb, tb).astype(BF16)
    kup = (k3 * jnp.exp(b_end - b3)).astype(BF16)
    qb = (q3 * jnp.exp(b3)).astype(BF16)
    dec = jnp.exp(b_end)

    lane_head = lax.broadcasted_iota(jnp.int32, (1, tb), 1) // HG_DK
    acc = jnp.zeros((tb, tb), F32)
    for h in range(HG_HEADS):
        sel = lane_head == h
        qh = jnp.where(sel, qd, 0.0).astype(BF16)
        s = _dot_nt(qh, kd)
        s = jnp.where(mask, s, 0.0).astype(BF16)
        vh = jnp.where(sel, v, 0.0).astype(BF16)
        acc = acc + _dot(s, vh)

    v3 = v.astype(BF16).reshape(nch, ch, tb)
    inter = [None] * nch
    order = range(nch - 1, -1, -1) if reverse else range(nch)
    st = st_ref[...]
    for c in order:
        inter[c] = _dot_nt(qb[c], st.astype(BF16))
        upd = _dot_tn(v3[c], kup[c])
        st = st * dec[c] + jnp.where(bdiag, upd, 0.0)
    st_ref[...] = st
    return acc + jnp.concatenate(inter, axis=0)


def _gla_body(qf, ff, vf, qb, fb, vb, qc, ffc, fbc, vc, lb_ref, of_x, ob_x, of_c, ob_c, stf, stb, *, layer):
    t = pl.program_id(1)
    is_ctx = t == 0

    @pl.when(is_ctx)
    def _():
        stf[...] = jnp.zeros_like(stf)
        stb[...] = jnp.zeros_like(stb)

    lg = lb_ref[...]
    e = jnp.exp(lg - jnp.max(lg, axis=0, keepdims=True))
    den = jnp.sum(e, axis=0, keepdims=True)
    num = jnp.zeros_like(den)
    for r in range(1, layer + 1):
        num = num + e[r:r + 1, :]
    lb = num / den
    p = jnp.log(lb)
    r1 = jnp.log(1.0 - lb)

    scale = HG_DK ** -0.5
    o_f, o_b = [], []
    for b in range(qf.shape[0]):
        o_f.append(_gla_dir(jnp.where(is_ctx, qc[b], qf[b]) * scale, jnp.where(is_ctx, ffc[b], ff[b]),
                            jnp.where(is_ctx, vc[b], vf[b]), p, r1, stf.at[b], False))
        o_b.append(_gla_dir(jnp.where(is_ctx, qc[b], qb[b]) * scale, jnp.where(is_ctx, fbc[b], fb[b]),
                            jnp.where(is_ctx, vc[b], vb[b]), p, r1, stb.at[b], True))

    @pl.when(is_ctx)
    def _():
        for b in range(len(o_f)):
            of_c[b] = o_f[b].astype(of_c.dtype)
            ob_c[b] = o_b[b].astype(ob_c.dtype)

    @pl.when(jnp.logical_not(is_ctx))
    def _():
        for b in range(len(o_f)):
            of_x[b] = o_f[b].astype(of_x.dtype)
            ob_x[b] = o_b[b].astype(ob_x.dtype)


def _scan_block_maps(nxb):
    fwd = lambda t: jnp.maximum(t - 1, 0)
    bwd = lambda t: jnp.where(t == 0, nxb - 1, nxb - t)
    return fwd, bwd


def _gla(u_x, u_c, lb_logits, layer):
    bsz, seq, _ = u_x.shape
    tb = TOKEN_BLOCK
    nxb = seq // tb
    xf, xb = _scan_block_maps(nxb)

    def xs(blk, col):
        return pl.BlockSpec((1, tb, HG_WIDTH), lambda b, t: (b, blk(t), col))

    def cs(col):
        return pl.BlockSpec((1, tb, HG_WIDTH), lambda b, t: (b, 0, col))

    out_x = jax.ShapeDtypeStruct((bsz, seq, HG_WIDTH), BF16)
    out_c = jax.ShapeDtypeStruct((bsz, CTX_LEN, HG_WIDTH), BF16)
    state = pltpu.VMEM((1, HG_WIDTH, HG_WIDTH), F32)
    return pl.pallas_call(
        functools.partial(_gla_body, layer=layer),
        grid=(bsz, nxb + 1),
        in_specs=[xs(xf, 0), xs(xf, 1), xs(xf, 3), xs(xb, 0), xs(xb, 2), xs(xb, 3),
                  cs(0), cs(1), cs(2), cs(3),
                  pl.BlockSpec((DEPTH, HG_WIDTH), lambda b, t: (0, 0))],
        out_specs=[xs(xf, 0), xs(xb, 0), cs(0), cs(0)],
        out_shape=[out_x, out_x, out_c, out_c],
        scratch_shapes=[state, state],
        compiler_params=_cparams(("parallel", "arbitrary"), big=True),
        name="hgrn2_scan",
    )(u_x, u_x, u_x, u_x, u_x, u_x, u_c, u_c, u_c, u_c, lb_logits)


def _ssd_dir(xbc, dt_raw, dtt_all, dtb_e, alog_e, dtb_t, alog_t, dsk, st_ref, reverse):
    tb, ch = TOKEN_BLOCK, SSD_CHUNK
    nch = tb // ch
    d = 1 if reverse else 0
    xs = xbc[:, :SSD_WIDTH].astype(F32)
    bm = xbc[:, SSD_WIDTH:SSD_WIDTH + 256].astype(BF16)
    cm = xbc[:, SSD_WIDTH + 256:].astype(BF16)

    ri = lax.broadcasted_iota(jnp.int32, (tb, tb), 0)
    ci = lax.broadcasted_iota(jnp.int32, (tb, tb), 1)
    same = (ri // ch) == (ci // ch)
    mask = jnp.logical_and(same, (ci >= ri) if reverse else (ci <= ri))
    mask_b = jnp.where(mask, 1.0, 0.0).astype(BF16)
    mask_t = jnp.logical_and(same, (ri >= ci) if reverse else (ri <= ci))
    mask_tb = jnp.where(mask_t, 1.0, 0.0).astype(BF16)

    ej = lax.broadcasted_iota(jnp.int32, (128, SSD_WIDTH), 0)
    el = lax.broadcasted_iota(jnp.int32, (128, SSD_WIDTH), 1)
    expand = jnp.where(ej == DT_LANE0 + d * SSD_HEADS + el // SSD_HEAD_DIM, 1.0, 0.0).astype(BF16)
    dh, dm, dl = _split3(dt_raw)
    dte_raw = _dot(jnp.concatenate([dh, dm, dl], axis=1), jnp.concatenate([expand] * 3, axis=0))
    dte = _softplus(dte_raw + dtb_e[d:d + 1, :])
    a_e = -jnp.exp(alog_e[d:d + 1, :])
    ah, am, al = _split3(dte * a_e)
    acs3p = _dot(mask_b, jnp.concatenate([ah, am, al], axis=1))
    acs = acs3p[:, :SSD_WIDTH] + acs3p[:, SSD_WIDTH:2 * SSD_WIDTH] + acs3p[:, 2 * SSD_WIDTH:]
    xdt = xs * dte
    acs3 = acs.reshape(nch, ch, SSD_WIDTH)
    end = 0 if reverse else ch - 1
    a_end = acs3[:, end:end + 1, :]
    xw = (xdt.reshape(nch, ch, SSD_WIDTH) * jnp.exp(a_end - acs3)).astype(BF16).reshape(tb, SSD_WIDTH)
    ea = jnp.exp(acs)
    dec = jnp.exp(a_end)

    dtt_raw = dtt_all[d * SSD_HEADS:(d + 1) * SSD_HEADS, :]
    dtt = _softplus(dtt_raw + dtb_t[d * SSD_HEADS:(d + 1) * SSD_HEADS, :])
    a_t = -jnp.exp(alog_t[d * SSD_HEADS:(d + 1) * SSD_HEADS, :])
    th, tm_, tl = _split3(dtt * a_t)
    acs_t3 = _dot(jnp.concatenate([th, tm_, tl], axis=0), mask_tb)
    acs_t = acs_t3[:SSD_HEADS] + acs_t3[SSD_HEADS:2 * SSD_HEADS] + acs_t3[2 * SSD_HEADS:]

    def chunk_diag(m):
        z = jnp.zeros((ch, m.shape[1]), m.dtype)
        return jnp.concatenate(
            [jnp.concatenate([m[c * ch:(c + 1) * ch] if k == c else z for k in range(nch)], axis=1)
             for c in range(nch)], axis=0)

    xdt_b = xdt.astype(BF16)
    hd = SSD_HEAD_DIM
    hpg = SSD_HEADS // SSD_GROUPS
    gw = SSD_WIDTH // SSD_GROUPS
    rr = lax.broadcasted_iota(jnp.int32, (tb, gw), 0)
    ll = lax.broadcasted_iota(jnp.int32, (tb, gw), 1)
    tri4 = ((ll % hd) >= (rr % ch)) if reverse else ((ll % hd) <= (rr % ch))
    head_diag = (rr // hd) == (ll // hd)
    order = range(nch - 1, -1, -1) if reverse else range(nch)
    outs = []
    for g in range(SSD_GROUPS):
        st_cols = slice(g * SSD_STATE, (g + 1) * SSD_STATE)
        cols = slice(g * gw, (g + 1) * gw)
        cm_d = chunk_diag(cm[:, st_cols])
        bm_cat = jnp.concatenate([bm[c * ch:(c + 1) * ch, st_cols] for c in range(nch)], axis=1)
        cb4 = _dot_nt(cm_d, jnp.concatenate([bm_cat] * hpg, axis=0))
        rowm = jnp.concatenate(
            [jnp.broadcast_to(jnp.concatenate([acs_t[g * hpg + h:g * hpg + h + 1, c * ch:(c + 1) * ch]
                                               for h in range(hpg)], axis=1), (ch, gw))
             for c in range(nch)], axis=0)
        w4 = (cb4 * jnp.exp(jnp.where(tri4, acs[:, cols] - rowm, -1e30))).astype(BF16)
        xg = xdt_b[:, cols]
        zero = jnp.zeros((hpg * ch, gw), BF16)
        x4 = jnp.concatenate(
            [jnp.where(head_diag, jnp.concatenate([xg[c * ch:(c + 1) * ch]] * hpg, axis=0), zero)
             for c in range(nch)], axis=0)
        y_intra = _dot(chunk_diag(w4), x4)

        upd = _dot_tn(bm[:, st_cols], chunk_diag(xw[:, cols]))
        st = st_ref[g]
        entering = [None] * nch
        for c in order:
            entering[c] = st.astype(BF16)
            st = st * dec[c][:, cols] + upd[:, c * gw:(c + 1) * gw]
        st_ref[g] = st
        y_inter = _dot(cm_d, jnp.concatenate(entering, axis=0)) * ea[:, cols]
        outs.append(y_intra + y_inter)
    o = jnp.concatenate(outs, axis=1)
    if not reverse:
        o = o + dsk[...] * xs
    return o


def _ssd_body(xf, dtf, dttf, xb, dtb, dttb, xc, dtc, dttc, dtb_e, alog_e, dtb_t, alog_t, dsk,
              of_x, ob_x, of_c, ob_c, stf, stb):
    t = pl.program_id(1)
    is_ctx = t == 0

    @pl.when(is_ctx)
    def _():
        stf[...] = jnp.zeros_like(stf)
        stb[...] = jnp.zeros_like(stb)

    prm = (dtb_e, alog_e, dtb_t, alog_t, dsk)
    o_f = _ssd_dir(jnp.where(is_ctx, xc[0], xf[0]), jnp.where(is_ctx, dtc[0], dtf[0]),
                   jnp.where(is_ctx, dttc[0], dttf[0]), *prm, stf, False)
    o_b = _ssd_dir(jnp.where(is_ctx, xc[0], xb[0]), jnp.where(is_ctx, dtc[0], dtb[0]),
                   jnp.where(is_ctx, dttc[0], dttb[0]), *prm, stb, True)

    @pl.when(is_ctx)
    def _():
        of_c[0] = o_f.astype(of_c.dtype)
        ob_c[0] = o_b.astype(ob_c.dtype)

    @pl.when(jnp.logical_not(is_ctx))
    def _():
        of_x[0] = o_f.astype(of_x.dtype)
        ob_x[0] = o_b.astype(ob_x.dtype)


def _ssd(xbc_x, xbc_c, dt_x, dt_c, dtt_x, dtt_c, prm, l):
    bsz, seq, _ = xbc_x.shape
    tb = TOKEN_BLOCK
    nxb = seq // tb
    xf, xb = _scan_block_maps(nxb)
    full = lambda a: _layer_spec(a, l)
    params = [prm["dtb_e"], prm["alog_e"], prm["dtb_t"], prm["alog_t"], prm["dsk"]]

    def xspecs(blk):
        return [pl.BlockSpec((1, tb, SSD_XBC), lambda b, t: (b, blk(t), 0)),
                pl.BlockSpec((1, tb, 128), lambda b, t: (b, blk(t), 0)),
                pl.BlockSpec((1, 16, tb), lambda b, t: (b, 0, blk(t)))]

    zero = lambda t: 0
    yspec = lambda blk: pl.BlockSpec((1, tb, SSD_WIDTH), lambda b, t: (b, blk(t), 0))
    out_x = jax.ShapeDtypeStruct((bsz, seq, SSD_WIDTH), BF16)
    out_c = jax.ShapeDtypeStruct((bsz, CTX_LEN, SSD_WIDTH), BF16)
    state = pltpu.VMEM((SSD_GROUPS, SSD_STATE, SSD_WIDTH // SSD_GROUPS), F32)
    return pl.pallas_call(
        _ssd_body,
        grid=(bsz, nxb + 1),
        in_specs=xspecs(xf) + xspecs(xb) + xspecs(zero) + [full(a) for a in params],
        out_specs=[yspec(xf), yspec(xb), yspec(zero), yspec(zero)],
        out_shape=[out_x, out_x, out_c, out_c],
        scratch_shapes=[state, state],
        compiler_params=_cparams(("parallel", "arbitrary"), big=True),
        name="ssd_scan",
    )(xbc_x, dt_x, dtt_x, xbc_x, dt_x, dtt_x, xbc_c, dt_c, dtt_c, *params)


def _filt_body(w1, b1, f1, w2, b2, f2, w3, delta, buf_ref, sa_ref, *, seq_len, tr):
    i = pl.program_id(0)
    hr = tr // 2
    lane = lax.broadcasted_iota(jnp.int32, (hr, 128), 1)
    grp, sub = lane // HY_HIDDEN, lane % HY_HIDDEN
    jrow = i * tr + grp * hr + lax.broadcasted_iota(jnp.int32, (hr, 128), 0)
    pos = jnp.where(jrow < seq_len, jrow, 2 * seq_len - jrow).astype(F32)
    tpos = pos / (seq_len - 1.0)
    wpos = pos * (2.0 * math.pi / seq_len)
    bidx = jnp.where(sub <= HY_EMB_BANDS, sub - 1, sub - 1 - HY_EMB_BANDS).astype(F32)
    band = 1e-4 + bidx * ((HY_EMB_BANDS - 1 - 1e-4) / (HY_EMB_BANDS - 1))
    ang = band * wpos + jnp.where(sub > HY_EMB_BANDS, 0.5 * math.pi, 0.0)
    z = jnp.where(sub == 0, tpos, jnp.where(sub <= 2 * HY_EMB_BANDS, jnp.cos(ang), 0.0))
    h = jnp.sin(f1[...] * (_dot3(z, w1[...]) + b1[...]))
    h = jnp.sin(f2[...] * (_dot3(h, w2[...]) + b2[...]))

    @pl.when(i == 0)
    def _():
        sa_ref[...] = jnp.zeros_like(sa_ref)

    for g in range(2):
        o = _dot3(h, w3[0, g])
        o = o * jnp.exp(-tpos[:, g * HY_HIDDEN:g * HY_HIDDEN + 1] * delta[...])
        o = jnp.where(jrow[:, g * HY_HIDDEN:g * HY_HIDDEN + 1] == seq_len, 0.0, o)
        for order in range(2):
            for hf in range(HY_HALVES):
                lo = order * HY_WIDTH + hf * LANES
                buf_ref[order, hf, g * hr:(g + 1) * hr, :] = o[:, lo:lo + LANES]
        sa_ref[...] = sa_ref[...] + jnp.sum(jnp.abs(o), axis=0, keepdims=True)


def _filt_gen(p, seq_len, tr):
    n = 2 * seq_len
    assert seq_len % tr == 0
    half = seq_len // tr
    full = lambda a: pl.BlockSpec(a.shape, lambda i: (0,) * a.ndim)
    small = [p["w1"], p["b1"], p["f1"], p["w2"], p["b2"], p["f2"]]
    return pl.pallas_call(
        functools.partial(_filt_body, seq_len=seq_len, tr=tr),
        grid=(n // tr,),
        in_specs=[full(a) for a in small] + [
            pl.BlockSpec((1, 2, 2 * HY_HIDDEN, 2 * HY_WIDTH), lambda i: (i // half, 0, 0, 0)),
            full(p["delta"]),
        ],
        out_specs=[pl.BlockSpec((2, HY_HALVES, tr, LANES), lambda i: (0, 0, i, 0)),
                   pl.BlockSpec((8, 2 * HY_WIDTH), lambda i: (0, 0))],
        out_shape=[jax.ShapeDtypeStruct((2, HY_HALVES, n, LANES), F32),
                   jax.ShapeDtypeStruct((8, 2 * HY_WIDTH), F32)],
        compiler_params=_cparams(("arbitrary",)),
        name="hyena_filter",
    )(*small, p["w3"], p["delta"])


FFT_NB = 8


def _strided_rows(ref3, j, n):
    flat = ref3.reshape(ref3.shape[0] * FFT_NB, LANES)
    return flat[pl.ds(j, n, stride=FFT_NB), :]


def _store_strided_rows(ref3, j, val):
    flat = ref3.reshape(ref3.shape[0] * FFT_NB, LANES)
    flat[pl.ds(j, val.shape[0], stride=FFT_NB), :] = val


def _s1_body(x_ref, f_ref, t_ref, o_ref):
    rows = x_ref.shape[2]
    for j in range(FFT_NB):
        x = jnp.concatenate([_strided_rows(x_ref.at[0, hf], j, rows) for hf in range(HY_HALVES)], axis=1)
        a = _dot3_pre((f_ref[0], f_ref[1]), x)
        tr = t_ref[0, 0, :, j:j + 1]
        ti = t_ref[0, 1, :, j:j + 1]
        ar, ai = a[:FFT_N1], a[FFT_N1:]
        out = jnp.concatenate([ar * tr - ai * ti, ar * ti + ai * tr], axis=0)
        for hf in range(HY_HALVES):
            _store_strided_rows(o_ref.at[0, hf], j, out[:, hf * LANES:(hf + 1) * LANES])


def _fft_stage1(x, fmat, twid):
    g, _, rows, n2, _ = x.shape
    return pl.pallas_call(
        _s1_body,
        grid=(g, n2 // FFT_NB),
        in_specs=[
            pl.BlockSpec((1, HY_HALVES, rows, FFT_NB, LANES), lambda i, j: (i, 0, 0, j, 0)),
            pl.BlockSpec(fmat.shape, lambda i, j: (0, 0, 0)),
            pl.BlockSpec((1, 2, FFT_N1, FFT_NB), lambda i, j: (j, 0, 0, 0)),
        ],
        out_specs=pl.BlockSpec((1, HY_HALVES, 2 * FFT_N1, FFT_NB, LANES), lambda i, j: (i, 0, 0, j, 0)),
        out_shape=jax.ShapeDtypeStruct((g, HY_HALVES, 2 * FFT_N1, n2, LANES), F32),
        compiler_params=_cparams(("parallel", "arbitrary"), big=True),
        name="fft_stage1",
    )(x, fmat, twid)


def _load_complex(ref, k):
    return jnp.concatenate(
        [jnp.concatenate([ref[hf, 0, k], ref[hf, 1, k]], axis=0) for hf in range(HY_HALVES)], axis=1)


def _store_complex(ref, k, val):
    for hf in range(HY_HALVES):
        ref[hf, 0, k] = val[:FFT_N1, hf * LANES:(hf + 1) * LANES].astype(ref.dtype)
        ref[hf, 1, k] = val[FFT_N1:, hf * LANES:(hf + 1) * LANES].astype(ref.dtype)


def _s2_body(a_ref, h_ref, f_ref, fc_ref, o_ref, *, kb):
    for k in range(kb):
        x = _dot3_pre((f_ref[0], f_ref[1]), _load_complex(a_ref, k))
        xr, xi = x[:FFT_N1], x[FFT_N1:]
        h = _load_complex(h_ref.at[0], k).astype(F32)
        hr, hi = h[:FFT_N1], h[FFT_N1:]
        y = jnp.concatenate([xr * hr - xi * hi, xr * hi + xi * hr], axis=0)
        _store_complex(o_ref, k, _dot3_pre((fc_ref[0], fc_ref[1]), y))


def _fft_stage2(a, hspec, order, fmat, fmat_c, kb=8):
    blk = pl.BlockSpec((HY_HALVES, 2, kb, FFT_N1, LANES), lambda i: (0, 0, i, 0, 0))
    return pl.pallas_call(
        functools.partial(_s2_body, kb=kb),
        grid=(FFT_N1 // kb,),
        in_specs=[blk,
                  pl.BlockSpec((1, HY_HALVES, 2, kb, FFT_N1, LANES), lambda i: (order, 0, 0, i, 0, 0)),
                  pl.BlockSpec(fmat.shape, lambda i: (0, 0, 0)),
                  pl.BlockSpec(fmat_c.shape, lambda i: (0, 0, 0))],
        out_specs=blk,
        out_shape=jax.ShapeDtypeStruct(a.shape, F32),
        compiler_params=_cparams(("arbitrary",), big=True),
        name="fft_stage2_mul",
    )(a, hspec, fmat, fmat_c)


def _s2f_body(a_ref, sa_ref, f_ref, o_ref, *, kb, n):
    o = pl.program_id(0)
    sa = jnp.where(o == 0, sa_ref[0:1, :HY_WIDTH], sa_ref[0:1, HY_WIDTH:])
    scale = 1.0 / (sa * float(n))
    for k in range(kb):
        x = _dot3_pre((f_ref[0], f_ref[1]), _load_complex(a_ref.at[0], k))
        _store_complex(o_ref.at[0], k, x * scale)


def _fft_stage2_filter(a, sumabs, fmat, n, kb=8):
    blk = pl.BlockSpec((1, HY_HALVES, 2, kb, FFT_N1, LANES), lambda o, i: (o, 0, 0, i, 0, 0))
    return pl.pallas_call(
        functools.partial(_s2f_body, kb=kb, n=n),
        grid=(2, FFT_N1 // kb),
        in_specs=[blk, pl.BlockSpec(sumabs.shape, lambda o, i: (0, 0)),
                  pl.BlockSpec(fmat.shape, lambda o, i: (0, 0, 0))],
        out_specs=blk,
        out_shape=jax.ShapeDtypeStruct(a.shape, BF16),
        compiler_params=_cparams(("parallel", "arbitrary"), big=True),
        name="fft_stage2_filter",
    )(a, sumabs, fmat)


def _s3_body(c_ref, t_ref, f_ref, gate_ref, v_ref, bias_ref, o_ref):
    rows = gate_ref.shape[1]
    for j in range(FFT_NB):
        tr = t_ref[0, 0, :, j:j + 1]
        ti = t_ref[0, 1, :, j:j + 1]
        c = jnp.concatenate([_strided_rows(c_ref.at[hf], j, 2 * FFT_N1) for hf in range(HY_HALVES)], axis=1)
        cr, ci = c[:FFT_N1], c[FFT_N1:]
        dmat = jnp.concatenate([cr * tr + ci * ti, ci * tr - cr * ti], axis=0)
        w = _dot3_pre((f_ref[0], f_ref[1]), dmat)
        for hf in range(HY_HALVES):
            lanes = slice(hf * LANES, (hf + 1) * LANES)
            g = _strided_rows(gate_ref.at[hf], j, rows)
            v = _strided_rows(v_ref.at[hf], j, rows)
            _store_strided_rows(o_ref.at[hf], j, g * (w[:, lanes] + v * bias_ref[:, lanes]))


def _fft_stage3(c, twid, fmat, gate, v, bias, l, order):
    _, rows, n2, _ = gate.shape
    dspec = pl.BlockSpec((HY_HALVES, rows, FFT_NB, LANES), lambda j: (0, 0, j, 0))
    return pl.pallas_call(
        _s3_body,
        grid=(n2 // FFT_NB,),
        in_specs=[
            pl.BlockSpec((HY_HALVES, 2 * FFT_N1, FFT_NB, LANES), lambda j: (0, 0, j, 0)),
            pl.BlockSpec((1, 2, FFT_N1, FFT_NB), lambda j: (j, 0, 0, 0)),
            pl.BlockSpec(fmat.shape, lambda j: (0, 0, 0)),
            dspec, dspec,
            pl.BlockSpec((None, None, 1, HY_WIDTH), lambda j: (l, order, 0, 0)),
        ],
        out_specs=dspec,
        out_shape=jax.ShapeDtypeStruct(gate.shape, F32),
        compiler_params=_cparams(("arbitrary",), big=True),
        name="fft_stage3_gate",
    )(c, twid, fmat, gate, v, bias)


FFT_TILES = FFT_N1 // FFT_NB
TILE_ROWS = 2 * FFT_N1 * FFT_NB


def _spec_stage1(x_refs, f_ref, t_ref, scr, j, rows):
    for i in range(FFT_NB):
        x = jnp.concatenate([_strided_rows(r, i, rows) for r in x_refs], axis=1)
        a = _dot3_pre((f_ref[0], f_ref[1]), x)
        tr = t_ref[0, 0, :, i:i + 1]
        ti = t_ref[0, 1, :, i:i + 1]
        ar, ai = a[:FFT_N1], a[FFT_N1:]
        out = jnp.concatenate([ar * tr - ai * ti, ar * ti + ai * tr], axis=0)
        for hf in range(HY_HALVES):
            scr[hf, pl.ds(j * TILE_ROWS + i, 2 * FFT_N1, stride=FFT_NB), :] = out[:, hf * LANES:(hf + 1) * LANES]


def _spec_slab(scr, k1):
    def part(hf, p):
        r0 = (p * FFT_N1 + k1) * FFT_NB
        return jnp.concatenate(
            [scr[hf, pl.ds(pl.multiple_of(jj * TILE_ROWS + r0, FFT_NB), FFT_NB), :] for jj in range(FFT_TILES)], axis=0)
    return jnp.concatenate(
        [jnp.concatenate([part(hf, 0), part(hf, 1)], axis=0) for hf in range(HY_HALVES)], axis=1)


def _spec_slab_store(scr, k1, val):
    for hf in range(HY_HALVES):
        for p in range(2):
            r0 = (p * FFT_N1 + k1) * FFT_NB
            for jj in range(FFT_TILES):
                scr[hf, pl.ds(pl.multiple_of(jj * TILE_ROWS + r0, FFT_NB), FFT_NB), :] = val[
                    p * FFT_N1 + jj * FFT_NB:p * FFT_N1 + (jj + 1) * FFT_NB, hf * LANES:(hf + 1) * LANES]


def _hconv_body(y_ref, h_ref, gate_ref, f1_ref, f2_ref, f2c_ref, f3_ref, t_ref, bias_ref, o_ref, scr, *, kb):
    s = pl.program_id(0)
    rows = y_ref.shape[1]

    @pl.when(s < FFT_TILES)
    def _():
        _spec_stage1([y_ref.at[hf] for hf in range(HY_HALVES)], f1_ref, t_ref, scr, s, rows)

    @pl.when(jnp.logical_and(s >= FFT_TILES, s < 2 * FFT_TILES))
    def _():
        kt = s - FFT_TILES
        for k in range(kb):
            k1 = kt * kb + k
            x = _dot3_pre((f2_ref[0], f2_ref[1]), _spec_slab(scr, k1))
            xr, xi = x[:FFT_N1], x[FFT_N1:]
            h = _load_complex(h_ref, k).astype(F32)
            hr, hi = h[:FFT_N1], h[FFT_N1:]
            y = jnp.concatenate([xr * hr - xi * hi, xr * hi + xi * hr], axis=0)
            _spec_slab_store(scr, k1, _dot3_pre((f2c_ref[0], f2c_ref[1]), y))

    @pl.when(s >= 2 * FFT_TILES)
    def _():
        j = s - 2 * FFT_TILES
        for i in range(FFT_NB):
            tr = t_ref[0, 0, :, i:i + 1]
            ti = t_ref[0, 1, :, i:i + 1]
            c = jnp.concatenate(
                [scr[hf, pl.ds(j * TILE_ROWS + i, 2 * FFT_N1, stride=FFT_NB), :] for hf in range(HY_HALVES)], axis=1)
            cr, ci = c[:FFT_N1], c[FFT_N1:]
            dmat = jnp.concatenate([cr * tr + ci * ti, ci * tr - cr * ti], axis=0)
            w = _dot3_pre((f3_ref[0], f3_ref[1]), dmat)
            for hf in range(HY_HALVES):
                lanes = slice(hf * LANES, (hf + 1) * LANES)
                g = _strided_rows(gate_ref.at[hf], i, rows)
                v = _strided_rows(y_ref.at[hf], i, rows)
                _store_strided_rows(o_ref.at[hf], i, g * (w[:, lanes] + v * bias_ref[:, lanes]))


def _hyena_conv(y, gate, hspec, bias, l, order, consts, kb=8):
    nh, rows, n2, lanes = y.shape
    nt = FFT_TILES
    tile = lambda f: pl.BlockSpec((nh, rows, FFT_NB, lanes), lambda s: (0, 0, f(s), 0))
    in_tile = lambda s: jnp.where(s < nt, s, jnp.maximum(s - 2 * nt, 0))
    out_tile = lambda s: jnp.maximum(s - 2 * nt, 0)
    const = lambda a: pl.BlockSpec(a.shape, lambda s: (0,) * a.ndim)
    return pl.pallas_call(
        functools.partial(_hconv_body, kb=kb),
        grid=(3 * nt,),
        in_specs=[
            tile(in_tile),
            pl.BlockSpec((None, nh, 2, kb, FFT_N1, lanes),
                         lambda s: (order, 0, 0, jnp.clip(s - nt, 0, FFT_N1 // kb - 1), 0, 0)),
            tile(out_tile),
            const(consts["f1_data"]), const(consts["f2"]), const(consts["f2c"]), const(consts["f3"]),
            pl.BlockSpec((1, 2, FFT_N1, FFT_NB), lambda s: (in_tile(s), 0, 0, 0)),
            pl.BlockSpec((None, None, 1, HY_WIDTH), lambda s: (l, order, 0, 0)),
        ],
        out_specs=tile(out_tile),
        out_shape=jax.ShapeDtypeStruct(y.shape, F32),
        scratch_shapes=[pltpu.VMEM((nh, nt * TILE_ROWS, lanes), F32)],
        compiler_params=_cparams(("arbitrary",), big=True),
        name="hyena_conv",
    )(y, hspec, gate, consts["f1_data"], consts["f2"], consts["f2c"], consts["f3"], consts["twid"], bias)


def _hspec_body(b_ref, sa_ref, f1_ref, f2_ref, t_ref, o_ref, scr, *, kb, n):
    o = pl.program_id(0)
    s = pl.program_id(1)
    rows = b_ref.shape[1]

    @pl.when(s < FFT_TILES)
    def _():
        _spec_stage1([b_ref.at[hf] for hf in range(HY_HALVES)], f1_ref, t_ref, scr, s, rows)

    @pl.when(s >= FFT_TILES)
    def _():
        sa = jnp.where(o == 0, sa_ref[0:1, :HY_WIDTH], sa_ref[0:1, HY_WIDTH:])
        scale = 1.0 / (sa * float(n))
        kt = s - FFT_TILES
        for k in range(kb):
            x = _dot3_pre((f2_ref[0], f2_ref[1]), _spec_slab(scr, kt * kb + k))
            _store_complex(o_ref, k, x * scale)


def _hyena_spectrum(buf, sumabs, consts, n, kb=8):
    _, nh, rows, n2, lanes = buf.shape
    nt = FFT_TILES
    const = lambda a: pl.BlockSpec(a.shape, lambda o, s: (0,) * a.ndim)
    return pl.pallas_call(
        functools.partial(_hspec_body, kb=kb, n=n),
        grid=(2, 2 * nt),
        in_specs=[
            pl.BlockSpec((None, nh, rows, FFT_NB, lanes), lambda o, s: (o, 0, 0, jnp.minimum(s, nt - 1), 0)),
            const(sumabs), const(consts["f1_real"]), const(consts["f2"]),
            pl.BlockSpec((1, 2, FFT_N1, FFT_NB), lambda o, s: (jnp.minimum(s, nt - 1), 0, 0, 0)),
        ],
        out_specs=pl.BlockSpec((None, nh, 2, kb, FFT_N1, lanes), lambda o, s: (o, 0, 0, jnp.maximum(s - nt, 0), 0, 0)),
        out_shape=jax.ShapeDtypeStruct((2, nh, 2, FFT_N1, FFT_N1, lanes), BF16),
        scratch_shapes=[pltpu.VMEM((nh, nt * TILE_ROWS, lanes), F32)],
        compiler_params=_cparams(("arbitrary", "arbitrary"), big=True),
        name="hyena_spectrum",
    )(buf, sumabs, consts["f1_real"], consts["f2"], consts["twid"])


def _hyc_body(v_ref, x1_ref, x2_ref, buf_ref, sa_ref, cs_ref, bias_ref, o_ref):
    n = 2 * CTX_LEN
    cmat, smat = cs_ref[0], cs_ref[1]
    c_in, s_in = cmat[:, :CTX_LEN], smat[:, :CTX_LEN]
    c_out, s_out = cmat[:CTX_LEN, :], smat[:CTX_LEN, :]
    batch = lambda ref, b: jnp.concatenate([ref[hf, b] for hf in range(HY_HALVES)], axis=1)
    yr, yi = batch(v_ref, 0), batch(v_ref, 1)
    for o, gate in enumerate((x1_ref, x2_ref)):
        bufo = jnp.concatenate([buf_ref[o, hf] for hf in range(HY_HALVES)], axis=1)
        scale = 1.0 / (sa_ref[0:1, o * HY_WIDTH:(o + 1) * HY_WIDTH] * float(n))
        hr = _dot3(cmat, bufo) * scale
        hi = -_dot3(smat, bufo) * scale
        xr = _dot3(c_in, yr) + _dot3(s_in, yi)
        xi = _dot3(c_in, yi) - _dot3(s_in, yr)
        zr = xr * hr - xi * hi
        zi = xr * hi + xi * hr
        wr = _dot3(c_out, zr) - _dot3(s_out, zi)
        wi = _dot3(c_out, zi) + _dot3(s_out, zr)
        b = bias_ref[o:o + 1, :]
        yr = batch(gate, 0) * (wr + yr * b)
        yi = batch(gate, 1) * (wi + yi * b)
    for hf in range(HY_HALVES):
        o_ref[hf, 0] = yr[:, hf * LANES:(hf + 1) * LANES]
        o_ref[hf, 1] = yi[:, hf * LANES:(hf + 1) * LANES]


def _hy_ctx(v, x1, x2, buf, sumabs, cs, bias):
    return pl.pallas_call(
        _hyc_body,
        out_shape=jax.ShapeDtypeStruct(v.shape, F32),
        compiler_params=pltpu.CompilerParams(vmem_limit_bytes=VMEM_LIMIT),
        name="hyena_ctx",
    )(v, x1, x2, buf, sumabs, cs, bias)


def _dft_constants(seq_len):
    n = 2 * seq_len
    n1 = FFT_N1
    assert n == n1 * n1
    idx = np.arange(n1, dtype=np.float64)
    th = 2.0 * np.pi * np.outer(idx, idx) / n1
    fr, fi = np.cos(th), -np.sin(th)
    half = n1 // 2

    def parts(m):
        m32 = jnp.asarray(m, F32)
        hi = m32.astype(BF16)
        mid = (m32 - hi.astype(F32)).astype(BF16)
        return jnp.stack([hi, mid])

    f1_data = np.block([[fr[:, :half], -fi[:, :half]], [fi[:, :half], fr[:, :half]]])
    f1_real = np.concatenate([fr, fi], axis=0)
    f2 = np.block([[fr, -fi], [fi, fr]])
    f2c = np.block([[fr, fi], [-fi, fr]])
    f3 = np.block([[fr[:half], fi[:half]], [-fi[:half], fr[:half]]])
    tw = 2.0 * np.pi * np.outer(idx, idx) / n
    twid = np.stack([np.cos(tw), -np.sin(tw)])
    twid = twid.reshape(2, n1, n1 // FFT_NB, FFT_NB).transpose(2, 0, 1, 3)
    return dict(f1_data=parts(f1_data), f1_real=parts(f1_real), f2=parts(f2), f2c=parts(f2c),
                f3=parts(f3), twid=jnp.asarray(twid, F32))


def _ctx_dft_constants():
    n = 2 * CTX_LEN
    idx = np.arange(n, dtype=np.float64)
    th = 2.0 * np.pi * np.outer(idx, idx) / n
    return jnp.asarray(np.stack([np.cos(th), np.sin(th)]), F32)


def _hyena_filter_params(hy_w1, hy_b1, hy_freq1, hy_w2, hy_b2, hy_freq2, hy_w3):
    hid = HY_HIDDEN
    zeros = jnp.zeros((hid, hid), F32)
    w1 = jnp.zeros((hid, hid), F32).at[:hy_w1.shape[0]].set(hy_w1)
    w1 = jnp.block([[w1, zeros], [zeros, w1]])
    w2 = jnp.block([[hy_w2, zeros], [zeros, hy_w2]])
    w3 = hy_w3.reshape(hid, 2, 2 * HY_WIDTH).transpose(1, 0, 2)
    z3 = jnp.zeros_like(w3)
    w3 = jnp.stack([jnp.concatenate([w3, z3], axis=1), jnp.concatenate([z3, w3], axis=1)], axis=1)
    deltas = np.abs(np.linspace(HY_MIN_DECAY, HY_MAX_DECAY, HY_WIDTH))
    delta = jnp.asarray(np.tile(deltas, 2)[None, :], F32)
    row = lambda a: jnp.tile(a, 2).reshape(1, -1)
    return dict(w1=w1, b1=row(hy_b1), f1=row(hy_freq1), w2=w2, b2=row(hy_b2), f2=row(hy_freq2),
                w3=w3, delta=delta)


def _hyena_x(v, x1, x2, hspec, bias, l, consts):
    nh, bsz, seq, lanes = v.shape
    assert bsz == 2
    rows = bsz * (seq // FFT_N1)
    view = lambda a: a.reshape(nh, rows, FFT_N1, lanes)
    y = view(v)
    for o, gate in enumerate((x1, x2)):
        y = _hyena_conv(y, view(gate), hspec, bias, l, o, consts)
    return y.reshape(nh, bsz, seq, lanes)


def _out_body(of, ob, ug, hy, yf, yb, zz, hgn, ssn, w_ref, h_ref, gate_ref, o_ref):
    o = of[0].astype(F32) + ob[0].astype(F32)
    gi = lax.broadcasted_iota(jnp.int32, (HG_WIDTH, HG_WIDTH), 0) // HG_DK
    gj = lax.broadcasted_iota(jnp.int32, (HG_WIDTH, HG_WIDTH), 1) // HG_DK
    avg = jnp.where(gi == gj, 1.0 / HG_DK, 0.0).astype(BF16)
    sq = o * o
    sh = sq.astype(BF16)
    sl = (sq - sh.astype(F32)).astype(BF16)
    ms = _dot(sh, avg) + _dot(sl, avg)
    hg = o * lax.rsqrt(ms + EPS) * hgn[...] * _silu(ug[0])
    y = (yf[0].astype(F32) + yb[0].astype(F32)) * _silu(zz[0])
    gw = SSD_WIDTH // SSD_GROUPS
    parts = []
    for g in range(SSD_GROUPS):
        yg = y[:, g * gw:(g + 1) * gw]
        parts.append(yg * lax.rsqrt(jnp.mean(yg * yg, axis=-1, keepdims=True) + EPS))
    ys = jnp.concatenate(parts, axis=1) * ssn[...]
    acc = _dot(hg.astype(BF16), w_ref[0:HG_WIDTH, :])
    hyv = jnp.concatenate([hy[hf, 0] for hf in range(HY_HALVES)], axis=1)
    acc = acc + _dot(hyv.astype(BF16), w_ref[HG_WIDTH:HG_WIDTH + HY_WIDTH, :])
    acc = acc + _dot(ys.astype(BF16), w_ref[HG_WIDTH + HY_WIDTH:, :])
    o_ref[0] = h_ref[0] + gate_ref[...] * acc


def _out_proj(of, ob, u_hg, hy, yf, yb, zgate, hgn, ssn, w, h, mods, l, ctx_row, tm):
    bsz, seq, d = h.shape
    tok = lambda n, col=0: pl.BlockSpec((1, tm, n), lambda b, t: (b, t, col))
    return pl.pallas_call(
        _out_body,
        grid=(bsz, seq // tm),
        in_specs=[tok(HG_WIDTH), tok(HG_WIDTH), tok(HG_WIDTH, 4),
                  pl.BlockSpec((HY_HALVES, 1, tm, LANES), lambda b, t: (0, b, t, 0)),
                  tok(SSD_WIDTH), tok(SSD_WIDTH), tok(SSD_WIDTH),
                  _layer_spec(hgn, l), _layer_spec(ssn, l), _layer_spec(w, l),
                  tok(d),
                  _mod_spec(l, ctx_row, MOD_GATE1)],
        out_specs=tok(d),
        out_shape=jax.ShapeDtypeStruct(h.shape, F32),
        compiler_params=_cparams(("parallel", "arbitrary"), big=True),
        name="mixer_out_proj",
    )(of, ob, u_hg, hy, yf, yb, zgate, hgn, ssn, w, h, mods)


def _ffn_body(*refs, tm, grid_w, vertical, final, cchunk):
    if vertical:
        x_ref, hp, hn, sh_ref, sc_ref, g_ref, wg, wu, cw, cb, wd, gate_ref = refs[:12]
        rest = refs[12:]
    else:
        x_ref, sh_ref, sc_ref, g_ref, wg, wu, cw, cb, wd, gate_ref = refs[:10]
        rest = refs[10:]
    if final:
        fg_ref, o_ref, act_ref = rest
    else:
        o_ref, act_ref = rest
    t = pl.program_id(1)
    nt = pl.num_programs(1)

    def normmod(v):
        ms = jnp.mean(v * v, axis=-1, keepdims=True)
        y = v * lax.rsqrt(ms + EPS) * g_ref[...]
        return (y * (1.0 + sc_ref[...]) + sh_ref[...]).astype(BF16)

    x = x_ref[0]
    yb = normmod(x)
    pad = grid_w if vertical else 0
    ext_rows = tm + 2 * pad
    ye = jnp.concatenate([normmod(hp[0]), yb, normmod(hn[0])], axis=0) if vertical else yb
    col = lax.broadcasted_iota(jnp.int32, (ext_rows, cchunk), 0) % grid_w
    for j in range(0, D_FF, cchunk):
        cs = slice(j, j + cchunk)
        ext = _dot(ye, wg[:, cs])
        if vertical:
            ext = jnp.concatenate([jnp.where(t > 0, ext[:pad], 0.0), ext[pad:pad + tm],
                                   jnp.where(t < nt - 1, ext[pad + tm:], 0.0)], axis=0)
        left = jnp.where(col != 0, pltpu.roll(ext, 1, 0), 0.0)
        right = jnp.where(col != grid_w - 1, pltpu.roll(ext, ext_rows - 1, 0), 0.0)
        acc = jnp.zeros((tm, cchunk), F32) + cb[:, cs]
        for dy in ((-1, 0, 1) if vertical else (0,)):
            r0 = pad + dy * grid_w
            ky = dy + 1
            acc = acc + left[r0:r0 + tm] * cw[3 * ky:3 * ky + 1, cs]
            acc = acc + ext[r0:r0 + tm] * cw[3 * ky + 1:3 * ky + 2, cs]
            acc = acc + right[r0:r0 + tm] * cw[3 * ky + 2:3 * ky + 3, cs]
        act_ref[:, cs] = (_silu(acc) * _dot(yb, wu[:, cs])).astype(BF16)
    y = _dot(act_ref[...], wd[...])
    x = x + gate_ref[...] * y
    if final:
        ms = jnp.mean(x * x, axis=-1, keepdims=True)
        x = x * lax.rsqrt(ms + EPS) * fg_ref[...]
    o_ref[0] = x


def _ffn(h, mods, l, ctx_row, g, wg, wu, cw, cb, wd, tm, grid_w, vertical, final_g=None):
    bsz, seq, d = h.shape
    nhb = tm // grid_w
    tok = lambda n: pl.BlockSpec((1, tm, n), lambda b, t: (b, t, 0))
    full = lambda a: _layer_spec(a, l)
    resident = lambda a: pl.BlockSpec((None,) + tuple(a.shape[1:]), lambda b, t: (l, 0, 0),
                                      pipeline_mode=pl.Buffered(1))
    in_specs = [tok(d)]
    args = [h]
    if vertical:
        nrow = seq // grid_w
        in_specs += [
            pl.BlockSpec((1, grid_w, d), lambda b, t: (b, jnp.maximum(t * nhb - 1, 0), 0)),
            pl.BlockSpec((1, grid_w, d), lambda b, t: (b, jnp.minimum((t + 1) * nhb, nrow - 1), 0)),
        ]
        args += [h, h]
    in_specs += [_mod_spec(l, ctx_row, MOD_SHIFT2), _mod_spec(l, ctx_row, MOD_SCALE2), full(g),
                 resident(wg), resident(wu), full(cw), full(cb), resident(wd),
                 _mod_spec(l, ctx_row, MOD_GATE2)]
    args += [mods, mods, g, wg, wu, cw, cb, wd, mods]
    if final_g is not None:
        in_specs.append(pl.BlockSpec((1, d), lambda b, t: (0, 0)))
        args.append(final_g.reshape(1, d))
    return pl.pallas_call(
        functools.partial(_ffn_body, tm=tm, grid_w=grid_w, vertical=vertical,
                          final=final_g is not None, cchunk=256),
        grid=(bsz, seq // tm),
        in_specs=in_specs,
        out_specs=tok(d),
        out_shape=jax.ShapeDtypeStruct(h.shape, F32),
        scratch_shapes=[pltpu.VMEM((tm, D_FF), BF16)],
        compiler_params=_cparams(("parallel", "arbitrary"), big=True),
        name="ffn",
    )(*args)


def kernel(x, c, ctx, c_ctx, w_ada, b_ada, norm1_g, norm2_g, w_in, w_out, hg_lb_logits, hg_norm_g, hy_conv_w, hy_conv_b, hy_w1, hy_b1, hy_freq1, hy_w2, hy_b2, hy_freq2, hy_w3, hy_bias, ssd_conv_w, ssd_conv_b, ssd_dt_bias, ssd_a_log, ssd_d, ssd_norm_g, ffn_w_gate, ffn_w_up, ffn_conv_w, ffn_conv_b, ffn_w_down, final_norm_g):
    bsz, seq, d = x.shape
    depth = w_in.shape[0]
    cc = jnp.zeros((8, d), F32).at[:bsz].set(c).at[bsz].set(c_ctx)
    mods = _ada(cc, w_ada, b_ada).reshape(depth, 8, 6, 1, d)
    ctx_row = bsz

    consts = _dft_constants(seq)
    cs_ctx = _ctx_dft_constants()
    lb_logits = hg_lb_logits.astype(F32)

    row3 = lambda a: a.reshape(depth, 1, -1)
    rep = lambda a: jnp.repeat(a, SSD_HEAD_DIM, axis=-1)
    w_in_b = jnp.pad(w_in, ((0, 0), (0, 0), (0, IN_COLS_PAD - w_in.shape[-1]))).astype(BF16)
    w_out_b = w_out.astype(BF16)
    wg_b, wu_b, wd_b = ffn_w_gate.astype(BF16), ffn_w_up.astype(BF16), ffn_w_down.astype(BF16)
    g1, g2 = row3(norm1_g), row3(norm2_g)
    hcb, scb = row3(hy_conv_b), row3(ssd_conv_b)
    hgn, ssn = row3(hg_norm_g), row3(ssd_norm_g)
    cw_f, cb_f = ffn_conv_w.reshape(depth, 9, D_FF), row3(ffn_conv_b)
    prm = dict(dtb_e=rep(ssd_dt_bias), alog_e=rep(ssd_a_log),
               dtb_t=ssd_dt_bias.reshape(depth, -1, 1), alog_t=ssd_a_log.reshape(depth, -1, 1),
               dsk=row3(rep(ssd_d)))

    for l in range(depth):
        last = l == depth - 1
        proj = lambda h, row, tm, name: _inproj(h, mods, l, row, g1, w_in_b, hy_conv_w, hcb, ssd_conv_w, scb,
                                                tm, name)
        uhg_x, v_x, x1_x, x2_x, xbc_x, z_x, dt_x = proj(x, None, 512, "in_proj_x")
        uhg_c, v_c, x1_c, x2_c, xbc_c, z_c, dt_c = proj(ctx, ctx_row, CTX_LEN, "in_proj_ctx")

        of_x, ob_x, of_c, ob_c = _gla(uhg_x, uhg_c, lb_logits, l)

        dtt_x = jnp.swapaxes(dt_x[:, :, DT_LANE0:DT_LANE0 + 2 * SSD_HEADS], 1, 2)
        dtt_c = jnp.swapaxes(dt_c[:, :, DT_LANE0:DT_LANE0 + 2 * SSD_HEADS], 1, 2)
        yf_x, yb_x, yf_c, yb_c = _ssd(xbc_x, xbc_c, dt_x, dt_c, dtt_x, dtt_c, prm, l)

        fp = _hyena_filter_params(hy_w1[l], hy_b1[l], hy_freq1[l], hy_w2[l], hy_b2[l], hy_freq2[l], hy_w3[l])
        buf, sumabs = _filt_gen(fp, seq, 1024)
        hspec = _hyena_spectrum(buf.reshape(2, HY_HALVES, FFT_N1, FFT_N1, LANES), sumabs, consts, 2 * seq)
        ohy_x = _hyena_x(v_x, x1_x, x2_x, hspec, hy_bias.reshape(depth, 2, 1, HY_WIDTH), l, consts)

        x = _out_proj(of_x, ob_x, uhg_x, ohy_x, yf_x, yb_x, z_x, hgn, ssn, w_out_b, x, mods, l, None, 512)

        if not last:
            buf_c, sumabs_c = _filt_gen(fp, CTX_LEN, CTX_LEN)
            ohy_c = _hy_ctx(v_c, x1_c, x2_c, buf_c, sumabs_c, cs_ctx, hy_bias[l])
            ctx = _out_proj(of_c, ob_c, uhg_c, ohy_c, yf_c, yb_c, z_c, hgn, ssn, w_out_b, ctx, mods, l, ctx_row,
                            CTX_LEN)
            ctx = _ffn(ctx, mods, l, ctx_row, g2, wg_b, wu_b, cw_f, cb_f, wd_b, CTX_LEN, CTX_LEN, False)

        x = _ffn(x, mods, l, None, g2, wg_b, wu_b, cw_f, cb_f, wd_b, 1024, GRID_W, True,
                 final_g=final_norm_g if last else None)
    return x
```

```python
import functools
import math

import numpy as np
import jax
import jax.numpy as jnp
from jax import lax
from jax.experimental import pallas as pl
from jax.experimental.pallas import tpu as pltpu

F32 = jnp.float32
BF16 = jnp.bfloat16

D_MODEL = 1024
DEPTH = 2
CTX_LEN = 256
GRID_W = 64
EPS = 1e-6

HG_HEADS = 4
HG_DK = 64
HG_WIDTH = 256
HG_CHUNK = 32

HY_WIDTH = 256
HY_EMB_BANDS = 16
HY_HIDDEN = 64
HY_MIN_DECAY = math.log(1e-2) / 1.5
HY_MAX_DECAY = math.log(1e-2) / 0.3

SSD_HEADS = 8
SSD_HEAD_DIM = 64
SSD_WIDTH = 512
SSD_GROUPS = 2
SSD_STATE = 128
SSD_XBC = 1024
SSD_CHUNK = 64

D_FF = 2816
HG_COLS = 5 * HG_WIDTH
HY_COLS = 3 * HY_WIDTH

LANES = 128
HY_HALVES = HY_WIDTH // LANES
TOKEN_BLOCK = 256
SCAN_BATCH_ROWS = 2
FFT_N1 = 128
VMEM_LIMIT = 56 * 1024 * 1024


def _cparams(sem, big=False):
    kw = dict(dimension_semantics=sem)
    if big:
        kw["vmem_limit_bytes"] = VMEM_LIMIT
    return pltpu.CompilerParams(**kw)


def _dot(a, b):
    return lax.dot_general(a, b, (((1,), (0,)), ((), ())), preferred_element_type=F32)


def _dot_nt(a, b):
    return lax.dot_general(a, b, (((1,), (1,)), ((), ())), preferred_element_type=F32)


def _dot_tn(a, b):
    return lax.dot_general(a, b, (((0,), (0,)), ((), ())), preferred_element_type=F32)


def _split3(a):
    hi = a.astype(BF16)
    r = a - hi.astype(F32)
    mid = r.astype(BF16)
    lo = (r - mid.astype(F32)).astype(BF16)
    return hi, mid, lo


def _dot_exact_lhs(mask_bf16, a):
    h, m, l = _split3(a)
    return _dot(mask_bf16, h) + _dot(mask_bf16, m) + _dot(mask_bf16, l)


def _dot_exact_rhs(a, mask_bf16):
    h, m, l = _split3(a)
    return _dot(h, mask_bf16) + _dot(m, mask_bf16) + _dot(l, mask_bf16)


def _dot3(a, b):
    ah, am, _ = _split3(a)
    bh, bm, _ = _split3(b)
    return _dot(ah, bh) + _dot(ah, bm) + _dot(am, bh)


def _dot3_pre(fparts, b):
    fh, fm = fparts
    bh, bm, _ = _split3(b)
    return _dot(fh, bh) + _dot(fh, bm) + _dot(fm, bh)


def _sigmoid(x):
    return 1.0 / (1.0 + jnp.exp(-x))


def _silu(x):
    return x * _sigmoid(x)


def _softplus(x):
    return jnp.maximum(x, 0.0) + jnp.log(1.0 + jnp.exp(-jnp.abs(x)))


def _log_sigmoid(x):
    return jnp.minimum(x, 0.0) - jnp.log(1.0 + jnp.exp(-jnp.abs(x)))


def _ada_body(c_ref, w_ref, b_ref, o_ref):
    cc = c_ref[...]
    o_ref[0] = _dot3(_silu(cc), w_ref[0]) + b_ref[0]


def _ada(cc, w_ada, b_ada):
    tn = 1536
    n = w_ada.shape[-1]
    return pl.pallas_call(
        _ada_body,
        grid=(DEPTH, n // tn),
        in_specs=[
            pl.BlockSpec((8, D_MODEL), lambda l, j: (0, 0)),
            pl.BlockSpec((1, D_MODEL, tn), lambda l, j: (l, 0, j)),
            pl.BlockSpec((1, 1, tn), lambda l, j: (l, 0, j)),
        ],
        out_specs=pl.BlockSpec((1, 8, tn), lambda l, j: (l, 0, j)),
        out_shape=jax.ShapeDtypeStruct((DEPTH, 8, n), F32),
        compiler_params=_cparams(("arbitrary", "arbitrary"), big=True),
        name="adaln",
    )(cc, w_ada, b_ada.reshape(DEPTH, 1, n))


IN_COLS = HG_COLS + HY_COLS + SSD_WIDTH + SSD_XBC + 2 * SSD_HEADS
COL_HY = HG_COLS
COL_Z = COL_HY + HY_COLS
COL_XBC = COL_Z + SSD_WIDTH
COL_DT = COL_XBC + SSD_XBC
IN_COLS_PAD = COL_DT + LANES
DT_LANE0 = 0
CONV_CHUNK = 256

MOD_SHIFT1, MOD_SCALE1, MOD_GATE1, MOD_SHIFT2, MOD_SCALE2, MOD_GATE2 = range(6)


def _layer_spec(a, l):
    nd = a.ndim - 1
    return pl.BlockSpec((None,) + tuple(a.shape[1:]), lambda b, t: (l,) + (0,) * nd)


def _mod_spec(l, ctx_row, idx):
    row = (lambda b: b) if ctx_row is None else (lambda b: ctx_row)
    return pl.BlockSpec((None, None, None, 1, D_MODEL), lambda b, t: (l, row(b), idx, 0, 0))


def _inproj_body(x_ref, hp_ref, hn_ref, sh_ref, sc_ref, g_ref, w_ref, hcw, hcb, scw, scb,
                 uhg_ref, v_ref, x1_ref, x2_ref, xbc_ref, z_ref, dt_ref, *, nt):
    t = pl.program_id(1)

    def normmod(x):
        ms = jnp.mean(x * x, axis=-1, keepdims=True)
        y = x * lax.rsqrt(ms + EPS) * g_ref[...]
        return (y * (1.0 + sc_ref[...]) + sh_ref[...]).astype(BF16)

    yb = normmod(x_ref[0])
    rows = yb.shape[0]
    ye = jnp.concatenate([yb, normmod(jnp.concatenate([hp_ref[0], hn_ref[0]], axis=0))], axis=0)
    for j in range(0, HG_COLS, 512):
        w = min(512, HG_COLS - j)
        uhg_ref[0, :, j:j + w] = _dot(yb, w_ref[:, j:j + w])
    z_ref[0] = _dot(yb, w_ref[:, COL_Z:COL_Z + SSD_WIDTH])
    dt_ref[0] = _dot(yb, w_ref[:, COL_DT:])

    ri = lax.broadcasted_iota(jnp.int32, (rows, CONV_CHUNK), 0)

    def conv_chunk(col0, cw_ref, cb_ref, k):
        cols = slice(col0 + k * CONV_CHUNK, col0 + (k + 1) * CONV_CHUNK)
        ccols = slice(k * CONV_CHUNK, (k + 1) * CONV_CHUNK)
        ue = _dot(ye, w_ref[:, cols])
        u = ue[:rows]
        prev_row = jnp.where(t > 0, ue[rows + 7:rows + 8, :], 0.0)
        next_row = jnp.where(t < nt - 1, ue[rows + 8:rows + 9, :], 0.0)
        dn = jnp.where(ri == 0, prev_row, pltpu.roll(u, 1, 0))
        up = jnp.where(ri == rows - 1, next_row, pltpu.roll(u, rows - 1, 0))
        return dn * cw_ref[0:1, ccols] + u * cw_ref[1:2, ccols] + up * cw_ref[2:3, ccols] + cb_ref[:, ccols]

    for k, ref in enumerate((v_ref, x1_ref, x2_ref)):
        cv = conv_chunk(COL_HY, hcw, hcb, k)
        for hf in range(HY_HALVES):
            ref[hf, 0] = cv[:, hf * LANES:(hf + 1) * LANES]
    for k in range(SSD_XBC // CONV_CHUNK):
        cv = conv_chunk(COL_XBC, scw, scb, k)
        xbc_ref[0, :, k * CONV_CHUNK:(k + 1) * CONV_CHUNK] = _silu(cv).astype(xbc_ref.dtype)


def _inproj(x, mods, l, ctx_row, g, w, hcw, hcb, scw, scb, tm, name):
    bsz, seq, d = x.shape
    nt = seq // tm
    hb = tm // 8
    tok = lambda n: pl.BlockSpec((1, tm, n), lambda b, t: (b, t, 0))
    hy_spec = pl.BlockSpec((HY_HALVES, 1, tm, LANES), lambda b, t: (0, b, t, 0))
    hy_shape = jax.ShapeDtypeStruct((HY_HALVES, bsz, seq, LANES), F32)
    tshape = lambda n: jax.ShapeDtypeStruct((bsz, seq, n), F32)
    return pl.pallas_call(
        functools.partial(_inproj_body, nt=nt),
        grid=(bsz, nt),
        in_specs=[
            tok(d),
            pl.BlockSpec((1, 8, d), lambda b, t: (b, jnp.maximum(t * hb - 1, 0), 0)),
            pl.BlockSpec((1, 8, d), lambda b, t: (b, jnp.minimum((t + 1) * hb, nt * hb - 1), 0)),
            _mod_spec(l, ctx_row, MOD_SHIFT1), _mod_spec(l, ctx_row, MOD_SCALE1),
            _layer_spec(g, l), _layer_spec(w, l),
            _layer_spec(hcw, l), _layer_spec(hcb, l), _layer_spec(scw, l), _layer_spec(scb, l),
        ],
        out_specs=[tok(HG_COLS), hy_spec, hy_spec, hy_spec, tok(SSD_XBC), tok(SSD_WIDTH), tok(LANES)],
        out_shape=[tshape(HG_COLS), hy_shape, hy_shape, hy_shape,
                   jax.ShapeDtypeStruct((bsz, seq, SSD_XBC), BF16), tshape(SSD_WIDTH), tshape(LANES)],
        compiler_params=_cparams(("parallel", "arbitrary"), big=True),
        name=name,
    )(x, x, x, mods, mods, g, w, hcw, hcb, scw, scb)


def _gla_dir(q, a, v, p, r1, st_ref, reverse):
    tb, ch = TOKEN_BLOCK, HG_CHUNK
    nch = tb // ch
    qv = r1 + _log_sigmoid(a)
    logf = jnp.maximum(p, qv) + jnp.log(1.0 + jnp.exp(-jnp.abs(p - qv)))
    k = 1.0 - jnp.exp(logf)

    ri = lax.broadcasted_iota(jnp.int32, (tb, tb), 0)
    ci = lax.broadcasted_iota(jnp.int32, (tb, tb), 1)
    same = (ri // ch) == (ci // ch)
    tri = (ci >= ri) if reverse else (ci <= ri)
    mask = jnp.logical_and(same, tri)
    mask_b = jnp.where(mask, 1.0, 0.0).astype(BF16)
    bdiag = (ri // HG_DK) == (ci // HG_DK)

    lh, lm, ll = _split3(logf)
    bcs3 = _dot(mask_b, jnp.concatenate([lh, lm, ll], axis=1))
    bcs = bcs3[:, :tb] + bcs3[:, tb:2 * tb] + bcs3[:, 2 * tb:]
    b3 = bcs.reshape(nch, ch, tb)
    mid = ch // 2 if reverse else ch // 2 - 1
    end = 0 if reverse else ch - 1
    b_mid = b3[:, mid:mid + 1, :]
    b_end = b3[:, end:end + 1, :]
    q3 = q.reshape(nch, ch, tb)
    k3 = k.reshape(nch, ch, tb)
    d1 = b3 - b_mid
    qd = (q3 * jnp.exp(d1)).reshape(tb, tb)
    kd = (k3 * jnp.exp(-d1)).reshape(tb, tb).astype(BF16)
    kup = (k3 * jnp.exp(b_end - b3)).astype(BF16)
    qb = (q3 * jnp.exp(b3)).astype(BF16)
    dec = jnp.exp(b_end)
    yield None

    lane_head = lax.broadcasted_iota(jnp.int32, (1, tb), 1) // HG_DK
    acc = jnp.zeros((tb, tb), F32)
    for h in range(HG_HEADS):
        sel = lane_head == h
        qh = jnp.where(sel, qd, 0.0).astype(BF16)
        s = _dot_nt(qh, kd)
        s = jnp.where(mask, s, 0.0).astype(BF16)
        vh = jnp.where(sel, v, 0.0).astype(BF16)
        acc = acc + _dot(s, vh)
        yield None

    v3 = v.astype(BF16).reshape(nch, ch, tb)
    inter = [None] * nch
    order = range(nch - 1, -1, -1) if reverse else range(nch)
    st = st_ref[...]
    for c in order:
        inter[c] = _dot_nt(qb[c], st.astype(BF16))
        upd = _dot_tn(v3[c], kup[c])
        st = st * dec[c] + jnp.where(bdiag, upd, 0.0)
        yield None
    st_ref[...] = st
    yield acc + jnp.concatenate(inter, axis=0)


def _interleave(gens, lag=3):
    results = [None] * len(gens)
    done = [False] * len(gens)
    step = 0
    while not all(done):
        for i, g in enumerate(gens):
            if done[i] or step < i * lag:
                continue
            try:
                results[i] = next(g)
            except StopIteration:
                done[i] = True
        step += 1
    return results


def _gla_body(qf, ff, vf, qb, fb, vb, qc, ffc, fbc, vc, lb_ref, of_x, ob_x, of_c, ob_c, stf, stb, *, layer):
    t = pl.program_id(1)
    is_ctx = t == 0

    @pl.when(is_ctx)
    def _():
        stf[...] = jnp.zeros_like(stf)
        stb[...] = jnp.zeros_like(stb)

    lg = lb_ref[...]
    e = jnp.exp(lg - jnp.max(lg, axis=0, keepdims=True))
    den = jnp.sum(e, axis=0, keepdims=True)
    num = jnp.zeros_like(den)
    for r in range(1, layer + 1):
        num = num + e[r:r + 1, :]
    lb = num / den
    p = jnp.log(lb)
    r1 = jnp.log(1.0 - lb)

    scale = HG_DK ** -0.5
    nb = qf.shape[0]
    gens = []
    for b in range(nb):
        gens.append(_gla_dir(jnp.where(is_ctx, qc[b], qf[b]) * scale, jnp.where(is_ctx, ffc[b], ff[b]),
                             jnp.where(is_ctx, vc[b], vf[b]), p, r1, stf.at[b], False))
        gens.append(_gla_dir(jnp.where(is_ctx, qc[b], qb[b]) * scale, jnp.where(is_ctx, fbc[b], fb[b]),
                             jnp.where(is_ctx, vc[b], vb[b]), p, r1, stb.at[b], True))
    outs = _interleave(gens, lag=1)

    @pl.when(is_ctx)
    def _():
        for b in range(nb):
            of_c[b] = outs[2 * b].astype(of_c.dtype)
            ob_c[b] = outs[2 * b + 1].astype(ob_c.dtype)

    @pl.when(jnp.logical_not(is_ctx))
    def _():
        for b in range(nb):
            of_x[b] = outs[2 * b].astype(of_x.dtype)
            ob_x[b] = outs[2 * b + 1].astype(ob_x.dtype)


def _scan_block_maps(nxb):
    fwd = lambda t: jnp.maximum(t - 1, 0)
    bwd = lambda t: jnp.where(t == 0, nxb - 1, nxb - t)
    return fwd, bwd


def _gla(u_x, u_c, lb_logits, layer):
    bsz, seq, _ = u_x.shape
    tb = TOKEN_BLOCK
    nxb = seq // tb
    xf, xb = _scan_block_maps(nxb)

    nb = SCAN_BATCH_ROWS

    def xs(blk, col):
        return pl.BlockSpec((nb, tb, HG_WIDTH), lambda b, t: (b, blk(t), col))

    def cs(col):
        return pl.BlockSpec((nb, tb, HG_WIDTH), lambda b, t: (b, 0, col))

    out_x = jax.ShapeDtypeStruct((bsz, seq, HG_WIDTH), BF16)
    out_c = jax.ShapeDtypeStruct((bsz, CTX_LEN, HG_WIDTH), BF16)
    state = pltpu.VMEM((nb, HG_WIDTH, HG_WIDTH), F32)
    return pl.pallas_call(
        functools.partial(_gla_body, layer=layer),
        grid=(bsz // nb, nxb + 1),
        in_specs=[xs(xf, 0), xs(xf, 1), xs(xf, 3), xs(xb, 0), xs(xb, 2), xs(xb, 3),
                  cs(0), cs(1), cs(2), cs(3),
                  pl.BlockSpec((DEPTH, HG_WIDTH), lambda b, t: (0, 0))],
        out_specs=[xs(xf, 0), xs(xb, 0), cs(0), cs(0)],
        out_shape=[out_x, out_x, out_c, out_c],
        scratch_shapes=[state, state],
        compiler_params=_cparams(("parallel", "arbitrary"), big=True),
        name="hgrn2_scan",
    )(u_x, u_x, u_x, u_x, u_x, u_x, u_c, u_c, u_c, u_c, lb_logits)


def _ssd_dir(xbc, dt_raw, dtt_all, dtb_e, alog_e, dtb_t, alog_t, dsk, st_ref, reverse):
    tb, ch = TOKEN_BLOCK, SSD_CHUNK
    nch = tb // ch
    d = 1 if reverse else 0
    xs = xbc[:, :SSD_WIDTH].astype(F32)
    bm = xbc[:, SSD_WIDTH:SSD_WIDTH + 256].astype(BF16)
    cm = xbc[:, SSD_WIDTH + 256:].astype(BF16)

    ri = lax.broadcasted_iota(jnp.int32, (tb, tb), 0)
    ci = lax.broadcasted_iota(jnp.int32, (tb, tb), 1)
    same = (ri // ch) == (ci // ch)
    mask = jnp.logical_and(same, (ci >= ri) if reverse else (ci <= ri))
    mask_b = jnp.where(mask, 1.0, 0.0).astype(BF16)
    mask_t = jnp.logical_and(same, (ri >= ci) if reverse else (ri <= ci))
    mask_tb = jnp.where(mask_t, 1.0, 0.0).astype(BF16)

    ej = lax.broadcasted_iota(jnp.int32, (128, SSD_WIDTH), 0)
    el = lax.broadcasted_iota(jnp.int32, (128, SSD_WIDTH), 1)
    expand = jnp.where(ej == DT_LANE0 + d * SSD_HEADS + el // SSD_HEAD_DIM, 1.0, 0.0).astype(BF16)
    dh, dm, dl = _split3(dt_raw)
    dte_raw = _dot(jnp.concatenate([dh, dm, dl], axis=1), jnp.concatenate([expand] * 3, axis=0))
    dte = _softplus(dte_raw + dtb_e[d:d + 1, :])
    a_e = -jnp.exp(alog_e[d:d + 1, :])
    ah, am, al = _split3(dte * a_e)
    acs3p = _dot(mask_b, jnp.concatenate([ah, am, al], axis=1))
    acs = acs3p[:, :SSD_WIDTH] + acs3p[:, SSD_WIDTH:2 * SSD_WIDTH] + acs3p[:, 2 * SSD_WIDTH:]
    xdt = xs * dte
    acs3 = acs.reshape(nch, ch, SSD_WIDTH)
    end = 0 if reverse else ch - 1
    a_end = acs3[:, end:end + 1, :]
    xw = (xdt.reshape(nch, ch, SSD_WIDTH) * jnp.exp(a_end - acs3)).astype(BF16).reshape(tb, SSD_WIDTH)
    ea = jnp.exp(acs)
    dec = jnp.exp(a_end)

    dtt_raw = dtt_all[d * SSD_HEADS:(d + 1) * SSD_HEADS, :]
    dtt = _softplus(dtt_raw + dtb_t[d * SSD_HEADS:(d + 1) * SSD_HEADS, :])
    a_t = -jnp.exp(alog_t[d * SSD_HEADS:(d + 1) * SSD_HEADS, :])
    th, tm_, tl = _split3(dtt * a_t)
    acs_t3 = _dot(jnp.concatenate([th, tm_, tl], axis=0), mask_tb)
    acs_t = acs_t3[:SSD_HEADS] + acs_t3[SSD_HEADS:2 * SSD_HEADS] + acs_t3[2 * SSD_HEADS:]
    yield None

    def chunk_diag(m):
        z = jnp.zeros((ch, m.shape[1]), m.dtype)
        return jnp.concatenate(
            [jnp.concatenate([m[c * ch:(c + 1) * ch] if k == c else z for k in range(nch)], axis=1)
             for c in range(nch)], axis=0)

    xdt_b = xdt.astype(BF16)
    hd = SSD_HEAD_DIM
    hpg = SSD_HEADS // SSD_GROUPS
    gw = SSD_WIDTH // SSD_GROUPS
    rr = lax.broadcasted_iota(jnp.int32, (tb, gw), 0)
    ll = lax.broadcasted_iota(jnp.int32, (tb, gw), 1)
    tri4 = ((ll % hd) >= (rr % ch)) if reverse else ((ll % hd) <= (rr % ch))
    head_diag = (rr // hd) == (ll // hd)
    order = range(nch - 1, -1, -1) if reverse else range(nch)
    outs = []
    for g in range(SSD_GROUPS):
        st_cols = slice(g * SSD_STATE, (g + 1) * SSD_STATE)
        cols = slice(g * gw, (g + 1) * gw)
        cm_d = chunk_diag(cm[:, st_cols])
        bm_cat = jnp.concatenate([bm[c * ch:(c + 1) * ch, st_cols] for c in range(nch)], axis=1)
        cb4 = _dot_nt(cm_d, jnp.concatenate([bm_cat] * hpg, axis=0))
        rowm = jnp.concatenate(
            [jnp.broadcast_to(jnp.concatenate([acs_t[g * hpg + h:g * hpg + h + 1, c * ch:(c + 1) * ch]
                                               for h in range(hpg)], axis=1), (ch, gw))
             for c in range(nch)], axis=0)
        w4 = (cb4 * jnp.exp(jnp.where(tri4, acs[:, cols] - rowm, -1e30))).astype(BF16)
        xg = xdt_b[:, cols]
        zero = jnp.zeros((hpg * ch, gw), BF16)
        x4 = jnp.concatenate(
            [jnp.where(head_diag, jnp.concatenate([xg[c * ch:(c + 1) * ch]] * hpg, axis=0), zero)
             for c in range(nch)], axis=0)
        y_intra = _dot(chunk_diag(w4), x4)
        yield None

        upd = _dot_tn(bm[:, st_cols], chunk_diag(xw[:, cols]))
        st = st_ref[g]
        entering = [None] * nch
        for c in order:
            entering[c] = st.astype(BF16)
            st = st * dec[c][:, cols] + upd[:, c * gw:(c + 1) * gw]
        st_ref[g] = st
        y_inter = _dot(cm_d, jnp.concatenate(entering, axis=0)) * ea[:, cols]
        outs.append(y_intra + y_inter)
        yield None
    o = jnp.concatenate(outs, axis=1)
    if not reverse:
        o = o + dsk[...] * xs
    yield o


def _ssd_body(xf, dtf, dttf, xb, dtb, dttb, xc, dtc, dttc, dtb_e, alog_e, dtb_t, alog_t, dsk,
              of_x, ob_x, of_c, ob_c, stf, stb):
    t = pl.program_id(1)
    is_ctx = t == 0

    @pl.when(is_ctx)
    def _():
        stf[...] = jnp.zeros_like(stf)
        stb[...] = jnp.zeros_like(stb)

    prm = (dtb_e, alog_e, dtb_t, alog_t, dsk)
    nb = xf.shape[0]
    gens = []
    for b in range(nb):
        gens.append(_ssd_dir(jnp.where(is_ctx, xc[b], xf[b]), jnp.where(is_ctx, dtc[b], dtf[b]),
                             jnp.where(is_ctx, dttc[b], dttf[b]), *prm, stf.at[b], False))
        gens.append(_ssd_dir(jnp.where(is_ctx, xc[b], xb[b]), jnp.where(is_ctx, dtc[b], dtb[b]),
                             jnp.where(is_ctx, dttc[b], dttb[b]), *prm, stb.at[b], True))
    outs = _interleave(gens, lag=2)

    @pl.when(is_ctx)
    def _():
        for b in range(nb):
            of_c[b] = outs[2 * b].astype(of_c.dtype)
            ob_c[b] = outs[2 * b + 1].astype(ob_c.dtype)

    @pl.when(jnp.logical_not(is_ctx))
    def _():
        for b in range(nb):
            of_x[b] = outs[2 * b].astype(of_x.dtype)
            ob_x[b] = outs[2 * b + 1].astype(ob_x.dtype)


def _ssd(xbc_x, xbc_c, dt_x, dt_c, dtt_x, dtt_c, prm, l):
    bsz, seq, _ = xbc_x.shape
    tb = TOKEN_BLOCK
    nxb = seq // tb
    xf, xb = _scan_block_maps(nxb)
    full = lambda a: _layer_spec(a, l)
    params = [prm["dtb_e"], prm["alog_e"], prm["dtb_t"], prm["alog_t"], prm["dsk"]]

    nb = SCAN_BATCH_ROWS

    def xspecs(blk):
        return [pl.BlockSpec((nb, tb, SSD_XBC), lambda b, t: (b, blk(t), 0)),
                pl.BlockSpec((nb, tb, 128), lambda b, t: (b, blk(t), 0)),
                pl.BlockSpec((nb, 16, tb), lambda b, t: (b, 0, blk(t)))]

    zero = lambda t: 0
    yspec = lambda blk: pl.BlockSpec((nb, tb, SSD_WIDTH), lambda b, t: (b, blk(t), 0))
    out_x = jax.ShapeDtypeStruct((bsz, seq, SSD_WIDTH), BF16)
    out_c = jax.ShapeDtypeStruct((bsz, CTX_LEN, SSD_WIDTH), BF16)
    state = pltpu.VMEM((nb, SSD_GROUPS, SSD_STATE, SSD_WIDTH // SSD_GROUPS), F32)
    return pl.pallas_call(
        _ssd_body,
        grid=(bsz // nb, nxb + 1),
        in_specs=xspecs(xf) + xspecs(xb) + xspecs(zero) + [full(a) for a in params],
        out_specs=[yspec(xf), yspec(xb), yspec(zero), yspec(zero)],
        out_shape=[out_x, out_x, out_c, out_c],
        scratch_shapes=[state, state],
        compiler_params=_cparams(("parallel", "arbitrary"), big=True),
        name="ssd_scan",
    )(xbc_x, dt_x, dtt_x, xbc_x, dt_x, dtt_x, xbc_c, dt_c, dtt_c, *params)


def _filt_body(w1, b1, f1, w2, b2, f2, w3, delta, buf_ref, sa_ref, *, seq_len, tr):
    i = pl.program_id(0)
    hr = tr // 2
    lane = lax.broadcasted_iota(jnp.int32, (hr, 128), 1)
    grp, sub = lane // HY_HIDDEN, lane % HY_HIDDEN
    jrow = i * tr + grp * hr + lax.broadcasted_iota(jnp.int32, (hr, 128), 0)
    pos = jnp.where(jrow < seq_len, jrow, 2 * seq_len - jrow).astype(F32)
    tpos = pos / (seq_len - 1.0)
    wpos = pos * (2.0 * math.pi / seq_len)
    bidx = jnp.where(sub <= HY_EMB_BANDS, sub - 1, sub - 1 - HY_EMB_BANDS).astype(F32)
    band = 1e-4 + bidx * ((HY_EMB_BANDS - 1 - 1e-4) / (HY_EMB_BANDS - 1))
    ang = band * wpos + jnp.where(sub > HY_EMB_BANDS, 0.5 * math.pi, 0.0)
    z = jnp.where(sub == 0, tpos, jnp.where(sub <= 2 * HY_EMB_BANDS, jnp.cos(ang), 0.0))
    h = jnp.sin(f1[...] * (_dot3(z, w1[...]) + b1[...]))
    h = jnp.sin(f2[...] * (_dot3(h, w2[...]) + b2[...]))

    @pl.when(i == 0)
    def _():
        sa_ref[...] = jnp.zeros_like(sa_ref)

    for g in range(2):
        o = _dot3(h, w3[0, g])
        o = o * jnp.exp(-tpos[:, g * HY_HIDDEN:g * HY_HIDDEN + 1] * delta[...])
        o = jnp.where(jrow[:, g * HY_HIDDEN:g * HY_HIDDEN + 1] == seq_len, 0.0, o)
        for order in range(2):
            for hf in range(HY_HALVES):
                lo = order * HY_WIDTH + hf * LANES
                buf_ref[order, hf, g * hr:(g + 1) * hr, :] = o[:, lo:lo + LANES]
        sa_ref[...] = sa_ref[...] + jnp.sum(jnp.abs(o), axis=0, keepdims=True)


def _filt_gen(p, seq_len, tr):
    n = 2 * seq_len
    assert seq_len % tr == 0
    half = seq_len // tr
    full = lambda a: pl.BlockSpec(a.shape, lambda i: (0,) * a.ndim)
    small = [p["w1"], p["b1"], p["f1"], p["w2"], p["b2"], p["f2"]]
    return pl.pallas_call(
        functools.partial(_filt_body, seq_len=seq_len, tr=tr),
        grid=(n // tr,),
        in_specs=[full(a) for a in small] + [
            pl.BlockSpec((1, 2, 2 * HY_HIDDEN, 2 * HY_WIDTH), lambda i: (i // half, 0, 0, 0)),
            full(p["delta"]),
        ],
        out_specs=[pl.BlockSpec((2, HY_HALVES, tr, LANES), lambda i: (0, 0, i, 0)),
                   pl.BlockSpec((8, 2 * HY_WIDTH), lambda i: (0, 0))],
        out_shape=[jax.ShapeDtypeStruct((2, HY_HALVES, n, LANES), F32),
                   jax.ShapeDtypeStruct((8, 2 * HY_WIDTH), F32)],
        compiler_params=_cparams(("arbitrary",)),
        name="hyena_filter",
    )(*small, p["w3"], p["delta"])


FFT_NB = 8


def _strided_rows(ref3, j, n):
    flat = ref3.reshape(ref3.shape[0] * FFT_NB, LANES)
    return flat[pl.ds(j, n, stride=FFT_NB), :]


def _store_strided_rows(ref3, j, val):
    flat = ref3.reshape(ref3.shape[0] * FFT_NB, LANES)
    flat[pl.ds(j, val.shape[0], stride=FFT_NB), :] = val


def _load_complex(ref, k):
    return jnp.concatenate(
        [jnp.concatenate([ref[hf, 0, k], ref[hf, 1, k]], axis=0) for hf in range(HY_HALVES)], axis=1)


def _store_complex(ref, k, val):
    for hf in range(HY_HALVES):
        ref[hf, 0, k] = val[:FFT_N1, hf * LANES:(hf + 1) * LANES].astype(ref.dtype)
        ref[hf, 1, k] = val[FFT_N1:, hf * LANES:(hf + 1) * LANES].astype(ref.dtype)


FFT_TILES = FFT_N1 // FFT_NB
TILE_ROWS = 2 * FFT_N1 * FFT_NB


def _spec_stage1(x_refs, f_ref, t_ref, scr, j, rows):
    for i in range(FFT_NB):
        x = jnp.concatenate([_strided_rows(r, i, rows) for r in x_refs], axis=1)
        a = _dot3_pre((f_ref[0], f_ref[1]), x)
        tr = t_ref[0, 0, :, i:i + 1]
        ti = t_ref[0, 1, :, i:i + 1]
        ar, ai = a[:FFT_N1], a[FFT_N1:]
        out = jnp.concatenate([ar * tr - ai * ti, ar * ti + ai * tr], axis=0)
        for hf in range(HY_HALVES):
            scr[hf, pl.ds(j * TILE_ROWS + i, 2 * FFT_N1, stride=FFT_NB), :] = out[:, hf * LANES:(hf + 1) * LANES]


def _spec_slab(scr, k1):
    def part(hf, p):
        r0 = (p * FFT_N1 + k1) * FFT_NB
        return jnp.concatenate(
            [scr[hf, pl.ds(pl.multiple_of(jj * TILE_ROWS + r0, FFT_NB), FFT_NB), :] for jj in range(FFT_TILES)], axis=0)
    return jnp.concatenate(
        [jnp.concatenate([part(hf, 0), part(hf, 1)], axis=0) for hf in range(HY_HALVES)], axis=1)


def _spec_slab_store(scr, k1, val):
    for hf in range(HY_HALVES):
        for p in range(2):
            r0 = (p * FFT_N1 + k1) * FFT_NB
            for jj in range(FFT_TILES):
                scr[hf, pl.ds(pl.multiple_of(jj * TILE_ROWS + r0, FFT_NB), FFT_NB), :] = val[
                    p * FFT_N1 + jj * FFT_NB:p * FFT_N1 + (jj + 1) * FFT_NB, hf * LANES:(hf + 1) * LANES]


def _hconv_body(y_ref, h_ref, gate_ref, f1_ref, f2_ref, f2c_ref, f3_ref, t_ref, bias_ref, o_ref, scr, *, kb):
    s = pl.program_id(0)
    rows = y_ref.shape[1]

    @pl.when(s < FFT_TILES)
    def _():
        _spec_stage1([y_ref.at[hf] for hf in range(HY_HALVES)], f1_ref, t_ref, scr, s, rows)

    @pl.when(jnp.logical_and(s >= FFT_TILES, s < 2 * FFT_TILES))
    def _():
        kt = s - FFT_TILES
        for k in range(kb):
            k1 = kt * kb + k
            x = _dot3_pre((f2_ref[0], f2_ref[1]), _spec_slab(scr, k1))
            xr, xi = x[:FFT_N1], x[FFT_N1:]
            h = _load_complex(h_ref, k).astype(F32)
            hr, hi = h[:FFT_N1], h[FFT_N1:]
            y = jnp.concatenate([xr * hr - xi * hi, xr * hi + xi * hr], axis=0)
            _spec_slab_store(scr, k1, _dot3_pre((f2c_ref[0], f2c_ref[1]), y))

    @pl.when(s >= 2 * FFT_TILES)
    def _():
        j = s - 2 * FFT_TILES
        for i in range(FFT_NB):
            tr = t_ref[0, 0, :, i:i + 1]
            ti = t_ref[0, 1, :, i:i + 1]
            c = jnp.concatenate(
                [scr[hf, pl.ds(j * TILE_ROWS + i, 2 * FFT_N1, stride=FFT_NB), :] for hf in range(HY_HALVES)], axis=1)
            cr, ci = c[:FFT_N1], c[FFT_N1:]
            dmat = jnp.concatenate([cr * tr + ci * ti, ci * tr - cr * ti], axis=0)
            w = _dot3_pre((f3_ref[0], f3_ref[1]), dmat)
            for hf in range(HY_HALVES):
                lanes = slice(hf * LANES, (hf + 1) * LANES)
                g = _strided_rows(gate_ref.at[hf], i, rows)
                v = _strided_rows(y_ref.at[hf], i, rows)
                _store_strided_rows(o_ref.at[hf], i, g * (w[:, lanes] + v * bias_ref[:, lanes]))


def _hyena_conv(y, gate, hspec, bias, l, order, consts, kb=8):
    nh, rows, n2, lanes = y.shape
    nt = FFT_TILES
    tile = lambda f: pl.BlockSpec((nh, rows, FFT_NB, lanes), lambda s: (0, 0, f(s), 0))
    in_tile = lambda s: jnp.where(s < nt, s, jnp.maximum(s - 2 * nt, 0))
    out_tile = lambda s: jnp.maximum(s - 2 * nt, 0)
    const = lambda a: pl.BlockSpec(a.shape, lambda s: (0,) * a.ndim)
    return pl.pallas_call(
        functools.partial(_hconv_body, kb=kb),
        grid=(3 * nt,),
        in_specs=[
            tile(in_tile),
            pl.BlockSpec((None, nh, 2, kb, FFT_N1, lanes),
                         lambda s: (order, 0, 0, jnp.clip(s - nt, 0, FFT_N1 // kb - 1), 0, 0)),
            tile(out_tile),
            const(consts["f1_data"]), const(consts["f2"]), const(consts["f2c"]), const(consts["f3"]),
            pl.BlockSpec((1, 2, FFT_N1, FFT_NB), lambda s: (in_tile(s), 0, 0, 0)),
            pl.BlockSpec((None, None, 1, HY_WIDTH), lambda s: (l, order, 0, 0)),
        ],
        out_specs=tile(out_tile),
        out_shape=jax.ShapeDtypeStruct(y.shape, F32),
        scratch_shapes=[pltpu.VMEM((nh, nt * TILE_ROWS, lanes), F32)],
        compiler_params=_cparams(("arbitrary",), big=True),
        name="hyena_conv",
    )(y, hspec, gate, consts["f1_data"], consts["f2"], consts["f2c"], consts["f3"], consts["twid"], bias)


def _hspec_body(b_ref, sa_ref, f1_ref, f2_ref, t_ref, o_ref, scr, *, kb, n):
    o = pl.program_id(0)
    s = pl.program_id(1)
    rows = b_ref.shape[1]

    @pl.when(s < FFT_TILES)
    def _():
        _spec_stage1([b_ref.at[hf] for hf in range(HY_HALVES)], f1_ref, t_ref, scr, s, rows)

    @pl.when(s >= FFT_TILES)
    def _():
        sa = jnp.where(o == 0, sa_ref[0:1, :HY_WIDTH], sa_ref[0:1, HY_WIDTH:])
        scale = 1.0 / (sa * float(n))
        kt = s - FFT_TILES
        for k in range(kb):
            x = _dot3_pre((f2_ref[0], f2_ref[1]), _spec_slab(scr, kt * kb + k))
            _store_complex(o_ref, k, x * scale)


def _hyena_spectrum(buf, sumabs, consts, n, kb=8):
    _, nh, rows, n2, lanes = buf.shape
    nt = FFT_TILES
    const = lambda a: pl.BlockSpec(a.shape, lambda o, s: (0,) * a.ndim)
    return pl.pallas_call(
        functools.partial(_hspec_body, kb=kb, n=n),
        grid=(2, 2 * nt),
        in_specs=[
            pl.BlockSpec((None, nh, rows, FFT_NB, lanes), lambda o, s: (o, 0, 0, jnp.minimum(s, nt - 1), 0)),
            const(sumabs), const(consts["f1_real"]), const(consts["f2"]),
            pl.BlockSpec((1, 2, FFT_N1, FFT_NB), lambda o, s: (jnp.minimum(s, nt - 1), 0, 0, 0)),
        ],
        out_specs=pl.BlockSpec((None, nh, 2, kb, FFT_N1, lanes), lambda o, s: (o, 0, 0, jnp.maximum(s - nt, 0), 0, 0)),
        out_shape=jax.ShapeDtypeStruct((2, nh, 2, FFT_N1, FFT_N1, lanes), BF16),
        scratch_shapes=[pltpu.VMEM((nh, nt * TILE_ROWS, lanes), F32)],
        compiler_params=_cparams(("arbitrary", "arbitrary"), big=True),
        name="hyena_spectrum",
    )(buf, sumabs, consts["f1_real"], consts["f2"], consts["twid"])


def _hyc_body(v_ref, x1_ref, x2_ref, buf_ref, sa_ref, cs_ref, bias_ref, o_ref):
    n = 2 * CTX_LEN
    cmat, smat = cs_ref[0], cs_ref[1]
    c_in, s_in = cmat[:, :CTX_LEN], smat[:, :CTX_LEN]
    c_out, s_out = cmat[:CTX_LEN, :], smat[:CTX_LEN, :]
    batch = lambda ref, b: jnp.concatenate([ref[hf, b] for hf in range(HY_HALVES)], axis=1)
    yr, yi = batch(v_ref, 0), batch(v_ref, 1)
    for o, gate in enumerate((x1_ref, x2_ref)):
        bufo = jnp.concatenate([buf_ref[o, hf] for hf in range(HY_HALVES)], axis=1)
        scale = 1.0 / (sa_ref[0:1, o * HY_WIDTH:(o + 1) * HY_WIDTH] * float(n))
        hr = _dot3(cmat, bufo) * scale
        hi = -_dot3(smat, bufo) * scale
        xr = _dot3(c_in, yr) + _dot3(s_in, yi)
        xi = _dot3(c_in, yi) - _dot3(s_in, yr)
        zr = xr * hr - xi * hi
        zi = xr * hi + xi * hr
        wr = _dot3(c_out, zr) - _dot3(s_out, zi)
        wi = _dot3(c_out, zi) + _dot3(s_out, zr)
        b = bias_ref[o:o + 1, :]
        yr = batch(gate, 0) * (wr + yr * b)
        yi = batch(gate, 1) * (wi + yi * b)
    for hf in range(HY_HALVES):
        o_ref[hf, 0] = yr[:, hf * LANES:(hf + 1) * LANES]
        o_ref[hf, 1] = yi[:, hf * LANES:(hf + 1) * LANES]


def _hy_ctx(v, x1, x2, buf, sumabs, cs, bias):
    return pl.pallas_call(
        _hyc_body,
        out_shape=jax.ShapeDtypeStruct(v.shape, F32),
        compiler_params=pltpu.CompilerParams(vmem_limit_bytes=VMEM_LIMIT),
        name="hyena_ctx",
    )(v, x1, x2, buf, sumabs, cs, bias)


def _dft_constants(seq_len):
    n = 2 * seq_len
    n1 = FFT_N1
    assert n == n1 * n1
    idx = np.arange(n1, dtype=np.float64)
    th = 2.0 * np.pi * np.outer(idx, idx) / n1
    fr, fi = np.cos(th), -np.sin(th)
    half = n1 // 2

    def parts(m):
        m32 = jnp.asarray(m, F32)
        hi = m32.astype(BF16)
        mid = (m32 - hi.astype(F32)).astype(BF16)
        return jnp.stack([hi, mid])

    f1_data = np.block([[fr[:, :half], -fi[:, :half]], [fi[:, :half], fr[:, :half]]])
    f1_real = np.concatenate([fr, fi], axis=0)
    f2 = np.block([[fr, -fi], [fi, fr]])
    f2c = np.block([[fr, fi], [-fi, fr]])
    f3 = np.block([[fr[:half], fi[:half]], [-fi[:half], fr[:half]]])
    tw = 2.0 * np.pi * np.outer(idx, idx) / n
    twid = np.stack([np.cos(tw), -np.sin(tw)])
    twid = twid.reshape(2, n1, n1 // FFT_NB, FFT_NB).transpose(2, 0, 1, 3)
    return dict(f1_data=parts(f1_data), f1_real=parts(f1_real), f2=parts(f2), f2c=parts(f2c),
                f3=parts(f3), twid=jnp.asarray(twid, F32))


def _ctx_dft_constants():
    n = 2 * CTX_LEN
    idx = np.arange(n, dtype=np.float64)
    th = 2.0 * np.pi * np.outer(idx, idx) / n
    return jnp.asarray(np.stack([np.cos(th), np.sin(th)]), F32)


def _hyena_filter_params(hy_w1, hy_b1, hy_freq1, hy_w2, hy_b2, hy_freq2, hy_w3):
    hid = HY_HIDDEN
    zeros = jnp.zeros((hid, hid), F32)
    w1 = jnp.zeros((hid, hid), F32).at[:hy_w1.shape[0]].set(hy_w1)
    w1 = jnp.block([[w1, zeros], [zeros, w1]])
    w2 = jnp.block([[hy_w2, zeros], [zeros, hy_w2]])
    w3 = hy_w3.reshape(hid, 2, 2 * HY_WIDTH).transpose(1, 0, 2)
    z3 = jnp.zeros_like(w3)
    w3 = jnp.stack([jnp.concatenate([w3, z3], axis=1), jnp.concatenate([z3, w3], axis=1)], axis=1)
    deltas = np.abs(np.linspace(HY_MIN_DECAY, HY_MAX_DECAY, HY_WIDTH))
    delta = jnp.asarray(np.tile(deltas, 2)[None, :], F32)
    row = lambda a: jnp.tile(a, 2).reshape(1, -1)
    return dict(w1=w1, b1=row(hy_b1), f1=row(hy_freq1), w2=w2, b2=row(hy_b2), f2=row(hy_freq2),
                w3=w3, delta=delta)


def _hyena_x(v, x1, x2, hspec, bias, l, consts):
    nh, bsz, seq, lanes = v.shape
    assert bsz == 2
    rows = bsz * (seq // FFT_N1)
    view = lambda a: a.reshape(nh, rows, FFT_N1, lanes)
    y = view(v)
    for o, gate in enumerate((x1, x2)):
        y = _hyena_conv(y, view(gate), hspec, bias, l, o, consts)
    return y.reshape(nh, bsz, seq, lanes)


def _out_body(of, ob, ug, hy, yf, yb, zz, hgn, ssn, w_ref, h_ref, gate_ref, o_ref):
    o = of[0].astype(F32) + ob[0].astype(F32)
    gi = lax.broadcasted_iota(jnp.int32, (HG_WIDTH, HG_WIDTH), 0) // HG_DK
    gj = lax.broadcasted_iota(jnp.int32, (HG_WIDTH, HG_WIDTH), 1) // HG_DK
    avg = jnp.where(gi == gj, 1.0 / HG_DK, 0.0).astype(BF16)
    sq = o * o
    sh = sq.astype(BF16)
    sl = (sq - sh.astype(F32)).astype(BF16)
    ms = _dot(sh, avg) + _dot(sl, avg)
    hg = o * lax.rsqrt(ms + EPS) * hgn[...] * _silu(ug[0])
    y = (yf[0].astype(F32) + yb[0].astype(F32)) * _silu(zz[0])
    gw = SSD_WIDTH // SSD_GROUPS
    parts = []
    for g in range(SSD_GROUPS):
        yg = y[:, g * gw:(g + 1) * gw]
        parts.append(yg * lax.rsqrt(jnp.mean(yg * yg, axis=-1, keepdims=True) + EPS))
    ys = jnp.concatenate(parts, axis=1) * ssn[...]
    acc = _dot(hg.astype(BF16), w_ref[0:HG_WIDTH, :])
    hyv = jnp.concatenate([hy[hf, 0] for hf in range(HY_HALVES)], axis=1)
    acc = acc + _dot(hyv.astype(BF16), w_ref[HG_WIDTH:HG_WIDTH + HY_WIDTH, :])
    acc = acc + _dot(ys.astype(BF16), w_ref[HG_WIDTH + HY_WIDTH:, :])
    o_ref[0] = h_ref[0] + gate_ref[...] * acc


def _out_proj(of, ob, u_hg, hy, yf, yb, zgate, hgn, ssn, w, h, mods, l, ctx_row, tm):
    bsz, seq, d = h.shape
    tok = lambda n, col=0: pl.BlockSpec((1, tm, n), lambda b, t: (b, t, col))
    return pl.pallas_call(
        _out_body,
        grid=(bsz, seq // tm),
        in_specs=[tok(HG_WIDTH), tok(HG_WIDTH), tok(HG_WIDTH, 4),
                  pl.BlockSpec((HY_HALVES, 1, tm, LANES), lambda b, t: (0, b, t, 0)),
                  tok(SSD_WIDTH), tok(SSD_WIDTH), tok(SSD_WIDTH),
                  _layer_spec(hgn, l), _layer_spec(ssn, l), _layer_spec(w, l),
                  tok(d),
                  _mod_spec(l, ctx_row, MOD_GATE1)],
        out_specs=tok(d),
        out_shape=jax.ShapeDtypeStruct(h.shape, F32),
        compiler_params=_cparams(("parallel", "arbitrary"), big=True),
        name="mixer_out_proj",
    )(of, ob, u_hg, hy, yf, yb, zgate, hgn, ssn, w, h, mods)


def _ffn_body(*refs, tm, grid_w, vertical, final, cchunk):
    if vertical:
        x_ref, hp, hn, sh_ref, sc_ref, g_ref, wg, wu, cw, cb, wd, gate_ref = refs[:12]
        rest = refs[12:]
    else:
        x_ref, sh_ref, sc_ref, g_ref, wg, wu, cw, cb, wd, gate_ref = refs[:10]
        rest = refs[10:]
    if final:
        fg_ref, o_ref, act_ref = rest
    else:
        o_ref, act_ref = rest
    t = pl.program_id(1)
    nt = pl.num_programs(1)

    def normmod(v):
        ms = jnp.mean(v * v, axis=-1, keepdims=True)
        y = v * lax.rsqrt(ms + EPS) * g_ref[...]
        return (y * (1.0 + sc_ref[...]) + sh_ref[...]).astype(BF16)

    x = x_ref[0]
    yb = normmod(x)
    pad = grid_w if vertical else 0
    ext_rows = tm + 2 * pad
    ye = jnp.concatenate([normmod(hp[0]), yb, normmod(hn[0])], axis=0) if vertical else yb
    col = lax.broadcasted_iota(jnp.int32, (ext_rows, cchunk), 0) % grid_w
    for j in range(0, D_FF, cchunk):
        cs = slice(j, j + cchunk)
        ext = _dot(ye, wg[:, cs])
        if vertical:
            ext = jnp.concatenate([jnp.where(t > 0, ext[:pad], 0.0), ext[pad:pad + tm],
                                   jnp.where(t < nt - 1, ext[pad + tm:], 0.0)], axis=0)
        left = jnp.where(col != 0, pltpu.roll(ext, 1, 0), 0.0)
        right = jnp.where(col != grid_w - 1, pltpu.roll(ext, ext_rows - 1, 0), 0.0)
        acc = jnp.zeros((tm, cchunk), F32) + cb[:, cs]
        for dy in ((-1, 0, 1) if vertical else (0,)):
            r0 = pad + dy * grid_w
            ky = dy + 1
            acc = acc + left[r0:r0 + tm] * cw[3 * ky:3 * ky + 1, cs]
            acc = acc + ext[r0:r0 + tm] * cw[3 * ky + 1:3 * ky + 2, cs]
            acc = acc + right[r0:r0 + tm] * cw[3 * ky + 2:3 * ky + 3, cs]
        act_ref[:, cs] = (_silu(acc) * _dot(yb, wu[:, cs])).astype(BF16)
    y = _dot(act_ref[...], wd[...])
    x = x + gate_ref[...] * y
    if final:
        ms = jnp.mean(x * x, axis=-1, keepdims=True)
        x = x * lax.rsqrt(ms + EPS) * fg_ref[...]
    o_ref[0] = x


def _ffn(h, mods, l, ctx_row, g, wg, wu, cw, cb, wd, tm, grid_w, vertical, final_g=None):
    bsz, seq, d = h.shape
    nhb = tm // grid_w
    tok = lambda n: pl.BlockSpec((1, tm, n), lambda b, t: (b, t, 0))
    full = lambda a: _layer_spec(a, l)
    resident = lambda a: pl.BlockSpec((None,) + tuple(a.shape[1:]), lambda b, t: (l, 0, 0),
                                      pipeline_mode=pl.Buffered(1))
    in_specs = [tok(d)]
    args = [h]
    if vertical:
        nrow = seq // grid_w
        in_specs += [
            pl.BlockSpec((1, grid_w, d), lambda b, t: (b, jnp.maximum(t * nhb - 1, 0), 0)),
            pl.BlockSpec((1, grid_w, d), lambda b, t: (b, jnp.minimum((t + 1) * nhb, nrow - 1), 0)),
        ]
        args += [h, h]
    in_specs += [_mod_spec(l, ctx_row, MOD_SHIFT2), _mod_spec(l, ctx_row, MOD_SCALE2), full(g),
                 resident(wg), resident(wu), full(cw), full(cb), resident(wd),
                 _mod_spec(l, ctx_row, MOD_GATE2)]
    args += [mods, mods, g, wg, wu, cw, cb, wd, mods]
    if final_g is not None:
        in_specs.append(pl.BlockSpec((1, d), lambda b, t: (0, 0)))
        args.append(final_g.reshape(1, d))
    return pl.pallas_call(
        functools.partial(_ffn_body, tm=tm, grid_w=grid_w, vertical=vertical,
                          final=final_g is not None, cchunk=256),
        grid=(bsz, seq // tm),
        in_specs=in_specs,
        out_specs=tok(d),
        out_shape=jax.ShapeDtypeStruct(h.shape, F32),
        scratch_shapes=[pltpu.VMEM((tm, D_FF), BF16)],
        compiler_params=_cparams(("parallel", "arbitrary"), big=True),
        name="ffn",
    )(*args)


def kernel(x, c, ctx, c_ctx, w_ada, b_ada, norm1_g, norm2_g, w_in, w_out, hg_lb_logits, hg_norm_g, hy_conv_w, hy_conv_b, hy_w1, hy_b1, hy_freq1, hy_w2, hy_b2, hy_freq2, hy_w3, hy_bias, ssd_conv_w, ssd_conv_b, ssd_dt_bias, ssd_a_log, ssd_d, ssd_norm_g, ffn_w_gate, ffn_w_up, ffn_conv_w, ffn_conv_b, ffn_w_down, final_norm_g):
    bsz, seq, d = x.shape
    depth = w_in.shape[0]
    cc = jnp.zeros((8, d), F32).at[:bsz].set(c).at[bsz].set(c_ctx)
    mods = _ada(cc, w_ada, b_ada).reshape(depth, 8, 6, 1, d)
    ctx_row = bsz

    consts = _dft_constants(seq)
    cs_ctx = _ctx_dft_constants()
    lb_logits = hg_lb_logits.astype(F32)

    row3 = lambda a: a.reshape(depth, 1, -1)
    rep = lambda a: jnp.repeat(a, SSD_HEAD_DIM, axis=-1)
    w_in_b = jnp.pad(w_in, ((0, 0), (0, 0), (0, IN_COLS_PAD - w_in.shape[-1]))).astype(BF16)
    w_out_b = w_out.astype(BF16)
    wg_b, wu_b, wd_b = ffn_w_gate.astype(BF16), ffn_w_up.astype(BF16), ffn_w_down.astype(BF16)
    g1, g2 = row3(norm1_g), row3(norm2_g)
    hcb, scb = row3(hy_conv_b), row3(ssd_conv_b)
    hgn, ssn = row3(hg_norm_g), row3(ssd_norm_g)
    cw_f, cb_f = ffn_conv_w.reshape(depth, 9, D_FF), row3(ffn_conv_b)
    prm = dict(dtb_e=rep(ssd_dt_bias), alog_e=rep(ssd_a_log),
               dtb_t=ssd_dt_bias.reshape(depth, -1, 1), alog_t=ssd_a_log.reshape(depth, -1, 1),
               dsk=row3(rep(ssd_d)))

    for l in range(depth):
        last = l == depth - 1
        proj = lambda h, row, tm, name: _inproj(h, mods, l, row, g1, w_in_b, hy_conv_w, hcb, ssd_conv_w, scb,
                                                tm, name)
        uhg_x, v_x, x1_x, x2_x, xbc_x, z_x, dt_x = proj(x, None, 512, "in_proj_x")
        uhg_c, v_c, x1_c, x2_c, xbc_c, z_c, dt_c = proj(ctx, ctx_row, CTX_LEN, "in_proj_ctx")

        of_x, ob_x, of_c, ob_c = _gla(uhg_x, uhg_c, lb_logits, l)

        dtt_x = jnp.swapaxes(dt_x[:, :, DT_LANE0:DT_LANE0 + 2 * SSD_HEADS], 1, 2)
        dtt_c = jnp.swapaxes(dt_c[:, :, DT_LANE0:DT_LANE0 + 2 * SSD_HEADS], 1, 2)
        yf_x, yb_x, yf_c, yb_c = _ssd(xbc_x, xbc_c, dt_x, dt_c, dtt_x, dtt_c, prm, l)

        fp = _hyena_filter_params(hy_w1[l], hy_b1[l], hy_freq1[l], hy_w2[l], hy_b2[l], hy_freq2[l], hy_w3[l])
        buf, sumabs = _filt_gen(fp, seq, 1024)
        hspec = _hyena_spectrum(buf.reshape(2, HY_HALVES, FFT_N1, FFT_N1, LANES), sumabs, consts, 2 * seq)
        ohy_x = _hyena_x(v_x, x1_x, x2_x, hspec, hy_bias.reshape(depth, 2, 1, HY_WIDTH), l, consts)

        x = _out_proj(of_x, ob_x, uhg_x, ohy_x, yf_x, yb_x, z_x, hgn, ssn, w_out_b, x, mods, l, None, 512)

        if not last:
            buf_c, sumabs_c = _filt_gen(fp, CTX_LEN, CTX_LEN)
            ohy_c = _hy_ctx(v_c, x1_c, x2_c, buf_c, sumabs_c, cs_ctx, hy_bias[l])
            ctx = _out_proj(of_c, ob_c, uhg_c, ohy_c, yf_c, yb_c, z_c, hgn, ssn, w_out_b, ctx, mods, l, ctx_row,
                            CTX_LEN)
            ctx = _ffn(ctx, mods, l, ctx_row, g2, wg_b, wu_b, cw_f, cb_f, wd_b, CTX_LEN, CTX_LEN, False)

        x = _ffn(x, mods, l, None, g2, wg_b, wu_b, cw_f, cb_f, wd_b, 1024, GRID_W, True,
                 final_g=final_norm_g if last else None)
    return x
```

```python
import functools
import math

import numpy as np
import jax
import jax.numpy as jnp
from jax import lax
from jax.experimental import pallas as pl
from jax.experimental.pallas import tpu as pltpu

F32 = jnp.float32
BF16 = jnp.bfloat16

D_MODEL = 1024
DEPTH = 2
CTX_LEN = 256
GRID_W = 64
EPS = 1e-6

HG_HEADS = 4
HG_DK = 64
HG_WIDTH = 256
HG_CHUNK = 32

HY_WIDTH = 256
HY_EMB_BANDS = 16
HY_HIDDEN = 64
HY_MIN_DECAY = math.log(1e-2) / 1.5
HY_MAX_DECAY = math.log(1e-2) / 0.3

SSD_HEADS = 8
SSD_HEAD_DIM = 64
SSD_WIDTH = 512
SSD_GROUPS = 2
SSD_STATE = 128
SSD_XBC = 1024
SSD_CHUNK = 64

D_FF = 2816
HG_COLS = 5 * HG_WIDTH
HY_COLS = 3 * HY_WIDTH

LANES = 128
HY_HALVES = HY_WIDTH // LANES
TOKEN_BLOCK = 256
SCAN_BATCH_ROWS = 2
FFT_N1 = 128
VMEM_LIMIT = 56 * 1024 * 1024


def _cparams(sem, big=False):
    kw = dict(dimension_semantics=sem)
    if big:
        kw["vmem_limit_bytes"] = VMEM_LIMIT
    return pltpu.CompilerParams(**kw)


def _dot(a, b):
    return lax.dot_general(a, b, (((1,), (0,)), ((), ())), preferred_element_type=F32)


def _dot_nt(a, b):
    return lax.dot_general(a, b, (((1,), (1,)), ((), ())), preferred_element_type=F32)


def _dot_tn(a, b):
    return lax.dot_general(a, b, (((0,), (0,)), ((), ())), preferred_element_type=F32)


def _split3(a):
    hi = a.astype(BF16)
    r = a - hi.astype(F32)
    mid = r.astype(BF16)
    lo = (r - mid.astype(F32)).astype(BF16)
    return hi, mid, lo


def _dot_exact_lhs(mask_bf16, a):
    h, m, l = _split3(a)
    return _dot(mask_bf16, h) + _dot(mask_bf16, m) + _dot(mask_bf16, l)


def _dot_exact_rhs(a, mask_bf16):
    h, m, l = _split3(a)
    return _dot(h, mask_bf16) + _dot(m, mask_bf16) + _dot(l, mask_bf16)


def _dot3(a, b):
    ah, am, _ = _split3(a)
    bh, bm, _ = _split3(b)
    return _dot(ah, bh) + _dot(ah, bm) + _dot(am, bh)


def _dot3_pre(fparts, b):
    fh, fm = fparts
    bh, bm, _ = _split3(b)
    return _dot(fh, bh) + _dot(fh, bm) + _dot(fm, bh)


def _sigmoid(x):
    return 1.0 / (1.0 + jnp.exp(-x))


def _silu(x):
    return x * _sigmoid(x)


def _softplus(x):
    return jnp.maximum(x, 0.0) + jnp.log(1.0 + jnp.exp(-jnp.abs(x)))


def _log_sigmoid(x):
    return jnp.minimum(x, 0.0) - jnp.log(1.0 + jnp.exp(-jnp.abs(x)))


def _ada_body(c_ref, w_ref, b_ref, o_ref):
    cc = c_ref[...]
    o_ref[0] = _dot3(_silu(cc), w_ref[0]) + b_ref[0]


def _ada(cc, w_ada, b_ada):
    tn = 1536
    n = w_ada.shape[-1]
    return pl.pallas_call(
        _ada_body,
        grid=(DEPTH, n // tn),
        in_specs=[
            pl.BlockSpec((8, D_MODEL), lambda l, j: (0, 0)),
            pl.BlockSpec((1, D_MODEL, tn), lambda l, j: (l, 0, j)),
            pl.BlockSpec((1, 1, tn), lambda l, j: (l, 0, j)),
        ],
        out_specs=pl.BlockSpec((1, 8, tn), lambda l, j: (l, 0, j)),
        out_shape=jax.ShapeDtypeStruct((DEPTH, 8, n), F32),
        compiler_params=_cparams(("arbitrary", "arbitrary"), big=True),
        name="adaln",
    )(cc, w_ada, b_ada.reshape(DEPTH, 1, n))


IN_COLS = HG_COLS + HY_COLS + SSD_WIDTH + SSD_XBC + 2 * SSD_HEADS
COL_HY = HG_COLS
COL_Z = COL_HY + HY_COLS
COL_XBC = COL_Z + SSD_WIDTH
COL_DT = COL_XBC + SSD_XBC
IN_COLS_PAD = COL_DT + LANES
DT_LANE0 = 0
CONV_CHUNK = 256

MOD_SHIFT1, MOD_SCALE1, MOD_GATE1, MOD_SHIFT2, MOD_SCALE2, MOD_GATE2 = range(6)


def _layer_spec(a, l):
    nd = a.ndim - 1
    return pl.BlockSpec((None,) + tuple(a.shape[1:]), lambda b, t: (l,) + (0,) * nd)


def _mod_spec(l, ctx_row, idx):
    row = (lambda b: b) if ctx_row is None else (lambda b: ctx_row)
    return pl.BlockSpec((None, None, None, 1, D_MODEL), lambda b, t: (l, row(b), idx, 0, 0))


def _inproj_body(x_ref, hp_ref, hn_ref, sh_ref, sc_ref, g_ref, w_ref, hcw, hcb, scw, scb,
                 uhg_ref, v_ref, x1_ref, x2_ref, xbc_ref, z_ref, dt_ref, *, nt):
    t = pl.program_id(1)

    def normmod(x):
        ms = jnp.mean(x * x, axis=-1, keepdims=True)
        y = x * lax.rsqrt(ms + EPS) * g_ref[...]
        return (y * (1.0 + sc_ref[...]) + sh_ref[...]).astype(BF16)

    yb = normmod(x_ref[0])
    rows = yb.shape[0]
    ye = jnp.concatenate([yb, normmod(jnp.concatenate([hp_ref[0], hn_ref[0]], axis=0))], axis=0)
    def plain(ref, c0, c1, o0):
        def run():
            ref[0, :, o0:o0 + (c1 - c0)] = _dot(yb, w_ref[:, c0:c1])
        return run

    plain_tasks = [plain(uhg_ref, j, min(j + 512, HG_COLS), j) for j in range(0, HG_COLS, 512)]
    plain_tasks += [plain(z_ref, COL_Z, COL_Z + SSD_WIDTH, 0), plain(dt_ref, COL_DT, COL_DT + LANES, 0)]

    ri = lax.broadcasted_iota(jnp.int32, (rows, CONV_CHUNK), 0)

    def conv_chunk(col0, cw_ref, cb_ref, k):
        cols = slice(col0 + k * CONV_CHUNK, col0 + (k + 1) * CONV_CHUNK)
        ccols = slice(k * CONV_CHUNK, (k + 1) * CONV_CHUNK)
        ue = _dot(ye, w_ref[:, cols])
        if plain_tasks:
            plain_tasks.pop(0)()
        u = ue[:rows]
        prev_row = jnp.where(t > 0, ue[rows + 7:rows + 8, :], 0.0)
        next_row = jnp.where(t < nt - 1, ue[rows + 8:rows + 9, :], 0.0)
        dn = jnp.where(ri == 0, prev_row, pltpu.roll(u, 1, 0))
        up = jnp.where(ri == rows - 1, next_row, pltpu.roll(u, rows - 1, 0))
        return dn * cw_ref[0:1, ccols] + u * cw_ref[1:2, ccols] + up * cw_ref[2:3, ccols] + cb_ref[:, ccols]

    for k, ref in enumerate((v_ref, x1_ref, x2_ref)):
        cv = conv_chunk(COL_HY, hcw, hcb, k)
        for hf in range(HY_HALVES):
            ref[hf, 0] = cv[:, hf * LANES:(hf + 1) * LANES]
    for k in range(SSD_XBC // CONV_CHUNK):
        cv = conv_chunk(COL_XBC, scw, scb, k)
        xbc_ref[0, :, k * CONV_CHUNK:(k + 1) * CONV_CHUNK] = _silu(cv).astype(xbc_ref.dtype)
    for task in plain_tasks:
        task()


def _inproj(x, mods, l, ctx_row, g, w, hcw, hcb, scw, scb, tm, name):
    bsz, seq, d = x.shape
    nt = seq // tm
    hb = tm // 8
    tok = lambda n: pl.BlockSpec((1, tm, n), lambda b, t: (b, t, 0))
    hy_spec = pl.BlockSpec((HY_HALVES, 1, tm, LANES), lambda b, t: (0, b, t, 0))
    hy_shape = jax.ShapeDtypeStruct((HY_HALVES, bsz, seq, LANES), F32)
    tshape = lambda n: jax.ShapeDtypeStruct((bsz, seq, n), F32)
    return pl.pallas_call(
        functools.partial(_inproj_body, nt=nt),
        grid=(bsz, nt),
        in_specs=[
            tok(d),
            pl.BlockSpec((1, 8, d), lambda b, t: (b, jnp.maximum(t * hb - 1, 0), 0)),
            pl.BlockSpec((1, 8, d), lambda b, t: (b, jnp.minimum((t + 1) * hb, nt * hb - 1), 0)),
            _mod_spec(l, ctx_row, MOD_SHIFT1), _mod_spec(l, ctx_row, MOD_SCALE1),
            _layer_spec(g, l), _layer_spec(w, l),
            _layer_spec(hcw, l), _layer_spec(hcb, l), _layer_spec(scw, l), _layer_spec(scb, l),
        ],
        out_specs=[tok(HG_COLS), hy_spec, hy_spec, hy_spec, tok(SSD_XBC), tok(SSD_WIDTH), tok(LANES)],
        out_shape=[tshape(HG_COLS), hy_shape, hy_shape, hy_shape,
                   jax.ShapeDtypeStruct((bsz, seq, SSD_XBC), BF16), tshape(SSD_WIDTH), tshape(LANES)],
        compiler_params=_cparams(("parallel", "arbitrary"), big=True),
        name=name,
    )(x, x, x, mods, mods, g, w, hcw, hcb, scw, scb)


def _gla_dir(q, a, v, p, r1, st_ref, reverse):
    tb, ch = TOKEN_BLOCK, HG_CHUNK
    nch = tb // ch
    qv = r1 + _log_sigmoid(a)
    logf = jnp.maximum(p, qv) + jnp.log(1.0 + jnp.exp(-jnp.abs(p - qv)))
    k = 1.0 - jnp.exp(logf)

    ri = lax.broadcasted_iota(jnp.int32, (tb, tb), 0)
    ci = lax.broadcasted_iota(jnp.int32, (tb, tb), 1)
    same = (ri // ch) == (ci // ch)
    tri = (ci >= ri) if reverse else (ci <= ri)
    mask = jnp.logical_and(same, tri)
    mask_b = jnp.where(mask, 1.0, 0.0).astype(BF16)
    bdiag = (ri // HG_DK) == (ci // HG_DK)

    lh, lm, ll = _split3(logf)
    bcs3 = _dot(mask_b, jnp.concatenate([lh, lm, ll], axis=1))
    bcs = bcs3[:, :tb] + bcs3[:, tb:2 * tb] + bcs3[:, 2 * tb:]
    yield None
    b3 = bcs.reshape(nch, ch, tb)
    mid = ch // 2 if reverse else ch // 2 - 1
    end = 0 if reverse else ch - 1
    b_mid = b3[:, mid:mid + 1, :]
    b_end = b3[:, end:end + 1, :]
    q3 = q.reshape(nch, ch, tb)
    k3 = k.reshape(nch, ch, tb)
    d1 = b3 - b_mid
    qd = (q3 * jnp.exp(d1)).reshape(tb, tb)
    kd = (k3 * jnp.exp(-d1)).reshape(tb, tb).astype(BF16)
    kup = (k3 * jnp.exp(b_end - b3)).astype(BF16)
    qb = (q3 * jnp.exp(b3)).astype(BF16)
    dec = jnp.exp(b_end)
    yield None

    lane_head = lax.broadcasted_iota(jnp.int32, (1, tb), 1) // HG_DK
    acc = jnp.zeros((tb, tb), F32)
    for h in range(HG_HEADS):
        sel = lane_head == h
        qh = jnp.where(sel, qd, 0.0).astype(BF16)
        s = _dot_nt(qh, kd)
        s = jnp.where(mask, s, 0.0).astype(BF16)
        vh = jnp.where(sel, v, 0.0).astype(BF16)
        acc = acc + _dot(s, vh)
        yield None

    v3 = v.astype(BF16).reshape(nch, ch, tb)
    inter = [None] * nch
    order = range(nch - 1, -1, -1) if reverse else range(nch)
    st = st_ref[...]
    for c in order:
        inter[c] = _dot_nt(qb[c], st.astype(BF16))
        upd = _dot_tn(v3[c], kup[c])
        st = st * dec[c] + jnp.where(bdiag, upd, 0.0)
        yield None
    st_ref[...] = st
    yield acc + jnp.concatenate(inter, axis=0)


def _interleave(gens, lag=3):
    results = [None] * len(gens)
    done = [False] * len(gens)
    step = 0
    while not all(done):
        for i, g in enumerate(gens):
            if done[i] or step < i * lag:
                continue
            try:
                results[i] = next(g)
            except StopIteration:
                done[i] = True
        step += 1
    return results


def _gla_body(qf, ff, vf, qb, fb, vb, qc, ffc, fbc, vc, lb_ref, of_x, ob_x, of_c, ob_c, stf, stb, *, layer):
    t = pl.program_id(1)
    is_ctx = t == 0

    @pl.when(is_ctx)
    def _():
        stf[...] = jnp.zeros_like(stf)
        stb[...] = jnp.zeros_like(stb)

    lg = lb_ref[...]
    e = jnp.exp(lg - jnp.max(lg, axis=0, keepdims=True))
    den = jnp.sum(e, axis=0, keepdims=True)
    num = jnp.zeros_like(den)
    for r in range(1, layer + 1):
        num = num + e[r:r + 1, :]
    lb = num / den
    p = jnp.log(lb)
    r1 = jnp.log(1.0 - lb)

    scale = HG_DK ** -0.5
    nb = qf.shape[0]
    gens = []
    for b in range(nb):
        gens.append(_gla_dir(jnp.where(is_ctx, qc[b], qf[b]) * scale, jnp.where(is_ctx, ffc[b], ff[b]),
                             jnp.where(is_ctx, vc[b], vf[b]), p, r1, stf.at[b], False))
        gens.append(_gla_dir(jnp.where(is_ctx, qc[b], qb[b]) * scale, jnp.where(is_ctx, fbc[b], fb[b]),
                             jnp.where(is_ctx, vc[b], vb[b]), p, r1, stb.at[b], True))
    outs = _interleave(gens, lag=1)

    @pl.when(is_ctx)
    def _():
        for b in range(nb):
            of_c[b] = outs[2 * b].astype(of_c.dtype)
            ob_c[b] = outs[2 * b + 1].astype(ob_c.dtype)

    @pl.when(jnp.logical_not(is_ctx))
    def _():
        for b in range(nb):
            of_x[b] = outs[2 * b].astype(of_x.dtype)
            ob_x[b] = outs[2 * b + 1].astype(ob_x.dtype)


def _scan_block_maps(nxb):
    fwd = lambda t: jnp.maximum(t - 1, 0)
    bwd = lambda t: jnp.where(t == 0, nxb - 1, nxb - t)
    return fwd, bwd


def _gla(u_x, u_c, lb_logits, layer):
    bsz, seq, _ = u_x.shape
    tb = TOKEN_BLOCK
    nxb = seq // tb
    xf, xb = _scan_block_maps(nxb)

    nb = SCAN_BATCH_ROWS

    def xs(blk, col):
        return pl.BlockSpec((nb, tb, HG_WIDTH), lambda b, t: (b, blk(t), col))

    def cs(col):
        return pl.BlockSpec((nb, tb, HG_WIDTH), lambda b, t: (b, 0, col))

    out_x = jax.ShapeDtypeStruct((bsz, seq, HG_WIDTH), BF16)
    out_c = jax.ShapeDtypeStruct((bsz, CTX_LEN, HG_WIDTH), BF16)
    state = pltpu.VMEM((nb, HG_WIDTH, HG_WIDTH), F32)
    return pl.pallas_call(
        functools.partial(_gla_body, layer=layer),
        grid=(bsz // nb, nxb + 1),
        in_specs=[xs(xf, 0), xs(xf, 1), xs(xf, 3), xs(xb, 0), xs(xb, 2), xs(xb, 3),
                  cs(0), cs(1), cs(2), cs(3),
                  pl.BlockSpec((DEPTH, HG_WIDTH), lambda b, t: (0, 0))],
        out_specs=[xs(xf, 0), xs(xb, 0), cs(0), cs(0)],
        out_shape=[out_x, out_x, out_c, out_c],
        scratch_shapes=[state, state],
        compiler_params=_cparams(("parallel", "arbitrary"), big=True),
        name="hgrn2_scan",
    )(u_x, u_x, u_x, u_x, u_x, u_x, u_c, u_c, u_c, u_c, lb_logits)


def _ssd_dir(xbc, dt_raw, dtt_all, dtb_e, alog_e, dtb_t, alog_t, dsk, st_ref, reverse):
    tb, ch = TOKEN_BLOCK, SSD_CHUNK
    nch = tb // ch
    d = 1 if reverse else 0
    xs = xbc[:, :SSD_WIDTH].astype(F32)
    bm = xbc[:, SSD_WIDTH:SSD_WIDTH + 256].astype(BF16)
    cm = xbc[:, SSD_WIDTH + 256:].astype(BF16)

    ri = lax.broadcasted_iota(jnp.int32, (tb, tb), 0)
    ci = lax.broadcasted_iota(jnp.int32, (tb, tb), 1)
    same = (ri // ch) == (ci // ch)
    mask = jnp.logical_and(same, (ci >= ri) if reverse else (ci <= ri))
    mask_b = jnp.where(mask, 1.0, 0.0).astype(BF16)
    mask_t = jnp.logical_and(same, (ri >= ci) if reverse else (ri <= ci))
    mask_tb = jnp.where(mask_t, 1.0, 0.0).astype(BF16)

    ej = lax.broadcasted_iota(jnp.int32, (128, SSD_WIDTH), 0)
    el = lax.broadcasted_iota(jnp.int32, (128, SSD_WIDTH), 1)
    expand = jnp.where(ej == DT_LANE0 + d * SSD_HEADS + el // SSD_HEAD_DIM, 1.0, 0.0).astype(BF16)
    dh, dm, dl = _split3(dt_raw)
    dte_raw = _dot(jnp.concatenate([dh, dm, dl], axis=1), jnp.concatenate([expand] * 3, axis=0))
    dte = _softplus(dte_raw + dtb_e[d:d + 1, :])
    a_e = -jnp.exp(alog_e[d:d + 1, :])
    ah, am, al = _split3(dte * a_e)
    acs3p = _dot(mask_b, jnp.concatenate([ah, am, al], axis=1))
    acs = acs3p[:, :SSD_WIDTH] + acs3p[:, SSD_WIDTH:2 * SSD_WIDTH] + acs3p[:, 2 * SSD_WIDTH:]
    yield None
    xdt = xs * dte
    acs3 = acs.reshape(nch, ch, SSD_WIDTH)
    end = 0 if reverse else ch - 1
    a_end = acs3[:, end:end + 1, :]
    xw = (xdt.reshape(nch, ch, SSD_WIDTH) * jnp.exp(a_end - acs3)).astype(BF16).reshape(tb, SSD_WIDTH)
    ea = jnp.exp(acs)
    dec = jnp.exp(a_end)

    dtt_raw = dtt_all[d * SSD_HEADS:(d + 1) * SSD_HEADS, :]
    dtt = _softplus(dtt_raw + dtb_t[d * SSD_HEADS:(d + 1) * SSD_HEADS, :])
    a_t = -jnp.exp(alog_t[d * SSD_HEADS:(d + 1) * SSD_HEADS, :])
    th, tm_, tl = _split3(dtt * a_t)
    acs_t3 = _dot(jnp.concatenate([th, tm_, tl], axis=0), mask_tb)
    acs_t = acs_t3[:SSD_HEADS] + acs_t3[SSD_HEADS:2 * SSD_HEADS] + acs_t3[2 * SSD_HEADS:]
    yield None

    def chunk_diag(m):
        z = jnp.zeros((ch, m.shape[1]), m.dtype)
        return jnp.concatenate(
            [jnp.concatenate([m[c * ch:(c + 1) * ch] if k == c else z for k in range(nch)], axis=1)
             for c in range(nch)], axis=0)

    xdt_b = xdt.astype(BF16)
    hd = SSD_HEAD_DIM
    hpg = SSD_HEADS // SSD_GROUPS
    gw = SSD_WIDTH // SSD_GROUPS
    rr = lax.broadcasted_iota(jnp.int32, (tb, gw), 0)
    ll = lax.broadcasted_iota(jnp.int32, (tb, gw), 1)
    tri4 = ((ll % hd) >= (rr % ch)) if reverse else ((ll % hd) <= (rr % ch))
    head_diag = (rr // hd) == (ll // hd)
    order = range(nch - 1, -1, -1) if reverse else range(nch)
    outs = []
    for g in range(SSD_GROUPS):
        st_cols = slice(g * SSD_STATE, (g + 1) * SSD_STATE)
        cols = slice(g * gw, (g + 1) * gw)
        cm_d = chunk_diag(cm[:, st_cols])
        bm_cat = jnp.concatenate([bm[c * ch:(c + 1) * ch, st_cols] for c in range(nch)], axis=1)
        cb4 = _dot_nt(cm_d, jnp.concatenate([bm_cat] * hpg, axis=0))
        rowm = jnp.concatenate(
            [jnp.broadcast_to(jnp.concatenate([acs_t[g * hpg + h:g * hpg + h + 1, c * ch:(c + 1) * ch]
                                               for h in range(hpg)], axis=1), (ch, gw))
             for c in range(nch)], axis=0)
        w4 = (cb4 * jnp.exp(jnp.where(tri4, acs[:, cols] - rowm, -1e30))).astype(BF16)
        xg = xdt_b[:, cols]
        zero = jnp.zeros((hpg * ch, gw), BF16)
        x4 = jnp.concatenate(
            [jnp.where(head_diag, jnp.concatenate([xg[c * ch:(c + 1) * ch]] * hpg, axis=0), zero)
             for c in range(nch)], axis=0)
        y_intra = _dot(chunk_diag(w4), x4)
        yield None

        upd = _dot_tn(bm[:, st_cols], chunk_diag(xw[:, cols]))
        st = st_ref[g]
        entering = [None] * nch
        for c in order:
            entering[c] = st.astype(BF16)
            st = st * dec[c][:, cols] + upd[:, c * gw:(c + 1) * gw]
        st_ref[g] = st
        y_inter = _dot(cm_d, jnp.concatenate(entering, axis=0)) * ea[:, cols]
        outs.append(y_intra + y_inter)
        yield None
    o = jnp.concatenate(outs, axis=1)
    if not reverse:
        o = o + dsk[...] * xs
    yield o


def _ssd_body(xf, dtf, dttf, xb, dtb, dttb, xc, dtc, dttc, dtb_e, alog_e, dtb_t, alog_t, dsk,
              of_x, ob_x, of_c, ob_c, stf, stb):
    t = pl.program_id(1)
    is_ctx = t == 0

    @pl.when(is_ctx)
    def _():
        stf[...] = jnp.zeros_like(stf)
        stb[...] = jnp.zeros_like(stb)

    prm = (dtb_e, alog_e, dtb_t, alog_t, dsk)
    nb = xf.shape[0]
    gens = []
    for b in range(nb):
        gens.append(_ssd_dir(jnp.where(is_ctx, xc[b], xf[b]), jnp.where(is_ctx, dtc[b], dtf[b]),
                             jnp.where(is_ctx, dttc[b], dttf[b]), *prm, stf.at[b], False))
        gens.append(_ssd_dir(jnp.where(is_ctx, xc[b], xb[b]), jnp.where(is_ctx, dtc[b], dtb[b]),
                             jnp.where(is_ctx, dttc[b], dttb[b]), *prm, stb.at[b], True))
    outs = _interleave(gens, lag=2)

    @pl.when(is_ctx)
    def _():
        for b in range(nb):
            of_c[b] = outs[2 * b].astype(of_c.dtype)
            ob_c[b] = outs[2 * b + 1].astype(ob_c.dtype)

    @pl.when(jnp.logical_not(is_ctx))
    def _():
        for b in range(nb):
            of_x[b] = outs[2 * b].astype(of_x.dtype)
            ob_x[b] = outs[2 * b + 1].astype(ob_x.dtype)


def _ssd(xbc_x, xbc_c, dt_x, dt_c, dtt_x, dtt_c, prm, l):
    bsz, seq, _ = xbc_x.shape
    tb = TOKEN_BLOCK
    nxb = seq // tb
    xf, xb = _scan_block_maps(nxb)
    full = lambda a: _layer_spec(a, l)
    params = [prm["dtb_e"], prm["alog_e"], prm["dtb_t"], prm["alog_t"], prm["dsk"]]

    nb = SCAN_BATCH_ROWS

    def xspecs(blk):
        return [pl.BlockSpec((nb, tb, SSD_XBC), lambda b, t: (b, blk(t), 0)),
                pl.BlockSpec((nb, tb, 128), lambda b, t: (b, blk(t), 0)),
                pl.BlockSpec((nb, 16, tb), lambda b, t: (b, 0, blk(t)))]

    zero = lambda t: 0
    yspec = lambda blk: pl.BlockSpec((nb, tb, SSD_WIDTH), lambda b, t: (b, blk(t), 0))
    out_x = jax.ShapeDtypeStruct((bsz, seq, SSD_WIDTH), BF16)
    out_c = jax.ShapeDtypeStruct((bsz, CTX_LEN, SSD_WIDTH), BF16)
    state = pltpu.VMEM((nb, SSD_GROUPS, SSD_STATE, SSD_WIDTH // SSD_GROUPS), F32)
    return pl.pallas_call(
        _ssd_body,
        grid=(bsz // nb, nxb + 1),
        in_specs=xspecs(xf) + xspecs(xb) + xspecs(zero) + [full(a) for a in params],
        out_specs=[yspec(xf), yspec(xb), yspec(zero), yspec(zero)],
        out_shape=[out_x, out_x, out_c, out_c],
        scratch_shapes=[state, state],
        compiler_params=_cparams(("parallel", "arbitrary"), big=True),
        name="ssd_scan",
    )(xbc_x, dt_x, dtt_x, xbc_x, dt_x, dtt_x, xbc_c, dt_c, dtt_c, *params)


def _filt_body(w1, b1, f1, w2, b2, f2, w3, delta, buf_ref, sa_ref, *, seq_len, tr):
    i = pl.program_id(0)
    hr = tr // 2
    lane = lax.broadcasted_iota(jnp.int32, (hr, 128), 1)
    grp, sub = lane // HY_HIDDEN, lane % HY_HIDDEN
    jrow = i * tr + grp * hr + lax.broadcasted_iota(jnp.int32, (hr, 128), 0)
    pos = jnp.where(jrow < seq_len, jrow, 2 * seq_len - jrow).astype(F32)
    tpos = pos / (seq_len - 1.0)
    wpos = pos * (2.0 * math.pi / seq_len)
    bidx = jnp.where(sub <= HY_EMB_BANDS, sub - 1, sub - 1 - HY_EMB_BANDS).astype(F32)
    band = 1e-4 + bidx * ((HY_EMB_BANDS - 1 - 1e-4) / (HY_EMB_BANDS - 1))
    ang = band * wpos + jnp.where(sub > HY_EMB_BANDS, 0.5 * math.pi, 0.0)
    z = jnp.where(sub == 0, tpos, jnp.where(sub <= 2 * HY_EMB_BANDS, jnp.cos(ang), 0.0))
    h = jnp.sin(f1[...] * (_dot3(z, w1[...]) + b1[...]))
    h = jnp.sin(f2[...] * (_dot3(h, w2[...]) + b2[...]))

    @pl.when(i == 0)
    def _():
        sa_ref[...] = jnp.zeros_like(sa_ref)

    for g in range(2):
        o = _dot3(h, w3[0, g])
        o = o * jnp.exp(-tpos[:, g * HY_HIDDEN:g * HY_HIDDEN + 1] * delta[...])
        o = jnp.where(jrow[:, g * HY_HIDDEN:g * HY_HIDDEN + 1] == seq_len, 0.0, o)
        for order in range(2):
            for hf in range(HY_HALVES):
                lo = order * HY_WIDTH + hf * LANES
                buf_ref[order, hf, g * hr:(g + 1) * hr, :] = o[:, lo:lo + LANES]
        sa_ref[...] = sa_ref[...] + jnp.sum(jnp.abs(o), axis=0, keepdims=True)


def _filt_gen(p, seq_len, tr):
    n = 2 * seq_len
    assert seq_len % tr == 0
    half = seq_len // tr
    full = lambda a: pl.BlockSpec(a.shape, lambda i: (0,) * a.ndim)
    small = [p["w1"], p["b1"], p["f1"], p["w2"], p["b2"], p["f2"]]
    return pl.pallas_call(
        functools.partial(_filt_body, seq_len=seq_len, tr=tr),
        grid=(n // tr,),
        in_specs=[full(a) for a in small] + [
            pl.BlockSpec((1, 2, 2 * HY_HIDDEN, 2 * HY_WIDTH), lambda i: (i // half, 0, 0, 0)),
            full(p["delta"]),
        ],
        out_specs=[pl.BlockSpec((2, HY_HALVES, tr, LANES), lambda i: (0, 0, i, 0)),
                   pl.BlockSpec((8, 2 * HY_WIDTH), lambda i: (0, 0))],
        out_shape=[jax.ShapeDtypeStruct((2, HY_HALVES, n, LANES), F32),
                   jax.ShapeDtypeStruct((8, 2 * HY_WIDTH), F32)],
        compiler_params=_cparams(("arbitrary",)),
        name="hyena_filter",
    )(*small, p["w3"], p["delta"])


FFT_NB = 8


def _strided_rows(ref3, j, n):
    flat = ref3.reshape(ref3.shape[0] * FFT_NB, LANES)
    return flat[pl.ds(j, n, stride=FFT_NB), :]


def _store_strided_rows(ref3, j, val):
    flat = ref3.reshape(ref3.shape[0] * FFT_NB, LANES)
    flat[pl.ds(j, val.shape[0], stride=FFT_NB), :] = val


def _load_complex(ref, k):
    return jnp.concatenate(
        [jnp.concatenate([ref[hf, 0, k], ref[hf, 1, k]], axis=0) for hf in range(HY_HALVES)], axis=1)


def _store_complex(ref, k, val):
    for hf in range(HY_HALVES):
        ref[hf, 0, k] = val[:FFT_N1, hf * LANES:(hf + 1) * LANES].astype(ref.dtype)
        ref[hf, 1, k] = val[FFT_N1:, hf * LANES:(hf + 1) * LANES].astype(ref.dtype)


FFT_TILES = FFT_N1 // FFT_NB
TILE_ROWS = 2 * FFT_N1 * FFT_NB


def _spec_stage1(x_refs, f_ref, t_ref, scr, j, rows):
    for i in range(FFT_NB):
        x = jnp.concatenate([_strided_rows(r, i, rows) for r in x_refs], axis=1)
        a = _dot3_pre((f_ref[0], f_ref[1]), x)
        tr = t_ref[0, 0, :, i:i + 1]
        ti = t_ref[0, 1, :, i:i + 1]
        ar, ai = a[:FFT_N1], a[FFT_N1:]
        out = jnp.concatenate([ar * tr - ai * ti, ar * ti + ai * tr], axis=0)
        for hf in range(HY_HALVES):
            scr[hf, pl.ds(j * TILE_ROWS + i, 2 * FFT_N1, stride=FFT_NB), :] = out[:, hf * LANES:(hf + 1) * LANES]


def _spec_slab(scr, k1):
    def part(hf, p):
        r0 = (p * FFT_N1 + k1) * FFT_NB
        return jnp.concatenate(
            [scr[hf, pl.ds(pl.multiple_of(jj * TILE_ROWS + r0, FFT_NB), FFT_NB), :] for jj in range(FFT_TILES)], axis=0)
    return jnp.concatenate(
        [jnp.concatenate([part(hf, 0), part(hf, 1)], axis=0) for hf in range(HY_HALVES)], axis=1)


def _spec_slab_store(scr, k1, val):
    for hf in range(HY_HALVES):
        for p in range(2):
            r0 = (p * FFT_N1 + k1) * FFT_NB
            for jj in range(FFT_TILES):
                scr[hf, pl.ds(pl.multiple_of(jj * TILE_ROWS + r0, FFT_NB), FFT_NB), :] = val[
                    p * FFT_N1 + jj * FFT_NB:p * FFT_N1 + (jj + 1) * FFT_NB, hf * LANES:(hf + 1) * LANES]


def _hconv_body(y_ref, h_ref, gate_ref, f1_ref, f2_ref, f2c_ref, f3_ref, t_ref, bias_ref, o_ref, scr, *, kb):
    s = pl.program_id(0)
    rows = y_ref.shape[1]

    @pl.when(s < FFT_TILES)
    def _():
        _spec_stage1([y_ref.at[hf] for hf in range(HY_HALVES)], f1_ref, t_ref, scr, s, rows)

    @pl.when(jnp.logical_and(s >= FFT_TILES, s < 2 * FFT_TILES))
    def _():
        kt = s - FFT_TILES
        for k in range(kb):
            k1 = kt * kb + k
            x = _dot3_pre((f2_ref[0], f2_ref[1]), _spec_slab(scr, k1))
            xr, xi = x[:FFT_N1], x[FFT_N1:]
            h = _load_complex(h_ref, k).astype(F32)
            hr, hi = h[:FFT_N1], h[FFT_N1:]
            y = jnp.concatenate([xr * hr - xi * hi, xr * hi + xi * hr], axis=0)
            _spec_slab_store(scr, k1, _dot3_pre((f2c_ref[0], f2c_ref[1]), y))

    @pl.when(s >= 2 * FFT_TILES)
    def _():
        j = s - 2 * FFT_TILES
        for i in range(FFT_NB):
            tr = t_ref[0, 0, :, i:i + 1]
            ti = t_ref[0, 1, :, i:i + 1]
            c = jnp.concatenate(
                [scr[hf, pl.ds(j * TILE_ROWS + i, 2 * FFT_N1, stride=FFT_NB), :] for hf in range(HY_HALVES)], axis=1)
            cr, ci = c[:FFT_N1], c[FFT_N1:]
            dmat = jnp.concatenate([cr * tr + ci * ti, ci * tr - cr * ti], axis=0)
            w = _dot3_pre((f3_ref[0], f3_ref[1]), dmat)
            for hf in range(HY_HALVES):
                lanes = slice(hf * LANES, (hf + 1) * LANES)
                g = _strided_rows(gate_ref.at[hf], i, rows)
                v = _strided_rows(y_ref.at[hf], i, rows)
                _store_strided_rows(o_ref.at[hf], i, g * (w[:, lanes] + v * bias_ref[:, lanes]))


def _hyena_conv(y, gate, hspec, bias, l, order, consts, kb=8):
    nh, rows, n2, lanes = y.shape
    nt = FFT_TILES
    tile = lambda f: pl.BlockSpec((nh, rows, FFT_NB, lanes), lambda s: (0, 0, f(s), 0))
    in_tile = lambda s: jnp.where(s < nt, s, jnp.maximum(s - 2 * nt, 0))
    out_tile = lambda s: jnp.maximum(s - 2 * nt, 0)
    const = lambda a: pl.BlockSpec(a.shape, lambda s: (0,) * a.ndim)
    return pl.pallas_call(
        functools.partial(_hconv_body, kb=kb),
        grid=(3 * nt,),
        in_specs=[
            tile(in_tile),
            pl.BlockSpec((None, nh, 2, kb, FFT_N1, lanes),
                         lambda s: (order, 0, 0, jnp.clip(s - nt, 0, FFT_N1 // kb - 1), 0, 0)),
            tile(out_tile),
            const(consts["f1_data"]), const(consts["f2"]), const(consts["f2c"]), const(consts["f3"]),
            pl.BlockSpec((1, 2, FFT_N1, FFT_NB), lambda s: (in_tile(s), 0, 0, 0)),
            pl.BlockSpec((None, None, 1, HY_WIDTH), lambda s: (l, order, 0, 0)),
        ],
        out_specs=tile(out_tile),
        out_shape=jax.ShapeDtypeStruct(y.shape, F32),
        scratch_shapes=[pltpu.VMEM((nh, nt * TILE_ROWS, lanes), F32)],
        compiler_params=_cparams(("arbitrary",), big=True),
        name="hyena_conv",
    )(y, hspec, gate, consts["f1_data"], consts["f2"], consts["f2c"], consts["f3"], consts["twid"], bias)


def _hspec_body(b_ref, sa_ref, f1_ref, f2_ref, t_ref, o_ref, scr, *, kb, n):
    o = pl.program_id(0)
    s = pl.program_id(1)
    rows = b_ref.shape[1]

    @pl.when(s < FFT_TILES)
    def _():
        _spec_stage1([b_ref.at[hf] for hf in range(HY_HALVES)], f1_ref, t_ref, scr, s, rows)

    @pl.when(s >= FFT_TILES)
    def _():
        sa = jnp.where(o == 0, sa_ref[0:1, :HY_WIDTH], sa_ref[0:1, HY_WIDTH:])
        scale = 1.0 / (sa * float(n))
        kt = s - FFT_TILES
        for k in range(kb):
            x = _dot3_pre((f2_ref[0], f2_ref[1]), _spec_slab(scr, kt * kb + k))
            _store_complex(o_ref, k, x * scale)


def _hyena_spectrum(buf, sumabs, consts, n, kb=8):
    _, nh, rows, n2, lanes = buf.shape
    nt = FFT_TILES
    const = lambda a: pl.BlockSpec(a.shape, lambda o, s: (0,) * a.ndim)
    return pl.pallas_call(
        functools.partial(_hspec_body, kb=kb, n=n),
        grid=(2, 2 * nt),
        in_specs=[
            pl.BlockSpec((None, nh, rows, FFT_NB, lanes), lambda o, s: (o, 0, 0, jnp.minimum(s, nt - 1), 0)),
            const(sumabs), const(consts["f1_real"]), const(consts["f2"]),
            pl.BlockSpec((1, 2, FFT_N1, FFT_NB), lambda o, s: (jnp.minimum(s, nt - 1), 0, 0, 0)),
        ],
        out_specs=pl.BlockSpec((None, nh, 2, kb, FFT_N1, lanes), lambda o, s: (o, 0, 0, jnp.maximum(s - nt, 0), 0, 0)),
        out_shape=jax.ShapeDtypeStruct((2, nh, 2, FFT_N1, FFT_N1, lanes), BF16),
        scratch_shapes=[pltpu.VMEM((nh, nt * TILE_ROWS, lanes), F32)],
        compiler_params=_cparams(("arbitrary", "arbitrary"), big=True),
        name="hyena_spectrum",
    )(buf, sumabs, consts["f1_real"], consts["f2"], consts["twid"])


def _hyc_body(v_ref, x1_ref, x2_ref, buf_ref, sa_ref, cs_ref, bias_ref, o_ref):
    n = 2 * CTX_LEN
    cmat, smat = cs_ref[0], cs_ref[1]
    c_in, s_in = cmat[:, :CTX_LEN], smat[:, :CTX_LEN]
    c_out, s_out = cmat[:CTX_LEN, :], smat[:CTX_LEN, :]
    batch = lambda ref, b: jnp.concatenate([ref[hf, b] for hf in range(HY_HALVES)], axis=1)
    yr, yi = batch(v_ref, 0), batch(v_ref, 1)
    for o, gate in enumerate((x1_ref, x2_ref)):
        bufo = jnp.concatenate([buf_ref[o, hf] for hf in range(HY_HALVES)], axis=1)
        scale = 1.0 / (sa_ref[0:1, o * HY_WIDTH:(o + 1) * HY_WIDTH] * float(n))
        hr = _dot3(cmat, bufo) * scale
        hi = -_dot3(smat, bufo) * scale
        xr = _dot3(c_in, yr) + _dot3(s_in, yi)
        xi = _dot3(c_in, yi) - _dot3(s_in, yr)
        zr = xr * hr - xi * hi
        zi = xr * hi + xi * hr
        wr = _dot3(c_out, zr) - _dot3(s_out, zi)
        wi = _dot3(c_out, zi) + _dot3(s_out, zr)
        b = bias_ref[o:o + 1, :]
        yr = batch(gate, 0) * (wr + yr * b)
        yi = batch(gate, 1) * (wi + yi * b)
    for hf in range(HY_HALVES):
        o_ref[hf, 0] = yr[:, hf * LANES:(hf + 1) * LANES]
        o_ref[hf, 1] = yi[:, hf * LANES:(hf + 1) * LANES]


def _hy_ctx(v, x1, x2, buf, sumabs, cs, bias):
    return pl.pallas_call(
        _hyc_body,
        out_shape=jax.ShapeDtypeStruct(v.shape, F32),
        compiler_params=pltpu.CompilerParams(vmem_limit_bytes=VMEM_LIMIT),
        name="hyena_ctx",
    )(v, x1, x2, buf, sumabs, cs, bias)


def _dft_constants(seq_len):
    n = 2 * seq_len
    n1 = FFT_N1
    assert n == n1 * n1
    idx = np.arange(n1, dtype=np.float64)
    th = 2.0 * np.pi * np.outer(idx, idx) / n1
    fr, fi = np.cos(th), -np.sin(th)
    half = n1 // 2

    def parts(m):
        m32 = jnp.asarray(m, F32)
        hi = m32.astype(BF16)
        mid = (m32 - hi.astype(F32)).astype(BF16)
        return jnp.stack([hi, mid])

    f1_data = np.block([[fr[:, :half], -fi[:, :half]], [fi[:, :half], fr[:, :half]]])
    f1_real = np.concatenate([fr, fi], axis=0)
    f2 = np.block([[fr, -fi], [fi, fr]])
    f2c = np.block([[fr, fi], [-fi, fr]])
    f3 = np.block([[fr[:half], fi[:half]], [-fi[:half], fr[:half]]])
    tw = 2.0 * np.pi * np.outer(idx, idx) / n
    twid = np.stack([np.cos(tw), -np.sin(tw)])
    twid = twid.reshape(2, n1, n1 // FFT_NB, FFT_NB).transpose(2, 0, 1, 3)
    return dict(f1_data=parts(f1_data), f1_real=parts(f1_real), f2=parts(f2), f2c=parts(f2c),
                f3=parts(f3), twid=jnp.asarray(twid, F32))


def _ctx_dft_constants():
    n = 2 * CTX_LEN
    idx = np.arange(n, dtype=np.float64)
    th = 2.0 * np.pi * np.outer(idx, idx) / n
    return jnp.asarray(np.stack([np.cos(th), np.sin(th)]), F32)


def _hyena_filter_params(hy_w1, hy_b1, hy_freq1, hy_w2, hy_b2, hy_freq2, hy_w3):
    hid = HY_HIDDEN
    zeros = jnp.zeros((hid, hid), F32)
    w1 = jnp.zeros((hid, hid), F32).at[:hy_w1.shape[0]].set(hy_w1)
    w1 = jnp.block([[w1, zeros], [zeros, w1]])
    w2 = jnp.block([[hy_w2, zeros], [zeros, hy_w2]])
    w3 = hy_w3.reshape(hid, 2, 2 * HY_WIDTH).transpose(1, 0, 2)
    z3 = jnp.zeros_like(w3)
    w3 = jnp.stack([jnp.concatenate([w3, z3], axis=1), jnp.concatenate([z3, w3], axis=1)], axis=1)
    deltas = np.abs(np.linspace(HY_MIN_DECAY, HY_MAX_DECAY, HY_WIDTH))
    delta = jnp.asarray(np.tile(deltas, 2)[None, :], F32)
    row = lambda a: jnp.tile(a, 2).reshape(1, -1)
    return dict(w1=w1, b1=row(hy_b1), f1=row(hy_freq1), w2=w2, b2=row(hy_b2), f2=row(hy_freq2),
                w3=w3, delta=delta)


def _hyena_x(v, x1, x2, hspec, bias, l, consts):
    nh, bsz, seq, lanes = v.shape
    assert bsz == 2
    rows = bsz * (seq // FFT_N1)
    view = lambda a: a.reshape(nh, rows, FFT_N1, lanes)
    y = view(v)
    for o, gate in enumerate((x1, x2)):
        y = _hyena_conv(y, view(gate), hspec, bias, l, o, consts)
    return y.reshape(nh, bsz, seq, lanes)


def _out_body(of, ob, ug, hy, yf, yb, zz, hgn, ssn, w_ref, h_ref, gate_ref, o_ref):
    o = of[0].astype(F32) + ob[0].astype(F32)
    gi = lax.broadcasted_iota(jnp.int32, (HG_WIDTH, HG_WIDTH), 0) // HG_DK
    gj = lax.broadcasted_iota(jnp.int32, (HG_WIDTH, HG_WIDTH), 1) // HG_DK
    avg = jnp.where(gi == gj, 1.0 / HG_DK, 0.0).astype(BF16)
    sq = o * o
    sh = sq.astype(BF16)
    sl = (sq - sh.astype(F32)).astype(BF16)
    ms = _dot(sh, avg) + _dot(sl, avg)
    hg = o * lax.rsqrt(ms + EPS) * hgn[...] * _silu(ug[0])
    y = (yf[0].astype(F32) + yb[0].astype(F32)) * _silu(zz[0])
    gw = SSD_WIDTH // SSD_GROUPS
    parts = []
    for g in range(SSD_GROUPS):
        yg = y[:, g * gw:(g + 1) * gw]
        parts.append(yg * lax.rsqrt(jnp.mean(yg * yg, axis=-1, keepdims=True) + EPS))
    ys = jnp.concatenate(parts, axis=1) * ssn[...]
    acc = _dot(hg.astype(BF16), w_ref[0:HG_WIDTH, :])
    hyv = jnp.concatenate([hy[hf, 0] for hf in range(HY_HALVES)], axis=1)
    acc = acc + _dot(hyv.astype(BF16), w_ref[HG_WIDTH:HG_WIDTH + HY_WIDTH, :])
    acc = acc + _dot(ys.astype(BF16), w_ref[HG_WIDTH + HY_WIDTH:, :])
    o_ref[0] = h_ref[0] + gate_ref[...] * acc


def _out_proj(of, ob, u_hg, hy, yf, yb, zgate, hgn, ssn, w, h, mods, l, ctx_row, tm):
    bsz, seq, d = h.shape
    tok = lambda n, col=0: pl.BlockSpec((1, tm, n), lambda b, t: (b, t, col))
    return pl.pallas_call(
        _out_body,
        grid=(bsz, seq // tm),
        in_specs=[tok(HG_WIDTH), tok(HG_WIDTH), tok(HG_WIDTH, 4),
                  pl.BlockSpec((HY_HALVES, 1, tm, LANES), lambda b, t: (0, b, t, 0)),
                  tok(SSD_WIDTH), tok(SSD_WIDTH), tok(SSD_WIDTH),
                  _layer_spec(hgn, l), _layer_spec(ssn, l), _layer_spec(w, l),
                  tok(d),
                  _mod_spec(l, ctx_row, MOD_GATE1)],
        out_specs=tok(d),
        out_shape=jax.ShapeDtypeStruct(h.shape, F32),
        compiler_params=_cparams(("parallel", "arbitrary"), big=True),
        name="mixer_out_proj",
    )(of, ob, u_hg, hy, yf, yb, zgate, hgn, ssn, w, h, mods)


def _ffn_body(*refs, tm, grid_w, vertical, final, cchunk):
    if vertical:
        x_ref, hp, hn, sh_ref, sc_ref, g_ref, wg, wu, cw, cb, wd, gate_ref = refs[:12]
        rest = refs[12:]
    else:
        x_ref, sh_ref, sc_ref, g_ref, wg, wu, cw, cb, wd, gate_ref = refs[:10]
        rest = refs[10:]
    if final:
        fg_ref, o_ref, act_ref = rest
    else:
        o_ref, act_ref = rest
    t = pl.program_id(1)
    nt = pl.num_programs(1)

    def normmod(v):
        ms = jnp.mean(v * v, axis=-1, keepdims=True)
        y = v * lax.rsqrt(ms + EPS) * g_ref[...]
        return (y * (1.0 + sc_ref[...]) + sh_ref[...]).astype(BF16)

    x = x_ref[0]
    yb = normmod(x)
    pad = grid_w if vertical else 0
    ext_rows = tm + 2 * pad
    ye = jnp.concatenate([normmod(hp[0]), yb, normmod(hn[0])], axis=0) if vertical else yb
    col = lax.broadcasted_iota(jnp.int32, (ext_rows, cchunk), 0) % grid_w
    for j in range(0, D_FF, cchunk):
        cs = slice(j, j + cchunk)
        ext = _dot(ye, wg[:, cs])
        if vertical:
            ext = jnp.concatenate([jnp.where(t > 0, ext[:pad], 0.0), ext[pad:pad + tm],
                                   jnp.where(t < nt - 1, ext[pad + tm:], 0.0)], axis=0)
        left = jnp.where(col != 0, pltpu.roll(ext, 1, 0), 0.0)
        right = jnp.where(col != grid_w - 1, pltpu.roll(ext, ext_rows - 1, 0), 0.0)
        acc = jnp.zeros((tm, cchunk), F32) + cb[:, cs]
        for dy in ((-1, 0, 1) if vertical else (0,)):
            r0 = pad + dy * grid_w
            ky = dy + 1
            acc = acc + left[r0:r0 + tm] * cw[3 * ky:3 * ky + 1, cs]
            acc = acc + ext[r0:r0 + tm] * cw[3 * ky + 1:3 * ky + 2, cs]
            acc = acc + right[r0:r0 + tm] * cw[3 * ky + 2:3 * ky + 3, cs]
        act_ref[:, cs] = (_silu(acc) * _dot(yb, wu[:, cs])).astype(BF16)
    y = _dot(act_ref[...], wd[...])
    x = x + gate_ref[...] * y
    if final:
        ms = jnp.mean(x * x, axis=-1, keepdims=True)
        x = x * lax.rsqrt(ms + EPS) * fg_ref[...]
    o_ref[0] = x


def _ffn(h, mods, l, ctx_row, g, wg, wu, cw, cb, wd, tm, grid_w, vertical, final_g=None):
    bsz, seq, d = h.shape
    nhb = tm // grid_w
    tok = lambda n: pl.BlockSpec((1, tm, n), lambda b, t: (b, t, 0))
    full = lambda a: _layer_spec(a, l)
    resident = lambda a: pl.BlockSpec((None,) + tuple(a.shape[1:]), lambda b, t: (l, 0, 0),
                                      pipeline_mode=pl.Buffered(1))
    in_specs = [tok(d)]
    args = [h]
    if vertical:
        nrow = seq // grid_w
        in_specs += [
            pl.BlockSpec((1, grid_w, d), lambda b, t: (b, jnp.maximum(t * nhb - 1, 0), 0)),
            pl.BlockSpec((1, grid_w, d), lambda b, t: (b, jnp.minimum((t + 1) * nhb, nrow - 1), 0)),
        ]
        args += [h, h]
    in_specs += [_mod_spec(l, ctx_row, MOD_SHIFT2), _mod_spec(l, ctx_row, MOD_SCALE2), full(g),
                 resident(wg), resident(wu), full(cw), full(cb), resident(wd),
                 _mod_spec(l, ctx_row, MOD_GATE2)]
    args += [mods, mods, g, wg, wu, cw, cb, wd, mods]
    if final_g is not None:
        in_specs.append(pl.BlockSpec((1, d), lambda b, t: (0, 0)))
        args.append(final_g.reshape(1, d))
    return pl.pallas_call(
        functools.partial(_ffn_body, tm=tm, grid_w=grid_w, vertical=vertical,
                          final=final_g is not None, cchunk=256),
        grid=(bsz, seq // tm),
        in_specs=in_specs,
        out_specs=tok(d),
        out_shape=jax.ShapeDtypeStruct(h.shape, F32),
        scratch_shapes=[pltpu.VMEM((tm, D_FF), BF16)],
        compiler_params=_cparams(("parallel", "arbitrary"), big=True),
        name="ffn",
    )(*args)


def kernel(x, c, ctx, c_ctx, w_ada, b_ada, norm1_g, norm2_g, w_in, w_out, hg_lb_logits, hg_norm_g, hy_conv_w, hy_conv_b, hy_w1, hy_b1, hy_freq1, hy_w2, hy_b2, hy_freq2, hy_w3, hy_bias, ssd_conv_w, ssd_conv_b, ssd_dt_bias, ssd_a_log, ssd_d, ssd_norm_g, ffn_w_gate, ffn_w_up, ffn_conv_w, ffn_conv_b, ffn_w_down, final_norm_g):
    bsz, seq, d = x.shape
    depth = w_in.shape[0]
    cc = jnp.zeros((8, d), F32).at[:bsz].set(c).at[bsz].set(c_ctx)
    mods = _ada(cc, w_ada, b_ada).reshape(depth, 8, 6, 1, d)
    ctx_row = bsz

    consts = _dft_constants(seq)
    cs_ctx = _ctx_dft_constants()
    lb_logits = hg_lb_logits.astype(F32)

    row3 = lambda a: a.reshape(depth, 1, -1)
    rep = lambda a: jnp.repeat(a, SSD_HEAD_DIM, axis=-1)
    w_in_b = jnp.pad(w_in, ((0, 0), (0, 0), (0, IN_COLS_PAD - w_in.shape[-1]))).astype(BF16)
    w_out_b = w_out.astype(BF16)
    wg_b, wu_b, wd_b = ffn_w_gate.astype(BF16), ffn_w_up.astype(BF16), ffn_w_down.astype(BF16)
    g1, g2 = row3(norm1_g), row3(norm2_g)
    hcb, scb = row3(hy_conv_b), row3(ssd_conv_b)
    hgn, ssn = row3(hg_norm_g), row3(ssd_norm_g)
    cw_f, cb_f = ffn_conv_w.reshape(depth, 9, D_FF), row3(ffn_conv_b)
    prm = dict(dtb_e=rep(ssd_dt_bias), alog_e=rep(ssd_a_log),
               dtb_t=ssd_dt_bias.reshape(depth, -1, 1), alog_t=ssd_a_log.reshape(depth, -1, 1),
               dsk=row3(rep(ssd_d)))

    for l in range(depth):
        last = l == depth - 1
        proj = lambda h, row, tm, name: _inproj(h, mods, l, row, g1, w_in_b, hy_conv_w, hcb, ssd_conv_w, scb,
                                                tm, name)
        uhg_x, v_x, x1_x, x2_x, xbc_x, z_x, dt_x = proj(x, None, 512, "in_proj_x")
        uhg_c, v_c, x1_c, x2_c, xbc_c, z_c, dt_c = proj(ctx, ctx_row, CTX_LEN, "in_proj_ctx")

        of_x, ob_x, of_c, ob_c = _gla(uhg_x, uhg_c, lb_logits, l)

        dtt_x = jnp.swapaxes(dt_x[:, :, DT_LANE0:DT_LANE0 + 2 * SSD_HEADS], 1, 2)
        dtt_c = jnp.swapaxes(dt_c[:, :, DT_LANE0:DT_LANE0 + 2 * SSD_HEADS], 1, 2)
        yf_x, yb_x, yf_c, yb_c = _ssd(xbc_x, xbc_c, dt_x, dt_c, dtt_x, dtt_c, prm, l)

        fp = _hyena_filter_params(hy_w1[l], hy_b1[l], hy_freq1[l], hy_w2[l], hy_b2[l], hy_freq2[l], hy_w3[l])
        buf, sumabs = _filt_gen(fp, seq, 1024)
        hspec = _hyena_spectrum(buf.reshape(2, HY_HALVES, FFT_N1, FFT_N1, LANES), sumabs, consts, 2 * seq)
        ohy_x = _hyena_x(v_x, x1_x, x2_x, hspec, hy_bias.reshape(depth, 2, 1, HY_WIDTH), l, consts)

        x = _out_proj(of_x, ob_x, uhg_x, ohy_x, yf_x, yb_x, z_x, hgn, ssn, w_out_b, x, mods, l, None, 512)

        if not last:
            buf_c, sumabs_c = _filt_gen(fp, CTX_LEN, CTX_LEN)
            ohy_c = _hy_ctx(v_c, x1_c, x2_c, buf_c, sumabs_c, cs_ctx, hy_bias[l])
            ctx = _out_proj(of_c, ob_c, uhg_c, ohy_c, yf_c, yb_c, z_c, hgn, ssn, w_out_b, ctx, mods, l, ctx_row,
                            CTX_LEN)
            ctx = _ffn(ctx, mods, l, ctx_row, g2, wg_b, wu_b, cw_f, cb_f, wd_b, CTX_LEN, CTX_LEN, False)

        x = _ffn(x, mods, l, None, g2, wg_b, wu_b, cw_f, cb_f, wd_b, 1024, GRID_W, True,
                 final_g=final_norm_g if last else None)
    return x
```

```python
import functools
import math

import numpy as np
import jax
import jax.numpy as jnp
from jax import lax
from jax.experimental import pallas as pl
from jax.experimental.pallas import tpu as pltpu

F32 = jnp.float32
BF16 = jnp.bfloat16

D_MODEL = 1024
DEPTH = 2
CTX_LEN = 256
GRID_W = 64
EPS = 1e-6

HG_HEADS = 4
HG_DK = 64
HG_WIDTH = 256
HG_CHUNK = 32

HY_WIDTH = 256
HY_EMB_BANDS = 16
HY_HIDDEN = 64
HY_MIN_DECAY = math.log(1e-2) / 1.5
HY_MAX_DECAY = math.log(1e-2) / 0.3

SSD_HEADS = 8
SSD_HEAD_DIM = 64
SSD_WIDTH = 512
SSD_GROUPS = 2
SSD_STATE = 128
SSD_XBC = 1024
SSD_CHUNK = 64

D_FF = 2816
HG_COLS = 5 * HG_WIDTH
HY_COLS = 3 * HY_WIDTH

LANES = 128
HY_HALVES = HY_WIDTH // LANES
TOKEN_BLOCK = 256
SCAN_BATCH_ROWS = 2
FFT_N1 = 128
VMEM_LIMIT = 56 * 1024 * 1024


def _cparams(sem, big=False):
    kw = dict(dimension_semantics=sem)
    if big:
        kw["vmem_limit_bytes"] = VMEM_LIMIT
    return pltpu.CompilerParams(**kw)


def _dot(a, b):
    return lax.dot_general(a, b, (((1,), (0,)), ((), ())), preferred_element_type=F32)


def _dot_nt(a, b):
    return lax.dot_general(a, b, (((1,), (1,)), ((), ())), preferred_element_type=F32)


def _dot_tn(a, b):
    return lax.dot_general(a, b, (((0,), (0,)), ((), ())), preferred_element_type=F32)


def _split3(a):
    hi = a.astype(BF16)
    r = a - hi.astype(F32)
    mid = r.astype(BF16)
    lo = (r - mid.astype(F32)).astype(BF16)
    return hi, mid, lo


def _dot_exact_lhs(mask_bf16, a):
    h, m, l = _split3(a)
    return _dot(mask_bf16, h) + _dot(mask_bf16, m) + _dot(mask_bf16, l)


def _dot_exact_rhs(a, mask_bf16):
    h, m, l = _split3(a)
    return _dot(h, mask_bf16) + _dot(m, mask_bf16) + _dot(l, mask_bf16)


def _dot3(a, b):
    ah, am, _ = _split3(a)
    bh, bm, _ = _split3(b)
    return _dot(ah, bh) + _dot(ah, bm) + _dot(am, bh)


def _dot3_pre(fparts, b):
    fh, fm = fparts
    bh, bm, _ = _split3(b)
    return _dot(fh, bh) + _dot(fh, bm) + _dot(fm, bh)


def _sigmoid(x):
    return 1.0 / (1.0 + jnp.exp(-x))


def _silu(x):
    return x * _sigmoid(x)


def _softplus(x):
    return jnp.maximum(x, 0.0) + jnp.log(1.0 + jnp.exp(-jnp.abs(x)))


def _log_sigmoid(x):
    return jnp.minimum(x, 0.0) - jnp.log(1.0 + jnp.exp(-jnp.abs(x)))


def _ada_body(c_ref, w_ref, b_ref, o_ref):
    cc = c_ref[...]
    o_ref[0] = _dot3(_silu(cc), w_ref[0]) + b_ref[0]


def _ada(cc, w_ada, b_ada):
    tn = 1536
    n = w_ada.shape[-1]
    return pl.pallas_call(
        _ada_body,
        grid=(DEPTH, n // tn),
        in_specs=[
            pl.BlockSpec((8, D_MODEL), lambda l, j: (0, 0)),
            pl.BlockSpec((1, D_MODEL, tn), lambda l, j: (l, 0, j)),
            pl.BlockSpec((1, 1, tn), lambda l, j: (l, 0, j)),
        ],
        out_specs=pl.BlockSpec((1, 8, tn), lambda l, j: (l, 0, j)),
        out_shape=jax.ShapeDtypeStruct((DEPTH, 8, n), F32),
        compiler_params=_cparams(("arbitrary", "arbitrary"), big=True),
        name="adaln",
    )(cc, w_ada, b_ada.reshape(DEPTH, 1, n))


IN_COLS = HG_COLS + HY_COLS + SSD_WIDTH + SSD_XBC + 2 * SSD_HEADS
COL_HY = HG_COLS
COL_Z = COL_HY + HY_COLS
COL_XBC = COL_Z + SSD_WIDTH
COL_DT = COL_XBC + SSD_XBC
IN_COLS_PAD = COL_DT + LANES
DT_LANE0 = 0
CONV_CHUNK = 256

MOD_SHIFT1, MOD_SCALE1, MOD_GATE1, MOD_SHIFT2, MOD_SCALE2, MOD_GATE2 = range(6)


def _layer_spec(a, l):
    nd = a.ndim - 1
    return pl.BlockSpec((None,) + tuple(a.shape[1:]), lambda b, t: (l,) + (0,) * nd)


def _mod_spec(l, ctx_row, idx):
    row = (lambda b: b) if ctx_row is None else (lambda b: ctx_row)
    return pl.BlockSpec((None, None, None, 1, D_MODEL), lambda b, t: (l, row(b), idx, 0, 0))


def _inproj_body(x_ref, hp_ref, hn_ref, sh_ref, sc_ref, g_ref, w_ref, hcw, hcb, scw, scb,
                 uhg_ref, v_ref, x1_ref, x2_ref, xbc_ref, z_ref, dt_ref, *, nt):
    t = pl.program_id(1)

    def normmod(x):
        ms = jnp.mean(x * x, axis=-1, keepdims=True)
        y = x * lax.rsqrt(ms + EPS) * g_ref[...]
        return (y * (1.0 + sc_ref[...]) + sh_ref[...]).astype(BF16)

    yb = normmod(x_ref[0])
    rows = yb.shape[0]
    ye = jnp.concatenate([yb, normmod(jnp.concatenate([hp_ref[0], hn_ref[0]], axis=0))], axis=0)
    def plain(ref, c0, c1, o0):
        def run():
            ref[0, :, o0:o0 + (c1 - c0)] = _dot(yb, w_ref[:, c0:c1])
        return run

    plain_tasks = [plain(uhg_ref, j, j + CONV_CHUNK, j) for j in range(0, HG_COLS, CONV_CHUNK)]
    plain_tasks += [plain(z_ref, COL_Z + j, COL_Z + j + CONV_CHUNK, j) for j in range(0, SSD_WIDTH, CONV_CHUNK)]
    plain_tasks += [plain(dt_ref, COL_DT, COL_DT + LANES, 0)]

    ri = lax.broadcasted_iota(jnp.int32, (rows, CONV_CHUNK), 0)

    def conv_chunk(col0, cw_ref, cb_ref, k):
        cols = slice(col0 + k * CONV_CHUNK, col0 + (k + 1) * CONV_CHUNK)
        ccols = slice(k * CONV_CHUNK, (k + 1) * CONV_CHUNK)
        ue = _dot(ye, w_ref[:, cols])
        if plain_tasks:
            plain_tasks.pop(0)()
        u = ue[:rows]
        prev_row = jnp.where(t > 0, ue[rows + 7:rows + 8, :], 0.0)
        next_row = jnp.where(t < nt - 1, ue[rows + 8:rows + 9, :], 0.0)
        dn = jnp.where(ri == 0, prev_row, pltpu.roll(u, 1, 0))
        up = jnp.where(ri == rows - 1, next_row, pltpu.roll(u, rows - 1, 0))
        return dn * cw_ref[0:1, ccols] + u * cw_ref[1:2, ccols] + up * cw_ref[2:3, ccols] + cb_ref[:, ccols]

    for k, ref in enumerate((v_ref, x1_ref, x2_ref)):
        cv = conv_chunk(COL_HY, hcw, hcb, k)
        for hf in range(HY_HALVES):
            ref[hf, 0] = cv[:, hf * LANES:(hf + 1) * LANES]
    for k in range(SSD_XBC // CONV_CHUNK):
        cv = conv_chunk(COL_XBC, scw, scb, k)
        xbc_ref[0, :, k * CONV_CHUNK:(k + 1) * CONV_CHUNK] = _silu(cv).astype(xbc_ref.dtype)
    for task in plain_tasks:
        task()


def _inproj(x, mods, l, ctx_row, g, w, hcw, hcb, scw, scb, tm, name):
    bsz, seq, d = x.shape
    nt = seq // tm
    hb = tm // 8
    tok = lambda n: pl.BlockSpec((1, tm, n), lambda b, t: (b, t, 0))
    hy_spec = pl.BlockSpec((HY_HALVES, 1, tm, LANES), lambda b, t: (0, b, t, 0))
    hy_shape = jax.ShapeDtypeStruct((HY_HALVES, bsz, seq, LANES), F32)
    tshape = lambda n: jax.ShapeDtypeStruct((bsz, seq, n), F32)
    return pl.pallas_call(
        functools.partial(_inproj_body, nt=nt),
        grid=(bsz, nt),
        in_specs=[
            tok(d),
            pl.BlockSpec((1, 8, d), lambda b, t: (b, jnp.maximum(t * hb - 1, 0), 0)),
            pl.BlockSpec((1, 8, d), lambda b, t: (b, jnp.minimum((t + 1) * hb, nt * hb - 1), 0)),
            _mod_spec(l, ctx_row, MOD_SHIFT1), _mod_spec(l, ctx_row, MOD_SCALE1),
            _layer_spec(g, l), _layer_spec(w, l),
            _layer_spec(hcw, l), _layer_spec(hcb, l), _layer_spec(scw, l), _layer_spec(scb, l),
        ],
        out_specs=[tok(HG_COLS), hy_spec, hy_spec, hy_spec, tok(SSD_XBC), tok(SSD_WIDTH), tok(LANES)],
        out_shape=[tshape(HG_COLS), hy_shape, hy_shape, hy_shape,
                   jax.ShapeDtypeStruct((bsz, seq, SSD_XBC), BF16), tshape(SSD_WIDTH), tshape(LANES)],
        compiler_params=_cparams(("parallel", "arbitrary"), big=True),
        name=name,
    )(x, x, x, mods, mods, g, w, hcw, hcb, scw, scb)


def _gla_dir(q, a, v, p, r1, st_ref, reverse):
    tb, ch = TOKEN_BLOCK, HG_CHUNK
    nch = tb // ch
    qv = r1 + _log_sigmoid(a)
    logf = jnp.maximum(p, qv) + jnp.log(1.0 + jnp.exp(-jnp.abs(p - qv)))
    k = 1.0 - jnp.exp(logf)

    ri = lax.broadcasted_iota(jnp.int32, (tb, tb), 0)
    ci = lax.broadcasted_iota(jnp.int32, (tb, tb), 1)
    same = (ri // ch) == (ci // ch)
    tri = (ci >= ri) if reverse else (ci <= ri)
    mask = jnp.logical_and(same, tri)
    mask_b = jnp.where(mask, 1.0, 0.0).astype(BF16)
    bdiag = (ri // HG_DK) == (ci // HG_DK)

    lh, lm, ll = _split3(logf)
    bcs3 = _dot(mask_b, jnp.concatenate([lh, lm, ll], axis=1))
    bcs = bcs3[:, :tb] + bcs3[:, tb:2 * tb] + bcs3[:, 2 * tb:]
    yield None
    b3 = bcs.reshape(nch, ch, tb)
    mid = ch // 2 if reverse else ch // 2 - 1
    end = 0 if reverse else ch - 1
    b_mid = b3[:, mid:mid + 1, :]
    b_end = b3[:, end:end + 1, :]
    q3 = q.reshape(nch, ch, tb)
    k3 = k.reshape(nch, ch, tb)
    d1 = b3 - b_mid
    qd = (q3 * jnp.exp(d1)).reshape(tb, tb)
    kd = (k3 * jnp.exp(-d1)).reshape(tb, tb).astype(BF16)
    kup = (k3 * jnp.exp(b_end - b3)).astype(BF16)
    qb = (q3 * jnp.exp(b3)).astype(BF16)
    dec = jnp.exp(b_end)
    yield None

    lane_head = lax.broadcasted_iota(jnp.int32, (1, tb), 1) // HG_DK
    acc = jnp.zeros((tb, tb), F32)
    for h in range(HG_HEADS):
        sel = lane_head == h
        qh = jnp.where(sel, qd, 0.0).astype(BF16)
        s = _dot_nt(qh, kd)
        s = jnp.where(mask, s, 0.0).astype(BF16)
        vh = jnp.where(sel, v, 0.0).astype(BF16)
        acc = acc + _dot(s, vh)
        yield None

    v3 = v.astype(BF16).reshape(nch, ch, tb)
    inter = [None] * nch
    order = range(nch - 1, -1, -1) if reverse else range(nch)
    st = st_ref[...]
    for c in order:
        inter[c] = _dot_nt(qb[c], st.astype(BF16))
        upd = _dot_tn(v3[c], kup[c])
        st = st * dec[c] + jnp.where(bdiag, upd, 0.0)
        yield None
    st_ref[...] = st
    yield acc + jnp.concatenate(inter, axis=0)


def _interleave(gens, lag=3):
    results = [None] * len(gens)
    done = [False] * len(gens)
    step = 0
    while not all(done):
        for i, g in enumerate(gens):
            if done[i] or step < i * lag:
                continue
            try:
                results[i] = next(g)
            except StopIteration:
                done[i] = True
        step += 1
    return results


def _gla_body(qf, ff, vf, qb, fb, vb, qc, ffc, fbc, vc, lb_ref, of_x, ob_x, of_c, ob_c, stf, stb, *, layer):
    t = pl.program_id(1)
    is_ctx = t == 0

    @pl.when(is_ctx)
    def _():
        stf[...] = jnp.zeros_like(stf)
        stb[...] = jnp.zeros_like(stb)

    lg = lb_ref[...]
    e = jnp.exp(lg - jnp.max(lg, axis=0, keepdims=True))
    den = jnp.sum(e, axis=0, keepdims=True)
    num = jnp.zeros_like(den)
    for r in range(1, layer + 1):
        num = num + e[r:r + 1, :]
    lb = num / den
    p = jnp.log(lb)
    r1 = jnp.log(1.0 - lb)

    scale = HG_DK ** -0.5
    nb = qf.shape[0]
    gens = []
    for b in range(nb):
        gens.append(_gla_dir(jnp.where(is_ctx, qc[b], qf[b]) * scale, jnp.where(is_ctx, ffc[b], ff[b]),
                             jnp.where(is_ctx, vc[b], vf[b]), p, r1, stf.at[b], False))
        gens.append(_gla_dir(jnp.where(is_ctx, qc[b], qb[b]) * scale, jnp.where(is_ctx, fbc[b], fb[b]),
                             jnp.where(is_ctx, vc[b], vb[b]), p, r1, stb.at[b], True))
    outs = _interleave(gens, lag=1)

    @pl.when(is_ctx)
    def _():
        for b in range(nb):
            of_c[b] = outs[2 * b].astype(of_c.dtype)
            ob_c[b] = outs[2 * b + 1].astype(ob_c.dtype)

    @pl.when(jnp.logical_not(is_ctx))
    def _():
        for b in range(nb):
            of_x[b] = outs[2 * b].astype(of_x.dtype)
            ob_x[b] = outs[2 * b + 1].astype(ob_x.dtype)


def _scan_block_maps(nxb):
    fwd = lambda t: jnp.maximum(t - 1, 0)
    bwd = lambda t: jnp.where(t == 0, nxb - 1, nxb - t)
    return fwd, bwd


def _gla(u_x, u_c, lb_logits, layer):
    bsz, seq, _ = u_x.shape
    tb = TOKEN_BLOCK
    nxb = seq // tb
    xf, xb = _scan_block_maps(nxb)

    nb = SCAN_BATCH_ROWS

    def xs(blk, col):
        return pl.BlockSpec((nb, tb, HG_WIDTH), lambda b, t: (b, blk(t), col))

    def cs(col):
        return pl.BlockSpec((nb, tb, HG_WIDTH), lambda b, t: (b, 0, col))

    out_x = jax.ShapeDtypeStruct((bsz, seq, HG_WIDTH), BF16)
    out_c = jax.ShapeDtypeStruct((bsz, CTX_LEN, HG_WIDTH), BF16)
    state = pltpu.VMEM((nb, HG_WIDTH, HG_WIDTH), F32)
    return pl.pallas_call(
        functools.partial(_gla_body, layer=layer),
        grid=(bsz // nb, nxb + 1),
        in_specs=[xs(xf, 0), xs(xf, 1), xs(xf, 3), xs(xb, 0), xs(xb, 2), xs(xb, 3),
                  cs(0), cs(1), cs(2), cs(3),
                  pl.BlockSpec((DEPTH, HG_WIDTH), lambda b, t: (0, 0))],
        out_specs=[xs(xf, 0), xs(xb, 0), cs(0), cs(0)],
        out_shape=[out_x, out_x, out_c, out_c],
        scratch_shapes=[state, state],
        compiler_params=_cparams(("parallel", "arbitrary"), big=True),
        name="hgrn2_scan",
    )(u_x, u_x, u_x, u_x, u_x, u_x, u_c, u_c, u_c, u_c, lb_logits)


def _ssd_dir(xbc, dt_raw, dtt_all, dtb_e, alog_e, dtb_t, alog_t, dsk, st_ref, reverse):
    tb, ch = TOKEN_BLOCK, SSD_CHUNK
    nch = tb // ch
    d = 1 if reverse else 0
    xs = xbc[:, :SSD_WIDTH].astype(F32)
    bm = xbc[:, SSD_WIDTH:SSD_WIDTH + 256].astype(BF16)
    cm = xbc[:, SSD_WIDTH + 256:].astype(BF16)

    ri = lax.broadcasted_iota(jnp.int32, (tb, tb), 0)
    ci = lax.broadcasted_iota(jnp.int32, (tb, tb), 1)
    same = (ri // ch) == (ci // ch)
    mask = jnp.logical_and(same, (ci >= ri) if reverse else (ci <= ri))
    mask_b = jnp.where(mask, 1.0, 0.0).astype(BF16)
    mask_t = jnp.logical_and(same, (ri >= ci) if reverse else (ri <= ci))
    mask_tb = jnp.where(mask_t, 1.0, 0.0).astype(BF16)

    ej = lax.broadcasted_iota(jnp.int32, (128, SSD_WIDTH), 0)
    el = lax.broadcasted_iota(jnp.int32, (128, SSD_WIDTH), 1)
    expand = jnp.where(ej == DT_LANE0 + d * SSD_HEADS + el // SSD_HEAD_DIM, 1.0, 0.0).astype(BF16)
    dh, dm, dl = _split3(dt_raw)
    dte_raw = _dot(jnp.concatenate([dh, dm, dl], axis=1), jnp.concatenate([expand] * 3, axis=0))
    dte = _softplus(dte_raw + dtb_e[d:d + 1, :])
    a_e = -jnp.exp(alog_e[d:d + 1, :])
    ah, am, al = _split3(dte * a_e)
    acs3p = _dot(mask_b, jnp.concatenate([ah, am, al], axis=1))
    acs = acs3p[:, :SSD_WIDTH] + acs3p[:, SSD_WIDTH:2 * SSD_WIDTH] + acs3p[:, 2 * SSD_WIDTH:]
    yield None
    xdt = xs * dte
    acs3 = acs.reshape(nch, ch, SSD_WIDTH)
    end = 0 if reverse else ch - 1
    a_end = acs3[:, end:end + 1, :]
    xw = (xdt.reshape(nch, ch, SSD_WIDTH) * jnp.exp(a_end - acs3)).astype(BF16).reshape(tb, SSD_WIDTH)
    ea = jnp.exp(acs)
    dec = jnp.exp(a_end)

    dtt_raw = dtt_all[d * SSD_HEADS:(d + 1) * SSD_HEADS, :]
    dtt = _softplus(dtt_raw + dtb_t[d * SSD_HEADS:(d + 1) * SSD_HEADS, :])
    a_t = -jnp.exp(alog_t[d * SSD_HEADS:(d + 1) * SSD_HEADS, :])
    th, tm_, tl = _split3(dtt * a_t)
    acs_t3 = _dot(jnp.concatenate([th, tm_, tl], axis=0), mask_tb)
    acs_t = acs_t3[:SSD_HEADS] + acs_t3[SSD_HEADS:2 * SSD_HEADS] + acs_t3[2 * SSD_HEADS:]
    yield None

    def chunk_diag(m):
        z = jnp.zeros((ch, m.shape[1]), m.dtype)
        return jnp.concatenate(
            [jnp.concatenate([m[c * ch:(c + 1) * ch] if k == c else z for k in range(nch)], axis=1)
             for c in range(nch)], axis=0)

    xdt_b = xdt.astype(BF16)
    hd = SSD_HEAD_DIM
    hpg = SSD_HEADS // SSD_GROUPS
    gw = SSD_WIDTH // SSD_GROUPS
    rr = lax.broadcasted_iota(jnp.int32, (tb, gw), 0)
    ll = lax.broadcasted_iota(jnp.int32, (tb, gw), 1)
    tri4 = ((ll % hd) >= (rr % ch)) if reverse else ((ll % hd) <= (rr % ch))
    head_diag = (rr // hd) == (ll // hd)
    order = range(nch - 1, -1, -1) if reverse else range(nch)
    outs = []
    for g in range(SSD_GROUPS):
        st_cols = slice(g * SSD_STATE, (g + 1) * SSD_STATE)
        cols = slice(g * gw, (g + 1) * gw)
        cm_d = chunk_diag(cm[:, st_cols])
        bm_cat = jnp.concatenate([bm[c * ch:(c + 1) * ch, st_cols] for c in range(nch)], axis=1)
        cb4 = _dot_nt(cm_d, jnp.concatenate([bm_cat] * hpg, axis=0))
        rowm = jnp.concatenate(
            [jnp.broadcast_to(jnp.concatenate([acs_t[g * hpg + h:g * hpg + h + 1, c * ch:(c + 1) * ch]
                                               for h in range(hpg)], axis=1), (ch, gw))
             for c in range(nch)], axis=0)
        w4 = (cb4 * jnp.exp(jnp.where(tri4, acs[:, cols] - rowm, -1e30))).astype(BF16)
        xg = xdt_b[:, cols]
        zero = jnp.zeros((hpg * ch, gw), BF16)
        x4 = jnp.concatenate(
            [jnp.where(head_diag, jnp.concatenate([xg[c * ch:(c + 1) * ch]] * hpg, axis=0), zero)
             for c in range(nch)], axis=0)
        y_intra = _dot(chunk_diag(w4), x4)
        yield None

        upd = _dot_tn(bm[:, st_cols], chunk_diag(xw[:, cols]))
        st = st_ref[g]
        entering = [None] * nch
        for c in order:
            entering[c] = st.astype(BF16)
            st = st * dec[c][:, cols] + upd[:, c * gw:(c + 1) * gw]
        st_ref[g] = st
        y_inter = _dot(cm_d, jnp.concatenate(entering, axis=0)) * ea[:, cols]
        outs.append(y_intra + y_inter)
        yield None
    o = jnp.concatenate(outs, axis=1)
    if not reverse:
        o = o + dsk[...] * xs
    yield o


def _ssd_body(xf, dtf, dttf, xb, dtb, dttb, xc, dtc, dttc, dtb_e, alog_e, dtb_t, alog_t, dsk,
              of_x, ob_x, of_c, ob_c, stf, stb):
    t = pl.program_id(1)
    is_ctx = t == 0

    @pl.when(is_ctx)
    def _():
        stf[...] = jnp.zeros_like(stf)
        stb[...] = jnp.zeros_like(stb)

    prm = (dtb_e, alog_e, dtb_t, alog_t, dsk)
    nb = xf.shape[0]
    gens = []
    for b in range(nb):
        gens.append(_ssd_dir(jnp.where(is_ctx, xc[b], xf[b]), jnp.where(is_ctx, dtc[b], dtf[b]),
                             jnp.where(is_ctx, dttc[b], dttf[b]), *prm, stf.at[b], False))
        gens.append(_ssd_dir(jnp.where(is_ctx, xc[b], xb[b]), jnp.where(is_ctx, dtc[b], dtb[b]),
                             jnp.where(is_ctx, dttc[b], dttb[b]), *prm, stb.at[b], True))
    outs = _interleave(gens, lag=2)

    @pl.when(is_ctx)
    def _():
        for b in range(nb):
            of_c[b] = outs[2 * b].astype(of_c.dtype)
            ob_c[b] = outs[2 * b + 1].astype(ob_c.dtype)

    @pl.when(jnp.logical_not(is_ctx))
    def _():
        for b in range(nb):
            of_x[b] = outs[2 * b].astype(of_x.dtype)
            ob_x[b] = outs[2 * b + 1].astype(ob_x.dtype)


def _ssd(xbc_x, xbc_c, dt_x, dt_c, dtt_x, dtt_c, prm, l):
    bsz, seq, _ = xbc_x.shape
    tb = TOKEN_BLOCK
    nxb = seq // tb
    xf, xb = _scan_block_maps(nxb)
    full = lambda a: _layer_spec(a, l)
    params = [prm["dtb_e"], prm["alog_e"], prm["dtb_t"], prm["alog_t"], prm["dsk"]]

    nb = SCAN_BATCH_ROWS

    def xspecs(blk):
        return [pl.BlockSpec((nb, tb, SSD_XBC), lambda b, t: (b, blk(t), 0)),
                pl.BlockSpec((nb, tb, 128), lambda b, t: (b, blk(t), 0)),
                pl.BlockSpec((nb, 16, tb), lambda b, t: (b, 0, blk(t)))]

    zero = lambda t: 0
    yspec = lambda blk: pl.BlockSpec((nb, tb, SSD_WIDTH), lambda b, t: (b, blk(t), 0))
    out_x = jax.ShapeDtypeStruct((bsz, seq, SSD_WIDTH), BF16)
    out_c = jax.ShapeDtypeStruct((bsz, CTX_LEN, SSD_WIDTH), BF16)
    state = pltpu.VMEM((nb, SSD_GROUPS, SSD_STATE, SSD_WIDTH // SSD_GROUPS), F32)
    return pl.pallas_call(
        _ssd_body,
        grid=(bsz // nb, nxb + 1),
        in_specs=xspecs(xf) + xspecs(xb) + xspecs(zero) + [full(a) for a in params],
        out_specs=[yspec(xf), yspec(xb), yspec(zero), yspec(zero)],
        out_shape=[out_x, out_x, out_c, out_c],
        scratch_shapes=[state, state],
        compiler_params=_cparams(("parallel", "arbitrary"), big=True),
        name="ssd_scan",
    )(xbc_x, dt_x, dtt_x, xbc_x, dt_x, dtt_x, xbc_c, dt_c, dtt_c, *params)


def _filt_body(w1, b1, f1, w2, b2, f2, w3, delta, buf_ref, sa_ref, *, seq_len, tr):
    i = pl.program_id(0)
    hr = tr // 2
    lane = lax.broadcasted_iota(jnp.int32, (hr, 128), 1)
    grp, sub = lane // HY_HIDDEN, lane % HY_HIDDEN
    jrow = i * tr + grp * hr + lax.broadcasted_iota(jnp.int32, (hr, 128), 0)
    pos = jnp.where(jrow < seq_len, jrow, 2 * seq_len - jrow).astype(F32)
    tpos = pos / (seq_len - 1.0)
    wpos = pos * (2.0 * math.pi / seq_len)
    bidx = jnp.where(sub <= HY_EMB_BANDS, sub - 1, sub - 1 - HY_EMB_BANDS).astype(F32)
    band = 1e-4 + bidx * ((HY_EMB_BANDS - 1 - 1e-4) / (HY_EMB_BANDS - 1))
    ang = band * wpos + jnp.where(sub > HY_EMB_BANDS, 0.5 * math.pi, 0.0)
    z = jnp.where(sub == 0, tpos, jnp.where(sub <= 2 * HY_EMB_BANDS, jnp.cos(ang), 0.0))
    h = jnp.sin(f1[...] * (_dot3(z, w1[...]) + b1[...]))
    h = jnp.sin(f2[...] * (_dot3(h, w2[...]) + b2[...]))

    @pl.when(i == 0)
    def _():
        sa_ref[...] = jnp.zeros_like(sa_ref)

    for g in range(2):
        o = _dot3(h, w3[0, g])
        o = o * jnp.exp(-tpos[:, g * HY_HIDDEN:g * HY_HIDDEN + 1] * delta[...])
        o = jnp.where(jrow[:, g * HY_HIDDEN:g * HY_HIDDEN + 1] == seq_len, 0.0, o)
        for order in range(2):
            for hf in range(HY_HALVES):
                lo = order * HY_WIDTH + hf * LANES
                buf_ref[order, hf, g * hr:(g + 1) * hr, :] = o[:, lo:lo + LANES]
        sa_ref[...] = sa_ref[...] + jnp.sum(jnp.abs(o), axis=0, keepdims=True)


def _filt_gen(p, seq_len, tr):
    n = 2 * seq_len
    assert seq_len % tr == 0
    half = seq_len // tr
    full = lambda a: pl.BlockSpec(a.shape, lambda i: (0,) * a.ndim)
    small = [p["w1"], p["b1"], p["f1"], p["w2"], p["b2"], p["f2"]]
    return pl.pallas_call(
        functools.partial(_filt_body, seq_len=seq_len, tr=tr),
        grid=(n // tr,),
        in_specs=[full(a) for a in small] + [
            pl.BlockSpec((1, 2, 2 * HY_HIDDEN, 2 * HY_WIDTH), lambda i: (i // half, 0, 0, 0)),
            full(p["delta"]),
        ],
        out_specs=[pl.BlockSpec((2, HY_HALVES, tr, LANES), lambda i: (0, 0, i, 0)),
                   pl.BlockSpec((8, 2 * HY_WIDTH), lambda i: (0, 0))],
        out_shape=[jax.ShapeDtypeStruct((2, HY_HALVES, n, LANES), F32),
                   jax.ShapeDtypeStruct((8, 2 * HY_WIDTH), F32)],
        compiler_params=_cparams(("arbitrary",)),
        name="hyena_filter",
    )(*small, p["w3"], p["delta"])


FFT_NB = 8


def _strided_rows(ref3, j, n):
    flat = ref3.reshape(ref3.shape[0] * FFT_NB, LANES)
    return flat[pl.ds(j, n, stride=FFT_NB), :]


def _store_strided_rows(ref3, j, val):
    flat = ref3.reshape(ref3.shape[0] * FFT_NB, LANES)
    flat[pl.ds(j, val.shape[0], stride=FFT_NB), :] = val


def _load_complex(ref, k):
    return jnp.concatenate(
        [jnp.concatenate([ref[hf, 0, k], ref[hf, 1, k]], axis=0) for hf in range(HY_HALVES)], axis=1)


def _store_complex(ref, k, val):
    for hf in range(HY_HALVES):
        ref[hf, 0, k] = val[:FFT_N1, hf * LANES:(hf + 1) * LANES].astype(ref.dtype)
        ref[hf, 1, k] = val[FFT_N1:, hf * LANES:(hf + 1) * LANES].astype(ref.dtype)


FFT_TILES = FFT_N1 // FFT_NB
TILE_ROWS = 2 * FFT_N1 * FFT_NB


def _spec_stage1(x_refs, f_ref, t_ref, scr, j, rows):
    for i in range(FFT_NB):
        x = jnp.concatenate([_strided_rows(r, i, rows) for r in x_refs], axis=1)
        a = _dot3_pre((f_ref[0], f_ref[1]), x)
        tr = t_ref[0, 0, :, i:i + 1]
        ti = t_ref[0, 1, :, i:i + 1]
        ar, ai = a[:FFT_N1], a[FFT_N1:]
        out = jnp.concatenate([ar * tr - ai * ti, ar * ti + ai * tr], axis=0)
        for hf in range(HY_HALVES):
            scr[hf, pl.ds(j * TILE_ROWS + i, 2 * FFT_N1, stride=FFT_NB), :] = out[:, hf * LANES:(hf + 1) * LANES]


def _spec_slab(scr, k1):
    def part(hf, p):
        r0 = (p * FFT_N1 + k1) * FFT_NB
        return jnp.concatenate(
            [scr[hf, pl.ds(pl.multiple_of(jj * TILE_ROWS + r0, FFT_NB), FFT_NB), :] for jj in range(FFT_TILES)], axis=0)
    return jnp.concatenate(
        [jnp.concatenate([part(hf, 0), part(hf, 1)], axis=0) for hf in range(HY_HALVES)], axis=1)


def _spec_slab_store(scr, k1, val):
    for hf in range(HY_HALVES):
        for p in range(2):
            r0 = (p * FFT_N1 + k1) * FFT_NB
            for jj in range(FFT_TILES):
                scr[hf, pl.ds(pl.multiple_of(jj * TILE_ROWS + r0, FFT_NB), FFT_NB), :] = val[
                    p * FFT_N1 + jj * FFT_NB:p * FFT_N1 + (jj + 1) * FFT_NB, hf * LANES:(hf + 1) * LANES]


def _hconv_body(y_ref, h_ref, gate_ref, f1_ref, f2_ref, f2c_ref, f3_ref, t_ref, bias_ref, o_ref, scr, *, kb):
    s = pl.program_id(0)
    rows = y_ref.shape[1]

    @pl.when(s < FFT_TILES)
    def _():
        _spec_stage1([y_ref.at[hf] for hf in range(HY_HALVES)], f1_ref, t_ref, scr, s, rows)

    @pl.when(jnp.logical_and(s >= FFT_TILES, s < 2 * FFT_TILES))
    def _():
        kt = s - FFT_TILES
        for k in range(kb):
            k1 = kt * kb + k
            x = _dot3_pre((f2_ref[0], f2_ref[1]), _spec_slab(scr, k1))
            xr, xi = x[:FFT_N1], x[FFT_N1:]
            h = _load_complex(h_ref, k).astype(F32)
            hr, hi = h[:FFT_N1], h[FFT_N1:]
            y = jnp.concatenate([xr * hr - xi * hi, xr * hi + xi * hr], axis=0)
            _spec_slab_store(scr, k1, _dot3_pre((f2c_ref[0], f2c_ref[1]), y))

    @pl.when(s >= 2 * FFT_TILES)
    def _():
        j = s - 2 * FFT_TILES
        for i in range(FFT_NB):
            tr = t_ref[0, 0, :, i:i + 1]
            ti = t_ref[0, 1, :, i:i + 1]
            c = jnp.concatenate(
                [scr[hf, pl.ds(j * TILE_ROWS + i, 2 * FFT_N1, stride=FFT_NB), :] for hf in range(HY_HALVES)], axis=1)
            cr, ci = c[:FFT_N1], c[FFT_N1:]
            dmat = jnp.concatenate([cr * tr + ci * ti, ci * tr - cr * ti], axis=0)
            w = _dot3_pre((f3_ref[0], f3_ref[1]), dmat)
            for hf in range(HY_HALVES):
                lanes = slice(hf * LANES, (hf + 1) * LANES)
                g = _strided_rows(gate_ref.at[hf], i, rows)
                v = _strided_rows(y_ref.at[hf], i, rows)
                _store_strided_rows(o_ref.at[hf], i, g * (w[:, lanes] + v * bias_ref[:, lanes]))


def _hyena_conv(y, gate, hspec, bias, l, order, consts, kb=8):
    nh, rows, n2, lanes = y.shape
    nt = FFT_TILES
    tile = lambda f: pl.BlockSpec((nh, rows, FFT_NB, lanes), lambda s: (0, 0, f(s), 0))
    in_tile = lambda s: jnp.where(s < nt, s, jnp.maximum(s - 2 * nt, 0))
    out_tile = lambda s: jnp.maximum(s - 2 * nt, 0)
    const = lambda a: pl.BlockSpec(a.shape, lambda s: (0,) * a.ndim)
    return pl.pallas_call(
        functools.partial(_hconv_body, kb=kb),
        grid=(3 * nt,),
        in_specs=[
            tile(in_tile),
            pl.BlockSpec((None, nh, 2, kb, FFT_N1, lanes),
                         lambda s: (order, 0, 0, jnp.clip(s - nt, 0, FFT_N1 // kb - 1), 0, 0)),
            tile(out_tile),
            const(consts["f1_data"]), const(consts["f2"]), const(consts["f2c"]), const(consts["f3"]),
            pl.BlockSpec((1, 2, FFT_N1, FFT_NB), lambda s: (in_tile(s), 0, 0, 0)),
            pl.BlockSpec((None, None, 1, HY_WIDTH), lambda s: (l, order, 0, 0)),
        ],
        out_specs=tile(out_tile),
        out_shape=jax.ShapeDtypeStruct(y.shape, F32),
        scratch_shapes=[pltpu.VMEM((nh, nt * TILE_ROWS, lanes), F32)],
        compiler_params=_cparams(("arbitrary",), big=True),
        name="hyena_conv",
    )(y, hspec, gate, consts["f1_data"], consts["f2"], consts["f2c"], consts["f3"], consts["twid"], bias)


def _hspec_body(b_ref, sa_ref, f1_ref, f2_ref, t_ref, o_ref, scr, *, kb, n):
    o = pl.program_id(0)
    s = pl.program_id(1)
    rows = b_ref.shape[1]

    @pl.when(s < FFT_TILES)
    def _():
        _spec_stage1([b_ref.at[hf] for hf in range(HY_HALVES)], f1_ref, t_ref, scr, s, rows)

    @pl.when(s >= FFT_TILES)
    def _():
        sa = jnp.where(o == 0, sa_ref[0:1, :HY_WIDTH], sa_ref[0:1, HY_WIDTH:])
        scale = 1.0 / (sa * float(n))
        kt = s - FFT_TILES
        for k in range(kb):
            x = _dot3_pre((f2_ref[0], f2_ref[1]), _spec_slab(scr, kt * kb + k))
            _store_complex(o_ref, k, x * scale)


def _hyena_spectrum(buf, sumabs, consts, n, kb=8):
    _, nh, rows, n2, lanes = buf.shape
    nt = FFT_TILES
    const = lambda a: pl.BlockSpec(a.shape, lambda o, s: (0,) * a.ndim)
    return pl.pallas_call(
        functools.partial(_hspec_body, kb=kb, n=n),
        grid=(2, 2 * nt),
        in_specs=[
            pl.BlockSpec((None, nh, rows, FFT_NB, lanes), lambda o, s: (o, 0, 0, jnp.minimum(s, nt - 1), 0)),
            const(sumabs), const(consts["f1_real"]), const(consts["f2"]),
            pl.BlockSpec((1, 2, FFT_N1, FFT_NB), lambda o, s: (jnp.minimum(s, nt - 1), 0, 0, 0)),
        ],
        out_specs=pl.BlockSpec((None, nh, 2, kb, FFT_N1, lanes), lambda o, s: (o, 0, 0, jnp.maximum(s - nt, 0), 0, 0)),
        out_shape=jax.ShapeDtypeStruct((2, nh, 2, FFT_N1, FFT_N1, lanes), BF16),
        scratch_shapes=[pltpu.VMEM((nh, nt * TILE_ROWS, lanes), F32)],
        compiler_params=_cparams(("arbitrary", "arbitrary"), big=True),
        name="hyena_spectrum",
    )(buf, sumabs, consts["f1_real"], consts["f2"], consts["twid"])


def _hyc_body(v_ref, x1_ref, x2_ref, buf_ref, sa_ref, cs_ref, bias_ref, o_ref):
    n = 2 * CTX_LEN
    cmat, smat = cs_ref[0], cs_ref[1]
    c_in, s_in = cmat[:, :CTX_LEN], smat[:, :CTX_LEN]
    c_out, s_out = cmat[:CTX_LEN, :], smat[:CTX_LEN, :]
    batch = lambda ref, b: jnp.concatenate([ref[hf, b] for hf in range(HY_HALVES)], axis=1)
    yr, yi = batch(v_ref, 0), batch(v_ref, 1)
    for o, gate in enumerate((x1_ref, x2_ref)):
        bufo = jnp.concatenate([buf_ref[o, hf] for hf in range(HY_HALVES)], axis=1)
        scale = 1.0 / (sa_ref[0:1, o * HY_WIDTH:(o + 1) * HY_WIDTH] * float(n))
        hr = _dot3(cmat, bufo) * scale
        hi = -_dot3(smat, bufo) * scale
        xr = _dot3(c_in, yr) + _dot3(s_in, yi)
        xi = _dot3(c_in, yi) - _dot3(s_in, yr)
        zr = xr * hr - xi * hi
        zi = xr * hi + xi * hr
        wr = _dot3(c_out, zr) - _dot3(s_out, zi)
        wi = _dot3(c_out, zi) + _dot3(s_out, zr)
        b = bias_ref[o:o + 1, :]
        yr = batch(gate, 0) * (wr + yr * b)
        yi = batch(gate, 1) * (wi + yi * b)
    for hf in range(HY_HALVES):
        o_ref[hf, 0] = yr[:, hf * LANES:(hf + 1) * LANES]
        o_ref[hf, 1] = yi[:, hf * LANES:(hf + 1) * LANES]


def _hy_ctx(v, x1, x2, buf, sumabs, cs, bias):
    return pl.pallas_call(
        _hyc_body,
        out_shape=jax.ShapeDtypeStruct(v.shape, F32),
        compiler_params=pltpu.CompilerParams(vmem_limit_bytes=VMEM_LIMIT),
        name="hyena_ctx",
    )(v, x1, x2, buf, sumabs, cs, bias)


def _dft_constants(seq_len):
    n = 2 * seq_len
    n1 = FFT_N1
    assert n == n1 * n1
    idx = np.arange(n1, dtype=np.float64)
    th = 2.0 * np.pi * np.outer(idx, idx) / n1
    fr, fi = np.cos(th), -np.sin(th)
    half = n1 // 2

    def parts(m):
        m32 = jnp.asarray(m, F32)
        hi = m32.astype(BF16)
        mid = (m32 - hi.astype(F32)).astype(BF16)
        return jnp.stack([hi, mid])

    f1_data = np.block([[fr[:, :half], -fi[:, :half]], [fi[:, :half], fr[:, :half]]])
    f1_real = np.concatenate([fr, fi], axis=0)
    f2 = np.block([[fr, -fi], [fi, fr]])
    f2c = np.block([[fr, fi], [-fi, fr]])
    f3 = np.block([[fr[:half], fi[:half]], [-fi[:half], fr[:half]]])
    tw = 2.0 * np.pi * np.outer(idx, idx) / n
    twid = np.stack([np.cos(tw), -np.sin(tw)])
    twid = twid.reshape(2, n1, n1 // FFT_NB, FFT_NB).transpose(2, 0, 1, 3)
    return dict(f1_data=parts(f1_data), f1_real=parts(f1_real), f2=parts(f2), f2c=parts(f2c),
                f3=parts(f3), twid=jnp.asarray(twid, F32))


def _ctx_dft_constants():
    n = 2 * CTX_LEN
    idx = np.arange(n, dtype=np.float64)
    th = 2.0 * np.pi * np.outer(idx, idx) / n
    return jnp.asarray(np.stack([np.cos(th), np.sin(th)]), F32)


def _hyena_filter_params(hy_w1, hy_b1, hy_freq1, hy_w2, hy_b2, hy_freq2, hy_w3):
    hid = HY_HIDDEN
    zeros = jnp.zeros((hid, hid), F32)
    w1 = jnp.zeros((hid, hid), F32).at[:hy_w1.shape[0]].set(hy_w1)
    w1 = jnp.block([[w1, zeros], [zeros, w1]])
    w2 = jnp.block([[hy_w2, zeros], [zeros, hy_w2]])
    w3 = hy_w3.reshape(hid, 2, 2 * HY_WIDTH).transpose(1, 0, 2)
    z3 = jnp.zeros_like(w3)
    w3 = jnp.stack([jnp.concatenate([w3, z3], axis=1), jnp.concatenate([z3, w3], axis=1)], axis=1)
    deltas = np.abs(np.linspace(HY_MIN_DECAY, HY_MAX_DECAY, HY_WIDTH))
    delta = jnp.asarray(np.tile(deltas, 2)[None, :], F32)
    row = lambda a: jnp.tile(a, 2).reshape(1, -1)
    return dict(w1=w1, b1=row(hy_b1), f1=row(hy_freq1), w2=w2, b2=row(hy_b2), f2=row(hy_freq2),
                w3=w3, delta=delta)


def _hyena_x(v, x1, x2, hspec, bias, l, consts):
    nh, bsz, seq, lanes = v.shape
    assert bsz == 2
    rows = bsz * (seq // FFT_N1)
    view = lambda a: a.reshape(nh, rows, FFT_N1, lanes)
    y = view(v)
    for o, gate in enumerate((x1, x2)):
        y = _hyena_conv(y, view(gate), hspec, bias, l, o, consts)
    return y.reshape(nh, bsz, seq, lanes)


def _out_body(of, ob, ug, hy, yf, yb, zz, hgn, ssn, w_ref, h_ref, gate_ref, o_ref):
    o = of[0].astype(F32) + ob[0].astype(F32)
    gi = lax.broadcasted_iota(jnp.int32, (HG_WIDTH, HG_WIDTH), 0) // HG_DK
    gj = lax.broadcasted_iota(jnp.int32, (HG_WIDTH, HG_WIDTH), 1) // HG_DK
    avg = jnp.where(gi == gj, 1.0 / HG_DK, 0.0).astype(BF16)
    sq = o * o
    sh = sq.astype(BF16)
    sl = (sq - sh.astype(F32)).astype(BF16)
    ms = _dot(sh, avg) + _dot(sl, avg)
    hg = o * lax.rsqrt(ms + EPS) * hgn[...] * _silu(ug[0])
    y = (yf[0].astype(F32) + yb[0].astype(F32)) * _silu(zz[0])
    gw = SSD_WIDTH // SSD_GROUPS
    parts = []
    for g in range(SSD_GROUPS):
        yg = y[:, g * gw:(g + 1) * gw]
        parts.append(yg * lax.rsqrt(jnp.mean(yg * yg, axis=-1, keepdims=True) + EPS))
    ys = jnp.concatenate(parts, axis=1) * ssn[...]
    acc = _dot(hg.astype(BF16), w_ref[0:HG_WIDTH, :])
    hyv = jnp.concatenate([hy[hf, 0] for hf in range(HY_HALVES)], axis=1)
    acc = acc + _dot(hyv.astype(BF16), w_ref[HG_WIDTH:HG_WIDTH + HY_WIDTH, :])
    acc = acc + _dot(ys.astype(BF16), w_ref[HG_WIDTH + HY_WIDTH:, :])
    o_ref[0] = h_ref[0] + gate_ref[...] * acc


def _out_proj(of, ob, u_hg, hy, yf, yb, zgate, hgn, ssn, w, h, mods, l, ctx_row, tm):
    bsz, seq, d = h.shape
    tok = lambda n, col=0: pl.BlockSpec((1, tm, n), lambda b, t: (b, t, col))
    return pl.pallas_call(
        _out_body,
        grid=(bsz, seq // tm),
        in_specs=[tok(HG_WIDTH), tok(HG_WIDTH), tok(HG_WIDTH, 4),
                  pl.BlockSpec((HY_HALVES, 1, tm, LANES), lambda b, t: (0, b, t, 0)),
                  tok(SSD_WIDTH), tok(SSD_WIDTH), tok(SSD_WIDTH),
                  _layer_spec(hgn, l), _layer_spec(ssn, l), _layer_spec(w, l),
                  tok(d),
                  _mod_spec(l, ctx_row, MOD_GATE1)],
        out_specs=tok(d),
        out_shape=jax.ShapeDtypeStruct(h.shape, F32),
        compiler_params=_cparams(("parallel", "arbitrary"), big=True),
        name="mixer_out_proj",
    )(of, ob, u_hg, hy, yf, yb, zgate, hgn, ssn, w, h, mods)


def _ffn_body(*refs, tm, grid_w, vertical, final, cchunk):
    if vertical:
        x_ref, hp, hn, sh_ref, sc_ref, g_ref, wg, wu, cw, cb, wd, gate_ref = refs[:12]
        rest = refs[12:]
    else:
        x_ref, sh_ref, sc_ref, g_ref, wg, wu, cw, cb, wd, gate_ref = refs[:10]
        rest = refs[10:]
    if final:
        fg_ref, o_ref, act_ref = rest
    else:
        o_ref, act_ref = rest
    t = pl.program_id(1)
    nt = pl.num_programs(1)

    def normmod(v):
        ms = jnp.mean(v * v, axis=-1, keepdims=True)
        y = v * lax.rsqrt(ms + EPS) * g_ref[...]
        return (y * (1.0 + sc_ref[...]) + sh_ref[...]).astype(BF16)

    x = x_ref[0]
    yb = normmod(x)
    pad = grid_w if vertical else 0
    ext_rows = tm + 2 * pad
    ye = jnp.concatenate([normmod(hp[0]), yb, normmod(hn[0])], axis=0) if vertical else yb
    col = lax.broadcasted_iota(jnp.int32, (ext_rows, cchunk), 0) % grid_w
    for j in range(0, D_FF, cchunk):
        cs = slice(j, j + cchunk)
        ext = _dot(ye, wg[:, cs])
        if vertical:
            ext = jnp.concatenate([jnp.where(t > 0, ext[:pad], 0.0), ext[pad:pad + tm],
                                   jnp.where(t < nt - 1, ext[pad + tm:], 0.0)], axis=0)
        left = jnp.where(col != 0, pltpu.roll(ext, 1, 0), 0.0)
        right = jnp.where(col != grid_w - 1, pltpu.roll(ext, ext_rows - 1, 0), 0.0)
        acc = jnp.zeros((tm, cchunk), F32) + cb[:, cs]
        for dy in ((-1, 0, 1) if vertical else (0,)):
            r0 = pad + dy * grid_w
            ky = dy + 1
            acc = acc + left[r0:r0 + tm] * cw[3 * ky:3 * ky + 1, cs]
            acc = acc + ext[r0:r0 + tm] * cw[3 * ky + 1:3 * ky + 2, cs]
            acc = acc + right[r0:r0 + tm] * cw[3 * ky + 2:3 * ky + 3, cs]
        act_ref[:, cs] = (_silu(acc) * _dot(yb, wu[:, cs])).astype(BF16)
    y = _dot(act_ref[...], wd[...])
    x = x + gate_ref[...] * y
    if final:
        ms = jnp.mean(x * x, axis=-1, keepdims=True)
        x = x * lax.rsqrt(ms + EPS) * fg_ref[...]
    o_ref[0] = x


def _ffn(h, mods, l, ctx_row, g, wg, wu, cw, cb, wd, tm, grid_w, vertical, final_g=None):
    bsz, seq, d = h.shape
    nhb = tm // grid_w
    tok = lambda n: pl.BlockSpec((1, tm, n), lambda b, t: (b, t, 0))
    full = lambda a: _layer_spec(a, l)
    resident = lambda a: pl.BlockSpec((None,) + tuple(a.shape[1:]), lambda b, t: (l, 0, 0),
                                      pipeline_mode=pl.Buffered(1))
    in_specs = [tok(d)]
    args = [h]
    if vertical:
        nrow = seq // grid_w
        in_specs += [
            pl.BlockSpec((1, grid_w, d), lambda b, t: (b, jnp.maximum(t * nhb - 1, 0), 0)),
            pl.BlockSpec((1, grid_w, d), lambda b, t: (b, jnp.minimum((t + 1) * nhb, nrow - 1), 0)),
        ]
        args += [h, h]
    in_specs += [_mod_spec(l, ctx_row, MOD_SHIFT2), _mod_spec(l, ctx_row, MOD_SCALE2), full(g),
                 resident(wg), resident(wu), full(cw), full(cb), resident(wd),
                 _mod_spec(l, ctx_row, MOD_GATE2)]
    args += [mods, mods, g, wg, wu, cw, cb, wd, mods]
    if final_g is not None:
        in_specs.append(pl.BlockSpec((1, d), lambda b, t: (0, 0)))
        args.append(final_g.reshape(1, d))
    return pl.pallas_call(
        functools.partial(_ffn_body, tm=tm, grid_w=grid_w, vertical=vertical,
                          final=final_g is not None, cchunk=256),
        grid=(bsz, seq // tm),
        in_specs=in_specs,
        out_specs=tok(d),
        out_shape=jax.ShapeDtypeStruct(h.shape, F32),
        scratch_shapes=[pltpu.VMEM((tm, D_FF), BF16)],
        compiler_params=_cparams(("parallel", "arbitrary"), big=True),
        name="ffn",
    )(*args)


def kernel(x, c, ctx, c_ctx, w_ada, b_ada, norm1_g, norm2_g, w_in, w_out, hg_lb_logits, hg_norm_g, hy_conv_w, hy_conv_b, hy_w1, hy_b1, hy_freq1, hy_w2, hy_b2, hy_freq2, hy_w3, hy_bias, ssd_conv_w, ssd_conv_b, ssd_dt_bias, ssd_a_log, ssd_d, ssd_norm_g, ffn_w_gate, ffn_w_up, ffn_conv_w, ffn_conv_b, ffn_w_down, final_norm_g):
    bsz, seq, d = x.shape
    depth = w_in.shape[0]
    cc = jnp.zeros((8, d), F32).at[:bsz].set(c).at[bsz].set(c_ctx)
    mods = _ada(cc, w_ada, b_ada).reshape(depth, 8, 6, 1, d)
    ctx_row = bsz

    consts = _dft_constants(seq)
    cs_ctx = _ctx_dft_constants()
    lb_logits = hg_lb_logits.astype(F32)

    row3 = lambda a: a.reshape(depth, 1, -1)
    rep = lambda a: jnp.repeat(a, SSD_HEAD_DIM, axis=-1)
    w_in_b = jnp.pad(w_in, ((0, 0), (0, 0), (0, IN_COLS_PAD - w_in.shape[-1]))).astype(BF16)
    w_out_b = w_out.astype(BF16)
    wg_b, wu_b, wd_b = ffn_w_gate.astype(BF16), ffn_w_up.astype(BF16), ffn_w_down.astype(BF16)
    g1, g2 = row3(norm1_g), row3(norm2_g)
    hcb, scb = row3(hy_conv_b), row3(ssd_conv_b)
    hgn, ssn = row3(hg_norm_g), row3(ssd_norm_g)
    cw_f, cb_f = ffn_conv_w.reshape(depth, 9, D_FF), row3(ffn_conv_b)
    prm = dict(dtb_e=rep(ssd_dt_bias), alog_e=rep(ssd_a_log),
               dtb_t=ssd_dt_bias.reshape(depth, -1, 1), alog_t=ssd_a_log.reshape(depth, -1, 1),
               dsk=row3(rep(ssd_d)))

    for l in range(depth):
        last = l == depth - 1
        proj = lambda h, row, tm, name: _inproj(h, mods, l, row, g1, w_in_b, hy_conv_w, hcb, ssd_conv_w, scb,
                                                tm, name)
        uhg_x, v_x, x1_x, x2_x, xbc_x, z_x, dt_x = proj(x, None, 512, "in_proj_x")
        uhg_c, v_c, x1_c, x2_c, xbc_c, z_c, dt_c = proj(ctx, ctx_row, CTX_LEN, "in_proj_ctx")

        of_x, ob_x, of_c, ob_c = _gla(uhg_x, uhg_c, lb_logits, l)

        dtt_x = jnp.swapaxes(dt_x[:, :, DT_LANE0:DT_LANE0 + 2 * SSD_HEADS], 1, 2)
        dtt_c = jnp.swapaxes(dt_c[:, :, DT_LANE0:DT_LANE0 + 2 * SSD_HEADS], 1, 2)
        yf_x, yb_x, yf_c, yb_c = _ssd(xbc_x, xbc_c, dt_x, dt_c, dtt_x, dtt_c, prm, l)

        fp = _hyena_filter_params(hy_w1[l], hy_b1[l], hy_freq1[l], hy_w2[l], hy_b2[l], hy_freq2[l], hy_w3[l])
        buf, sumabs = _filt_gen(fp, seq, 1024)
        hspec = _hyena_spectrum(buf.reshape(2, HY_HALVES, FFT_N1, FFT_N1, LANES), sumabs, consts, 2 * seq)
        ohy_x = _hyena_x(v_x, x1_x, x2_x, hspec, hy_bias.reshape(depth, 2, 1, HY_WIDTH), l, consts)

        x = _out_proj(of_x, ob_x, uhg_x, ohy_x, yf_x, yb_x, z_x, hgn, ssn, w_out_b, x, mods, l, None, 512)

        if not last:
            buf_c, sumabs_c = _filt_gen(fp, CTX_LEN, CTX_LEN)
            ohy_c = _hy_ctx(v_c, x1_c, x2_c, buf_c, sumabs_c, cs_ctx, hy_bias[l])
            ctx = _out_proj(of_c, ob_c, uhg_c, ohy_c, yf_c, yb_c, z_c, hgn, ssn, w_out_b, ctx, mods, l, ctx_row,
                            CTX_LEN)
            ctx = _ffn(ctx, mods, l, ctx_row, g2, wg_b, wu_b, cw_f, cb_f, wd_b, CTX_LEN, CTX_LEN, False)

        x = _ffn(x, mods, l, None, g2, wg_b, wu_b, cw_f, cb_f, wd_b, 1024, GRID_W, True,
                 final_g=final_norm_g if last else None)
    return x
```

```python
import functools
import math

import numpy as np
import jax
import jax.numpy as jnp
from jax import lax
from jax.experimental import pallas as pl
from jax.experimental.pallas import tpu as pltpu

F32 = jnp.float32
BF16 = jnp.bfloat16

D_MODEL = 1024
DEPTH = 2
CTX_LEN = 256
GRID_W = 64
EPS = 1e-6

HG_HEADS = 4
HG_DK = 64
HG_WIDTH = 256
HG_CHUNK = 32

HY_WIDTH = 256
HY_EMB_BANDS = 16
HY_HIDDEN = 64
HY_MIN_DECAY = math.log(1e-2) / 1.5
HY_MAX_DECAY = math.log(1e-2) / 0.3

SSD_HEADS = 8
SSD_HEAD_DIM = 64
SSD_WIDTH = 512
SSD_GROUPS = 2
SSD_STATE = 128
SSD_XBC = 1024
SSD_CHUNK = 64

D_FF = 2816
HG_COLS = 5 * HG_WIDTH
HY_COLS = 3 * HY_WIDTH

LANES = 128
HY_HALVES = HY_WIDTH // LANES
TOKEN_BLOCK = 256
SCAN_BATCH_ROWS = 2
FFT_N1 = 128
VMEM_LIMIT = 56 * 1024 * 1024


def _cparams(sem, big=False):
    kw = dict(dimension_semantics=sem)
    if big:
        kw["vmem_limit_bytes"] = VMEM_LIMIT
    return pltpu.CompilerParams(**kw)


def _dot(a, b):
    return lax.dot_general(a, b, (((1,), (0,)), ((), ())), preferred_element_type=F32)


def _dot_nt(a, b):
    return lax.dot_general(a, b, (((1,), (1,)), ((), ())), preferred_element_type=F32)


def _dot_tn(a, b):
    return lax.dot_general(a, b, (((0,), (0,)), ((), ())), preferred_element_type=F32)


def _split3(a):
    hi = a.astype(BF16)
    r = a - hi.astype(F32)
    mid = r.astype(BF16)
    lo = (r - mid.astype(F32)).astype(BF16)
    return hi, mid, lo


def _dot_exact_lhs(mask_bf16, a):
    h, m, l = _split3(a)
    return _dot(mask_bf16, h) + _dot(mask_bf16, m) + _dot(mask_bf16, l)


def _dot_exact_rhs(a, mask_bf16):
    h, m, l = _split3(a)
    return _dot(h, mask_bf16) + _dot(m, mask_bf16) + _dot(l, mask_bf16)


def _dot3(a, b):
    ah, am, _ = _split3(a)
    bh, bm, _ = _split3(b)
    return _dot(ah, bh) + _dot(ah, bm) + _dot(am, bh)


def _dot3_pre(fparts, b):
    fh, fm = fparts
    bh, bm, _ = _split3(b)
    return _dot(fh, bh) + _dot(fh, bm) + _dot(fm, bh)


def _sigmoid(x):
    return 1.0 / (1.0 + jnp.exp(-x))


def _silu(x):
    return x * _sigmoid(x)


def _softplus(x):
    return jnp.maximum(x, 0.0) + jnp.log(1.0 + jnp.exp(-jnp.abs(x)))


def _log_sigmoid(x):
    return jnp.minimum(x, 0.0) - jnp.log(1.0 + jnp.exp(-jnp.abs(x)))


def _ada_body(c_ref, w_ref, b_ref, o_ref):
    cc = c_ref[...]
    o_ref[0] = _dot3(_silu(cc), w_ref[0]) + b_ref[0]


def _ada(cc, w_ada, b_ada):
    tn = 1536
    n = w_ada.shape[-1]
    return pl.pallas_call(
        _ada_body,
        grid=(DEPTH, n // tn),
        in_specs=[
            pl.BlockSpec((8, D_MODEL), lambda l, j: (0, 0)),
            pl.BlockSpec((1, D_MODEL, tn), lambda l, j: (l, 0, j)),
            pl.BlockSpec((1, 1, tn), lambda l, j: (l, 0, j)),
        ],
        out_specs=pl.BlockSpec((1, 8, tn), lambda l, j: (l, 0, j)),
        out_shape=jax.ShapeDtypeStruct((DEPTH, 8, n), F32),
        compiler_params=_cparams(("arbitrary", "arbitrary"), big=True),
        name="adaln",
    )(cc, w_ada, b_ada.reshape(DEPTH, 1, n))


IN_COLS = HG_COLS + HY_COLS + SSD_WIDTH + SSD_XBC + 2 * SSD_HEADS
COL_HY = HG_COLS
COL_Z = COL_HY + HY_COLS
COL_XBC = COL_Z + SSD_WIDTH
COL_DT = COL_XBC + SSD_XBC
IN_COLS_PAD = COL_DT + LANES
DT_LANE0 = 0
CONV_CHUNK = 256

MOD_SHIFT1, MOD_SCALE1, MOD_GATE1, MOD_SHIFT2, MOD_SCALE2, MOD_GATE2 = range(6)


def _layer_spec(a, l):
    nd = a.ndim - 1
    return pl.BlockSpec((None,) + tuple(a.shape[1:]), lambda b, t: (l,) + (0,) * nd)


def _mod_spec(l, ctx_row, idx):
    row = (lambda b: b) if ctx_row is None else (lambda b: ctx_row)
    return pl.BlockSpec((None, None, None, 1, D_MODEL), lambda b, t: (l, row(b), idx, 0, 0))


def _inproj_body(x_ref, hp_ref, hn_ref, sh_ref, sc_ref, g_ref, w_ref, hcw, hcb, scw, scb,
                 uhg_ref, v_ref, x1_ref, x2_ref, xbc_ref, z_ref, dt_ref, *, nt):
    t = pl.program_id(1)

    def normmod(x):
        ms = jnp.mean(x * x, axis=-1, keepdims=True)
        y = x * lax.rsqrt(ms + EPS) * g_ref[...]
        return (y * (1.0 + sc_ref[...]) + sh_ref[...]).astype(BF16)

    yb = normmod(x_ref[0])
    rows = yb.shape[0]
    ye = jnp.concatenate([yb, normmod(jnp.concatenate([hp_ref[0], hn_ref[0]], axis=0))], axis=0)
    def plain(ref, c0, c1, o0):
        def run():
            ref[0, :, o0:o0 + (c1 - c0)] = _dot(yb, w_ref[:, c0:c1])
        return run

    plain_tasks = [plain(uhg_ref, j, j + CONV_CHUNK, j) for j in range(0, HG_COLS, CONV_CHUNK)]
    plain_tasks += [plain(z_ref, COL_Z + j, COL_Z + j + CONV_CHUNK, j) for j in range(0, SSD_WIDTH, CONV_CHUNK)]
    plain_tasks += [plain(dt_ref, COL_DT, COL_DT + LANES, 0)]

    ri = lax.broadcasted_iota(jnp.int32, (rows, CONV_CHUNK), 0)

    def conv_chunk(col0, cw_ref, cb_ref, k):
        cols = slice(col0 + k * CONV_CHUNK, col0 + (k + 1) * CONV_CHUNK)
        ccols = slice(k * CONV_CHUNK, (k + 1) * CONV_CHUNK)
        ue = _dot(ye, w_ref[:, cols])
        if plain_tasks:
            plain_tasks.pop(0)()
        u = ue[:rows]
        prev_row = jnp.where(t > 0, ue[rows + 7:rows + 8, :], 0.0)
        next_row = jnp.where(t < nt - 1, ue[rows + 8:rows + 9, :], 0.0)
        dn = jnp.where(ri == 0, prev_row, pltpu.roll(u, 1, 0))
        up = jnp.where(ri == rows - 1, next_row, pltpu.roll(u, rows - 1, 0))
        return dn * cw_ref[0:1, ccols] + u * cw_ref[1:2, ccols] + up * cw_ref[2:3, ccols] + cb_ref[:, ccols]

    for k, ref in enumerate((v_ref, x1_ref, x2_ref)):
        cv = conv_chunk(COL_HY, hcw, hcb, k)
        for hf in range(HY_HALVES):
            ref[hf, 0] = cv[:, hf * LANES:(hf + 1) * LANES]
    for k in range(SSD_XBC // CONV_CHUNK):
        cv = conv_chunk(COL_XBC, scw, scb, k)
        xbc_ref[0, :, k * CONV_CHUNK:(k + 1) * CONV_CHUNK] = _silu(cv).astype(xbc_ref.dtype)
    for task in plain_tasks:
        task()


def _inproj(x, mods, l, ctx_row, g, w, hcw, hcb, scw, scb, tm, name):
    bsz, seq, d = x.shape
    nt = seq // tm
    hb = tm // 8
    tok = lambda n: pl.BlockSpec((1, tm, n), lambda b, t: (b, t, 0))
    hy_spec = pl.BlockSpec((HY_HALVES, 1, tm, LANES), lambda b, t: (0, b, t, 0))
    hy_shape = jax.ShapeDtypeStruct((HY_HALVES, bsz, seq, LANES), F32)
    tshape = lambda n: jax.ShapeDtypeStruct((bsz, seq, n), F32)
    return pl.pallas_call(
        functools.partial(_inproj_body, nt=nt),
        grid=(bsz, nt),
        in_specs=[
            tok(d),
            pl.BlockSpec((1, 8, d), lambda b, t: (b, jnp.maximum(t * hb - 1, 0), 0)),
            pl.BlockSpec((1, 8, d), lambda b, t: (b, jnp.minimum((t + 1) * hb, nt * hb - 1), 0)),
            _mod_spec(l, ctx_row, MOD_SHIFT1), _mod_spec(l, ctx_row, MOD_SCALE1),
            _layer_spec(g, l), _layer_spec(w, l),
            _layer_spec(hcw, l), _layer_spec(hcb, l), _layer_spec(scw, l), _layer_spec(scb, l),
        ],
        out_specs=[tok(HG_COLS), hy_spec, hy_spec, hy_spec, tok(SSD_XBC), tok(SSD_WIDTH), tok(LANES)],
        out_shape=[tshape(HG_COLS), hy_shape, hy_shape, hy_shape,
                   jax.ShapeDtypeStruct((bsz, seq, SSD_XBC), BF16), tshape(SSD_WIDTH), tshape(LANES)],
        compiler_params=_cparams(("parallel", "arbitrary"), big=True),
        name=name,
    )(x, x, x, mods, mods, g, w, hcw, hcb, scw, scb)


def _gla_dir(q, a, v, p, r1, st_ref, reverse):
    tb, ch = TOKEN_BLOCK, HG_CHUNK
    nch = tb // ch
    qv = r1 + _log_sigmoid(a)
    logf = jnp.maximum(p, qv) + jnp.log(1.0 + jnp.exp(-jnp.abs(p - qv)))
    k = 1.0 - jnp.exp(logf)

    ri = lax.broadcasted_iota(jnp.int32, (tb, tb), 0)
    ci = lax.broadcasted_iota(jnp.int32, (tb, tb), 1)
    same = (ri // ch) == (ci // ch)
    tri = (ci >= ri) if reverse else (ci <= ri)
    mask = jnp.logical_and(same, tri)
    mask_b = jnp.where(mask, 1.0, 0.0).astype(BF16)
    bdiag = (ri // HG_DK) == (ci // HG_DK)

    lh, lm, ll = _split3(logf)
    bcs3 = _dot(mask_b, jnp.concatenate([lh, lm, ll], axis=1))
    bcs = bcs3[:, :tb] + bcs3[:, tb:2 * tb] + bcs3[:, 2 * tb:]
    yield None
    b3 = bcs.reshape(nch, ch, tb)
    mid = ch // 2 if reverse else ch // 2 - 1
    end = 0 if reverse else ch - 1
    b_mid = b3[:, mid:mid + 1, :]
    b_end = b3[:, end:end + 1, :]
    q3 = q.reshape(nch, ch, tb)
    k3 = k.reshape(nch, ch, tb)
    d1 = b3 - b_mid
    qd = (q3 * jnp.exp(d1)).reshape(tb, tb)
    kd = (k3 * jnp.exp(-d1)).reshape(tb, tb).astype(BF16)
    kup = (k3 * jnp.exp(b_end - b3)).astype(BF16)
    qb = (q3 * jnp.exp(b3)).astype(BF16)
    dec = jnp.exp(b_end)
    yield None

    lane_head = lax.broadcasted_iota(jnp.int32, (1, tb), 1) // HG_DK
    acc = jnp.zeros((tb, tb), F32)
    for h in range(HG_HEADS):
        sel = lane_head == h
        qh = jnp.where(sel, qd, 0.0).astype(BF16)
        s = _dot_nt(qh, kd)
        s = jnp.where(mask, s, 0.0).astype(BF16)
        vh = jnp.where(sel, v, 0.0).astype(BF16)
        acc = acc + _dot(s, vh)
        yield None

    v3 = v.astype(BF16).reshape(nch, ch, tb)
    inter = [None] * nch
    order = range(nch - 1, -1, -1) if reverse else range(nch)
    st = st_ref[...]
    for c in order:
        inter[c] = _dot_nt(qb[c], st.astype(BF16))
        upd = _dot_tn(v3[c], kup[c])
        st = st * dec[c] + jnp.where(bdiag, upd, 0.0)
        yield None
    st_ref[...] = st
    yield acc + jnp.concatenate(inter, axis=0)


def _interleave(gens, lag=3):
    results = [None] * len(gens)
    done = [False] * len(gens)
    step = 0
    while not all(done):
        for i, g in enumerate(gens):
            if done[i] or step < i * lag:
                continue
            try:
                results[i] = next(g)
            except StopIteration:
                done[i] = True
        step += 1
    return results


def _gla_body(qf, ff, vf, qb, fb, vb, qc, ffc, fbc, vc, lb_ref, of_x, ob_x, of_c, ob_c, stf, stb, *, layer):
    t = pl.program_id(1)
    is_ctx = t == 0

    @pl.when(is_ctx)
    def _():
        stf[...] = jnp.zeros_like(stf)
        stb[...] = jnp.zeros_like(stb)

    lg = lb_ref[...]
    e = jnp.exp(lg - jnp.max(lg, axis=0, keepdims=True))
    den = jnp.sum(e, axis=0, keepdims=True)
    num = jnp.zeros_like(den)
    for r in range(1, layer + 1):
        num = num + e[r:r + 1, :]
    lb = num / den
    p = jnp.log(lb)
    r1 = jnp.log(1.0 - lb)

    scale = HG_DK ** -0.5
    nb = qf.shape[0]
    gens = []
    for b in range(nb):
        gens.append(_gla_dir(jnp.where(is_ctx, qc[b], qf[b]) * scale, jnp.where(is_ctx, ffc[b], ff[b]),
                             jnp.where(is_ctx, vc[b], vf[b]), p, r1, stf.at[b], False))
        gens.append(_gla_dir(jnp.where(is_ctx, qc[b], qb[b]) * scale, jnp.where(is_ctx, fbc[b], fb[b]),
                             jnp.where(is_ctx, vc[b], vb[b]), p, r1, stb.at[b], True))
    outs = _interleave(gens, lag=1)

    @pl.when(is_ctx)
    def _():
        for b in range(nb):
            of_c[b] = outs[2 * b].astype(of_c.dtype)
            ob_c[b] = outs[2 * b + 1].astype(ob_c.dtype)

    @pl.when(jnp.logical_not(is_ctx))
    def _():
        for b in range(nb):
            of_x[b] = outs[2 * b].astype(of_x.dtype)
            ob_x[b] = outs[2 * b + 1].astype(ob_x.dtype)


def _scan_block_maps(nxb):
    fwd = lambda t: jnp.maximum(t - 1, 0)
    bwd = lambda t: jnp.where(t == 0, nxb - 1, nxb - t)
    return fwd, bwd


def _gla(u_x, u_c, lb_logits, layer):
    bsz, seq, _ = u_x.shape
    tb = TOKEN_BLOCK
    nxb = seq // tb
    xf, xb = _scan_block_maps(nxb)

    nb = SCAN_BATCH_ROWS

    def xs(blk, col):
        return pl.BlockSpec((nb, tb, HG_WIDTH), lambda b, t: (b, blk(t), col))

    def cs(col):
        return pl.BlockSpec((nb, tb, HG_WIDTH), lambda b, t: (b, 0, col))

    out_x = jax.ShapeDtypeStruct((bsz, seq, HG_WIDTH), BF16)
    out_c = jax.ShapeDtypeStruct((bsz, CTX_LEN, HG_WIDTH), BF16)
    state = pltpu.VMEM((nb, HG_WIDTH, HG_WIDTH), F32)
    return pl.pallas_call(
        functools.partial(_gla_body, layer=layer),
        grid=(bsz // nb, nxb + 1),
        in_specs=[xs(xf, 0), xs(xf, 1), xs(xf, 3), xs(xb, 0), xs(xb, 2), xs(xb, 3),
                  cs(0), cs(1), cs(2), cs(3),
                  pl.BlockSpec((DEPTH, HG_WIDTH), lambda b, t: (0, 0))],
        out_specs=[xs(xf, 0), xs(xb, 0), cs(0), cs(0)],
        out_shape=[out_x, out_x, out_c, out_c],
        scratch_shapes=[state, state],
        compiler_params=_cparams(("parallel", "arbitrary"), big=True),
        name="hgrn2_scan",
    )(u_x, u_x, u_x, u_x, u_x, u_x, u_c, u_c, u_c, u_c, lb_logits)


def _ssd_dir(xbc, dt_raw, dtt_all, dtb_e, alog_e, dtb_t, alog_t, dsk, st_ref, reverse):
    tb, ch = TOKEN_BLOCK, SSD_CHUNK
    nch = tb // ch
    d = 1 if reverse else 0
    xs = xbc[:, :SSD_WIDTH].astype(F32)
    bm = xbc[:, SSD_WIDTH:SSD_WIDTH + 256].astype(BF16)
    cm = xbc[:, SSD_WIDTH + 256:].astype(BF16)

    ri = lax.broadcasted_iota(jnp.int32, (tb, tb), 0)
    ci = lax.broadcasted_iota(jnp.int32, (tb, tb), 1)
    same = (ri // ch) == (ci // ch)
    mask = jnp.logical_and(same, (ci >= ri) if reverse else (ci <= ri))
    mask_b = jnp.where(mask, 1.0, 0.0).astype(BF16)
    mask_t = jnp.logical_and(same, (ri >= ci) if reverse else (ri <= ci))
    mask_tb = jnp.where(mask_t, 1.0, 0.0).astype(BF16)

    ej = lax.broadcasted_iota(jnp.int32, (128, SSD_WIDTH), 0)
    el = lax.broadcasted_iota(jnp.int32, (128, SSD_WIDTH), 1)
    expand = jnp.where(ej == DT_LANE0 + d * SSD_HEADS + el // SSD_HEAD_DIM, 1.0, 0.0).astype(BF16)
    dh, dm, dl = _split3(dt_raw)
    dte_raw = _dot(jnp.concatenate([dh, dm, dl], axis=1), jnp.concatenate([expand] * 3, axis=0))
    dte = _softplus(dte_raw + dtb_e[d:d + 1, :])
    a_e = -jnp.exp(alog_e[d:d + 1, :])
    ah, am, al = _split3(dte * a_e)
    acs3p = _dot(mask_b, jnp.concatenate([ah, am, al], axis=1))
    acs = acs3p[:, :SSD_WIDTH] + acs3p[:, SSD_WIDTH:2 * SSD_WIDTH] + acs3p[:, 2 * SSD_WIDTH:]
    yield None
    xdt = xs * dte
    acs3 = acs.reshape(nch, ch, SSD_WIDTH)
    end = 0 if reverse else ch - 1
    a_end = acs3[:, end:end + 1, :]
    xw = (xdt.reshape(nch, ch, SSD_WIDTH) * jnp.exp(a_end - acs3)).astype(BF16).reshape(tb, SSD_WIDTH)
    ea = jnp.exp(acs)
    dec = jnp.exp(a_end)

    dtt_raw = dtt_all[d * SSD_HEADS:(d + 1) * SSD_HEADS, :]
    dtt = _softplus(dtt_raw + dtb_t[d * SSD_HEADS:(d + 1) * SSD_HEADS, :])
    a_t = -jnp.exp(alog_t[d * SSD_HEADS:(d + 1) * SSD_HEADS, :])
    th, tm_, tl = _split3(dtt * a_t)
    acs_t3 = _dot(jnp.concatenate([th, tm_, tl], axis=0), mask_tb)
    acs_t = acs_t3[:SSD_HEADS] + acs_t3[SSD_HEADS:2 * SSD_HEADS] + acs_t3[2 * SSD_HEADS:]
    yield None

    def chunk_diag(m):
        z = jnp.zeros((ch, m.shape[1]), m.dtype)
        return jnp.concatenate(
            [jnp.concatenate([m[c * ch:(c + 1) * ch] if k == c else z for k in range(nch)], axis=1)
             for c in range(nch)], axis=0)

    xdt_b = xdt.astype(BF16)
    hd = SSD_HEAD_DIM
    hpg = SSD_HEADS // SSD_GROUPS
    gw = SSD_WIDTH // SSD_GROUPS
    rr = lax.broadcasted_iota(jnp.int32, (tb, gw), 0)
    ll = lax.broadcasted_iota(jnp.int32, (tb, gw), 1)
    tri4 = ((ll % hd) >= (rr % ch)) if reverse else ((ll % hd) <= (rr % ch))
    head_diag = (rr // hd) == (ll // hd)
    order = range(nch - 1, -1, -1) if reverse else range(nch)
    outs = []
    for g in range(SSD_GROUPS):
        st_cols = slice(g * SSD_STATE, (g + 1) * SSD_STATE)
        cols = slice(g * gw, (g + 1) * gw)
        cm_d = chunk_diag(cm[:, st_cols])
        bm_cat = jnp.concatenate([bm[c * ch:(c + 1) * ch, st_cols] for c in range(nch)], axis=1)
        cb4 = _dot_nt(cm_d, jnp.concatenate([bm_cat] * hpg, axis=0))
        rowm = jnp.concatenate(
            [jnp.broadcast_to(jnp.concatenate([acs_t[g * hpg + h:g * hpg + h + 1, c * ch:(c + 1) * ch]
                                               for h in range(hpg)], axis=1), (ch, gw))
             for c in range(nch)], axis=0)
        w4 = (cb4 * jnp.exp(jnp.where(tri4, acs[:, cols] - rowm, -1e30))).astype(BF16)
        xg = xdt_b[:, cols]
        zero = jnp.zeros((hpg * ch, gw), BF16)
        x4 = jnp.concatenate(
            [jnp.where(head_diag, jnp.concatenate([xg[c * ch:(c + 1) * ch]] * hpg, axis=0), zero)
             for c in range(nch)], axis=0)
        y_intra = _dot(chunk_diag(w4), x4)
        yield None

        upd = _dot_tn(bm[:, st_cols], chunk_diag(xw[:, cols]))
        st = st_ref[g]
        entering = [None] * nch
        for c in order:
            entering[c] = st.astype(BF16)
            st = st * dec[c][:, cols] + upd[:, c * gw:(c + 1) * gw]
        st_ref[g] = st
        y_inter = _dot(cm_d, jnp.concatenate(entering, axis=0)) * ea[:, cols]
        outs.append(y_intra + y_inter)
        yield None
    o = jnp.concatenate(outs, axis=1)
    if not reverse:
        o = o + dsk[...] * xs
    yield o


def _ssd_body(xf, dtf, dttf, xb, dtb, dttb, xc, dtc, dttc, dtb_e, alog_e, dtb_t, alog_t, dsk,
              of_x, ob_x, of_c, ob_c, stf, stb):
    t = pl.program_id(1)
    is_ctx = t == 0

    @pl.when(is_ctx)
    def _():
        stf[...] = jnp.zeros_like(stf)
        stb[...] = jnp.zeros_like(stb)

    prm = (dtb_e, alog_e, dtb_t, alog_t, dsk)
    nb = xf.shape[0]
    gens = []
    for b in range(nb):
        gens.append(_ssd_dir(jnp.where(is_ctx, xc[b], xf[b]), jnp.where(is_ctx, dtc[b], dtf[b]),
                             jnp.where(is_ctx, dttc[b], dttf[b]), *prm, stf.at[b], False))
        gens.append(_ssd_dir(jnp.where(is_ctx, xc[b], xb[b]), jnp.where(is_ctx, dtc[b], dtb[b]),
                             jnp.where(is_ctx, dttc[b], dttb[b]), *prm, stb.at[b], True))
    outs = _interleave(gens, lag=2)

    @pl.when(is_ctx)
    def _():
        for b in range(nb):
            of_c[b] = outs[2 * b].astype(of_c.dtype)
            ob_c[b] = outs[2 * b + 1].astype(ob_c.dtype)

    @pl.when(jnp.logical_not(is_ctx))
    def _():
        for b in range(nb):
            of_x[b] = outs[2 * b].astype(of_x.dtype)
            ob_x[b] = outs[2 * b + 1].astype(ob_x.dtype)


def _ssd(xbc_x, xbc_c, dt_x, dt_c, dtt_x, dtt_c, prm, l):
    bsz, seq, _ = xbc_x.shape
    tb = TOKEN_BLOCK
    nxb = seq // tb
    xf, xb = _scan_block_maps(nxb)
    full = lambda a: _layer_spec(a, l)
    params = [prm["dtb_e"], prm["alog_e"], prm["dtb_t"], prm["alog_t"], prm["dsk"]]

    nb = SCAN_BATCH_ROWS

    def xspecs(blk):
        return [pl.BlockSpec((nb, tb, SSD_XBC), lambda b, t: (b, blk(t), 0)),
                pl.BlockSpec((nb, tb, 128), lambda b, t: (b, blk(t), 0)),
                pl.BlockSpec((nb, 16, tb), lambda b, t: (b, 0, blk(t)))]

    zero = lambda t: 0
    yspec = lambda blk: pl.BlockSpec((nb, tb, SSD_WIDTH), lambda b, t: (b, blk(t), 0))
    out_x = jax.ShapeDtypeStruct((bsz, seq, SSD_WIDTH), BF16)
    out_c = jax.ShapeDtypeStruct((bsz, CTX_LEN, SSD_WIDTH), BF16)
    state = pltpu.VMEM((nb, SSD_GROUPS, SSD_STATE, SSD_WIDTH // SSD_GROUPS), F32)
    return pl.pallas_call(
        _ssd_body,
        grid=(bsz // nb, nxb + 1),
        in_specs=xspecs(xf) + xspecs(xb) + xspecs(zero) + [full(a) for a in params],
        out_specs=[yspec(xf), yspec(xb), yspec(zero), yspec(zero)],
        out_shape=[out_x, out_x, out_c, out_c],
        scratch_shapes=[state, state],
        compiler_params=_cparams(("parallel", "arbitrary"), big=True),
        name="ssd_scan",
    )(xbc_x, dt_x, dtt_x, xbc_x, dt_x, dtt_x, xbc_c, dt_c, dtt_c, *params)


def _filt_body(w1, b1, f1, w2, b2, f2, w3, delta, buf_ref, sa_ref, *, seq_len, tr):
    i = pl.program_id(0)
    hr = tr // 2
    lane = lax.broadcasted_iota(jnp.int32, (hr, 128), 1)
    grp, sub = lane // HY_HIDDEN, lane % HY_HIDDEN
    jrow = i * tr + grp * hr + lax.broadcasted_iota(jnp.int32, (hr, 128), 0)
    pos = jnp.where(jrow < seq_len, jrow, 2 * seq_len - jrow).astype(F32)
    tpos = pos / (seq_len - 1.0)
    wpos = pos * (2.0 * math.pi / seq_len)
    bidx = jnp.where(sub <= HY_EMB_BANDS, sub - 1, sub - 1 - HY_EMB_BANDS).astype(F32)
    band = 1e-4 + bidx * ((HY_EMB_BANDS - 1 - 1e-4) / (HY_EMB_BANDS - 1))
    ang = band * wpos + jnp.where(sub > HY_EMB_BANDS, 0.5 * math.pi, 0.0)
    z = jnp.where(sub == 0, tpos, jnp.where(sub <= 2 * HY_EMB_BANDS, jnp.cos(ang), 0.0))
    h = jnp.sin(f1[...] * (_dot3(z, w1[...]) + b1[...]))
    h = jnp.sin(f2[...] * (_dot3(h, w2[...]) + b2[...]))

    @pl.when(i == 0)
    def _():
        sa_ref[...] = jnp.zeros_like(sa_ref)

    for g in range(2):
        o = _dot3(h, w3[0, g])
        o = o * jnp.exp(-tpos[:, g * HY_HIDDEN:g * HY_HIDDEN + 1] * delta[...])
        o = jnp.where(jrow[:, g * HY_HIDDEN:g * HY_HIDDEN + 1] == seq_len, 0.0, o)
        for order in range(2):
            for hf in range(HY_HALVES):
                lo = order * HY_WIDTH + hf * LANES
                buf_ref[order, hf, g * hr:(g + 1) * hr, :] = o[:, lo:lo + LANES]
        sa_ref[...] = sa_ref[...] + jnp.sum(jnp.abs(o), axis=0, keepdims=True)


def _filt_gen(p, seq_len, tr):
    n = 2 * seq_len
    assert seq_len % tr == 0
    half = seq_len // tr
    full = lambda a: pl.BlockSpec(a.shape, lambda i: (0,) * a.ndim)
    small = [p["w1"], p["b1"], p["f1"], p["w2"], p["b2"], p["f2"]]
    return pl.pallas_call(
        functools.partial(_filt_body, seq_len=seq_len, tr=tr),
        grid=(n // tr,),
        in_specs=[full(a) for a in small] + [
            pl.BlockSpec((1, 2, 2 * HY_HIDDEN, 2 * HY_WIDTH), lambda i: (i // half, 0, 0, 0)),
            full(p["delta"]),
        ],
        out_specs=[pl.BlockSpec((2, HY_HALVES, tr, LANES), lambda i: (0, 0, i, 0)),
                   pl.BlockSpec((8, 2 * HY_WIDTH), lambda i: (0, 0))],
        out_shape=[jax.ShapeDtypeStruct((2, HY_HALVES, n, LANES), F32),
                   jax.ShapeDtypeStruct((8, 2 * HY_WIDTH), F32)],
        compiler_params=_cparams(("arbitrary",)),
        name="hyena_filter",
    )(*small, p["w3"], p["delta"])


FFT_NB = 8


def _strided_rows(ref3, j, n):
    flat = ref3.reshape(ref3.shape[0] * FFT_NB, LANES)
    return flat[pl.ds(j, n, stride=FFT_NB), :]


def _store_strided_rows(ref3, j, val):
    flat = ref3.reshape(ref3.shape[0] * FFT_NB, LANES)
    flat[pl.ds(j, val.shape[0], stride=FFT_NB), :] = val


def _load_complex(ref, k):
    return jnp.concatenate(
        [jnp.concatenate([ref[hf, 0, k], ref[hf, 1, k]], axis=0) for hf in range(HY_HALVES)], axis=1)


def _store_complex(ref, k, val):
    for hf in range(HY_HALVES):
        ref[hf, 0, k] = val[:FFT_N1, hf * LANES:(hf + 1) * LANES].astype(ref.dtype)
        ref[hf, 1, k] = val[FFT_N1:, hf * LANES:(hf + 1) * LANES].astype(ref.dtype)


FFT_TILES = FFT_N1 // FFT_NB
TILE_ROWS = 2 * FFT_N1 * FFT_NB


def _spec_stage1(x_refs, f_ref, t_ref, scr, j, rows):
    for i in range(FFT_NB):
        x = jnp.concatenate([_strided_rows(r, i, rows) for r in x_refs], axis=1)
        a = _dot3_pre((f_ref[0], f_ref[1]), x)
        tr = t_ref[0, 0, :, i:i + 1]
        ti = t_ref[0, 1, :, i:i + 1]
        ar, ai = a[:FFT_N1], a[FFT_N1:]
        out = jnp.concatenate([ar * tr - ai * ti, ar * ti + ai * tr], axis=0)
        for hf in range(HY_HALVES):
            scr[hf, pl.ds(j * TILE_ROWS + i, 2 * FFT_N1, stride=FFT_NB), :] = out[:, hf * LANES:(hf + 1) * LANES]


def _spec_slab(scr, k1):
    def part(hf, p):
        r0 = (p * FFT_N1 + k1) * FFT_NB
        return jnp.concatenate(
            [scr[hf, pl.ds(pl.multiple_of(jj * TILE_ROWS + r0, FFT_NB), FFT_NB), :] for jj in range(FFT_TILES)], axis=0)
    return jnp.concatenate(
        [jnp.concatenate([part(hf, 0), part(hf, 1)], axis=0) for hf in range(HY_HALVES)], axis=1)


def _spec_slab_store(scr, k1, val):
    for hf in range(HY_HALVES):
        for p in range(2):
            r0 = (p * FFT_N1 + k1) * FFT_NB
            for jj in range(FFT_TILES):
                scr[hf, pl.ds(pl.multiple_of(jj * TILE_ROWS + r0, FFT_NB), FFT_NB), :] = val[
                    p * FFT_N1 + jj * FFT_NB:p * FFT_N1 + (jj + 1) * FFT_NB, hf * LANES:(hf + 1) * LANES]


def _hconv_body(y_ref, h_ref, gate_ref, f1_ref, f2_ref, f2c_ref, f3_ref, t_ref, bias_ref, o_ref, scr, *, kb):
    s = pl.program_id(0)
    rows = y_ref.shape[1]

    @pl.when(s < FFT_TILES)
    def _():
        _spec_stage1([y_ref.at[hf] for hf in range(HY_HALVES)], f1_ref, t_ref, scr, s, rows)

    @pl.when(jnp.logical_and(s >= FFT_TILES, s < 2 * FFT_TILES))
    def _():
        kt = s - FFT_TILES
        for k in range(kb):
            k1 = kt * kb + k
            x = _dot3_pre((f2_ref[0], f2_ref[1]), _spec_slab(scr, k1))
            xr, xi = x[:FFT_N1], x[FFT_N1:]
            h = _load_complex(h_ref, k).astype(F32)
            hr, hi = h[:FFT_N1], h[FFT_N1:]
            y = jnp.concatenate([xr * hr - xi * hi, xr * hi + xi * hr], axis=0)
            _spec_slab_store(scr, k1, _dot3_pre((f2c_ref[0], f2c_ref[1]), y))

    @pl.when(s >= 2 * FFT_TILES)
    def _():
        j = s - 2 * FFT_TILES
        for i in range(FFT_NB):
            tr = t_ref[0, 0, :, i:i + 1]
            ti = t_ref[0, 1, :, i:i + 1]
            c = jnp.concatenate(
                [scr[hf, pl.ds(j * TILE_ROWS + i, 2 * FFT_N1, stride=FFT_NB), :] for hf in range(HY_HALVES)], axis=1)
            cr, ci = c[:FFT_N1], c[FFT_N1:]
            dmat = jnp.concatenate([cr * tr + ci * ti, ci * tr - cr * ti], axis=0)
            w = _dot3_pre((f3_ref[0], f3_ref[1]), dmat)
            for hf in range(HY_HALVES):
                lanes = slice(hf * LANES, (hf + 1) * LANES)
                g = _strided_rows(gate_ref.at[hf], i, rows)
                v = _strided_rows(y_ref.at[hf], i, rows)
                _store_strided_rows(o_ref.at[hf], i, g * (w[:, lanes] + v * bias_ref[:, lanes]))


def _hyena_conv(y, gate, hspec, bias, l, order, consts, kb=8):
    nh, rows, n2, lanes = y.shape
    nt = FFT_TILES
    tile = lambda f: pl.BlockSpec((nh, rows, FFT_NB, lanes), lambda s: (0, 0, f(s), 0))
    in_tile = lambda s: jnp.where(s < nt, s, jnp.maximum(s - 2 * nt, 0))
    out_tile = lambda s: jnp.maximum(s - 2 * nt, 0)
    const = lambda a: pl.BlockSpec(a.shape, lambda s: (0,) * a.ndim)
    return pl.pallas_call(
        functools.partial(_hconv_body, kb=kb),
        grid=(3 * nt,),
        in_specs=[
            tile(in_tile),
            pl.BlockSpec((None, nh, 2, kb, FFT_N1, lanes),
                         lambda s: (order, 0, 0, jnp.clip(s - nt, 0, FFT_N1 // kb - 1), 0, 0)),
            tile(out_tile),
            const(consts["f1_data"]), const(consts["f2"]), const(consts["f2c"]), const(consts["f3"]),
            pl.BlockSpec((1, 2, FFT_N1, FFT_NB), lambda s: (in_tile(s), 0, 0, 0)),
            pl.BlockSpec((None, None, 1, HY_WIDTH), lambda s: (l, order, 0, 0)),
        ],
        out_specs=tile(out_tile),
        out_shape=jax.ShapeDtypeStruct(y.shape, F32),
        scratch_shapes=[pltpu.VMEM((nh, nt * TILE_ROWS, lanes), F32)],
        compiler_params=_cparams(("arbitrary",), big=True),
        name="hyena_conv",
    )(y, hspec, gate, consts["f1_data"], consts["f2"], consts["f2c"], consts["f3"], consts["twid"], bias)


def _hspec_body(b_ref, sa_ref, f1_ref, f2_ref, t_ref, o_ref, scr, *, kb, n):
    o = pl.program_id(0)
    s = pl.program_id(1)
    rows = b_ref.shape[1]

    @pl.when(s < FFT_TILES)
    def _():
        _spec_stage1([b_ref.at[hf] for hf in range(HY_HALVES)], f1_ref, t_ref, scr, s, rows)

    @pl.when(s >= FFT_TILES)
    def _():
        sa = jnp.where(o == 0, sa_ref[0:1, :HY_WIDTH], sa_ref[0:1, HY_WIDTH:])
        scale = 1.0 / (sa * float(n))
        kt = s - FFT_TILES
        for k in range(kb):
            x = _dot3_pre((f2_ref[0], f2_ref[1]), _spec_slab(scr, kt * kb + k))
            _store_complex(o_ref, k, x * scale)


def _hyena_spectrum(buf, sumabs, consts, n, kb=8):
    _, nh, rows, n2, lanes = buf.shape
    nt = FFT_TILES
    const = lambda a: pl.BlockSpec(a.shape, lambda o, s: (0,) * a.ndim)
    return pl.pallas_call(
        functools.partial(_hspec_body, kb=kb, n=n),
        grid=(2, 2 * nt),
        in_specs=[
            pl.BlockSpec((None, nh, rows, FFT_NB, lanes), lambda o, s: (o, 0, 0, jnp.minimum(s, nt - 1), 0)),
            const(sumabs), const(consts["f1_real"]), const(consts["f2"]),
            pl.BlockSpec((1, 2, FFT_N1, FFT_NB), lambda o, s: (jnp.minimum(s, nt - 1), 0, 0, 0)),
        ],
        out_specs=pl.BlockSpec((None, nh, 2, kb, FFT_N1, lanes), lambda o, s: (o, 0, 0, jnp.maximum(s - nt, 0), 0, 0)),
        out_shape=jax.ShapeDtypeStruct((2, nh, 2, FFT_N1, FFT_N1, lanes), BF16),
        scratch_shapes=[pltpu.VMEM((nh, nt * TILE_ROWS, lanes), F32)],
        compiler_params=_cparams(("arbitrary", "arbitrary"), big=True),
        name="hyena_spectrum",
    )(buf, sumabs, consts["f1_real"], consts["f2"], consts["twid"])


def _hyc_body(v_ref, x1_ref, x2_ref, buf_ref, sa_ref, cs_ref, bias_ref, o_ref):
    n = 2 * CTX_LEN
    cmat, smat = cs_ref[0], cs_ref[1]
    c_in, s_in = cmat[:, :CTX_LEN], smat[:, :CTX_LEN]
    c_out, s_out = cmat[:CTX_LEN, :], smat[:CTX_LEN, :]
    batch = lambda ref, b: jnp.concatenate([ref[hf, b] for hf in range(HY_HALVES)], axis=1)
    yr, yi = batch(v_ref, 0), batch(v_ref, 1)
    for o, gate in enumerate((x1_ref, x2_ref)):
        bufo = jnp.concatenate([buf_ref[o, hf] for hf in range(HY_HALVES)], axis=1)
        scale = 1.0 / (sa_ref[0:1, o * HY_WIDTH:(o + 1) * HY_WIDTH] * float(n))
        hr = _dot3(cmat, bufo) * scale
        hi = -_dot3(smat, bufo) * scale
        xr = _dot3(c_in, yr) + _dot3(s_in, yi)
        xi = _dot3(c_in, yi) - _dot3(s_in, yr)
        zr = xr * hr - xi * hi
        zi = xr * hi + xi * hr
        wr = _dot3(c_out, zr) - _dot3(s_out, zi)
        wi = _dot3(c_out, zi) + _dot3(s_out, zr)
        b = bias_ref[o:o + 1, :]
        yr = batch(gate, 0) * (wr + yr * b)
        yi = batch(gate, 1) * (wi + yi * b)
    for hf in range(HY_HALVES):
        o_ref[hf, 0] = yr[:, hf * LANES:(hf + 1) * LANES]
        o_ref[hf, 1] = yi[:, hf * LANES:(hf + 1) * LANES]


def _hy_ctx(v, x1, x2, buf, sumabs, cs, bias):
    return pl.pallas_call(
        _hyc_body,
        out_shape=jax.ShapeDtypeStruct(v.shape, F32),
        compiler_params=pltpu.CompilerParams(vmem_limit_bytes=VMEM_LIMIT),
        name="hyena_ctx",
    )(v, x1, x2, buf, sumabs, cs, bias)


def _dft_constants(seq_len):
    n = 2 * seq_len
    n1 = FFT_N1
    assert n == n1 * n1
    idx = np.arange(n1, dtype=np.float64)
    th = 2.0 * np.pi * np.outer(idx, idx) / n1
    fr, fi = np.cos(th), -np.sin(th)
    half = n1 // 2

    def parts(m):
        m32 = jnp.asarray(m, F32)
        hi = m32.astype(BF16)
        mid = (m32 - hi.astype(F32)).astype(BF16)
        return jnp.stack([hi, mid])

    f1_data = np.block([[fr[:, :half], -fi[:, :half]], [fi[:, :half], fr[:, :half]]])
    f1_real = np.concatenate([fr, fi], axis=0)
    f2 = np.block([[fr, -fi], [fi, fr]])
    f2c = np.block([[fr, fi], [-fi, fr]])
    f3 = np.block([[fr[:half], fi[:half]], [-fi[:half], fr[:half]]])
    tw = 2.0 * np.pi * np.outer(idx, idx) / n
    twid = np.stack([np.cos(tw), -np.sin(tw)])
    twid = twid.reshape(2, n1, n1 // FFT_NB, FFT_NB).transpose(2, 0, 1, 3)
    return dict(f1_data=parts(f1_data), f1_real=parts(f1_real), f2=parts(f2), f2c=parts(f2c),
                f3=parts(f3), twid=jnp.asarray(twid, F32))


def _ctx_dft_constants():
    n = 2 * CTX_LEN
    idx = np.arange(n, dtype=np.float64)
    th = 2.0 * np.pi * np.outer(idx, idx) / n
    return jnp.asarray(np.stack([np.cos(th), np.sin(th)]), F32)


def _hyena_filter_params(hy_w1, hy_b1, hy_freq1, hy_w2, hy_b2, hy_freq2, hy_w3):
    hid = HY_HIDDEN
    zeros = jnp.zeros((hid, hid), F32)
    w1 = jnp.zeros((hid, hid), F32).at[:hy_w1.shape[0]].set(hy_w1)
    w1 = jnp.block([[w1, zeros], [zeros, w1]])
    w2 = jnp.block([[hy_w2, zeros], [zeros, hy_w2]])
    w3 = hy_w3.reshape(hid, 2, 2 * HY_WIDTH).transpose(1, 0, 2)
    z3 = jnp.zeros_like(w3)
    w3 = jnp.stack([jnp.concatenate([w3, z3], axis=1), jnp.concatenate([z3, w3], axis=1)], axis=1)
    deltas = np.abs(np.linspace(HY_MIN_DECAY, HY_MAX_DECAY, HY_WIDTH))
    delta = jnp.asarray(np.tile(deltas, 2)[None, :], F32)
    row = lambda a: jnp.tile(a, 2).reshape(1, -1)
    return dict(w1=w1, b1=row(hy_b1), f1=row(hy_freq1), w2=w2, b2=row(hy_b2), f2=row(hy_freq2),
                w3=w3, delta=delta)


def _hyena_x(v, x1, x2, hspec, bias, l, consts):
    nh, bsz, seq, lanes = v.shape
    assert bsz == 2
    rows = bsz * (seq // FFT_N1)
    view = lambda a: a.reshape(nh, rows, FFT_N1, lanes)
    y = view(v)
    for o, gate in enumerate((x1, x2)):
        y = _hyena_conv(y, view(gate), hspec, bias, l, o, consts)
    return y.reshape(nh, bsz, seq, lanes)


def _out_body(of, ob, ug, hy, yf, yb, zz, hgn, ssn, w_ref, h_ref, gate_ref, o_ref):
    o = of[0].astype(F32) + ob[0].astype(F32)
    gi = lax.broadcasted_iota(jnp.int32, (HG_WIDTH, HG_WIDTH), 0) // HG_DK
    gj = lax.broadcasted_iota(jnp.int32, (HG_WIDTH, HG_WIDTH), 1) // HG_DK
    avg = jnp.where(gi == gj, 1.0 / HG_DK, 0.0).astype(BF16)
    sq = o * o
    sh = sq.astype(BF16)
    sl = (sq - sh.astype(F32)).astype(BF16)
    ms = _dot(sh, avg) + _dot(sl, avg)
    hg = o * lax.rsqrt(ms + EPS) * hgn[...] * _silu(ug[0])
    y = (yf[0].astype(F32) + yb[0].astype(F32)) * _silu(zz[0])
    gw = SSD_WIDTH // SSD_GROUPS
    parts = []
    for g in range(SSD_GROUPS):
        yg = y[:, g * gw:(g + 1) * gw]
        parts.append(yg * lax.rsqrt(jnp.mean(yg * yg, axis=-1, keepdims=True) + EPS))
    ys = jnp.concatenate(parts, axis=1) * ssn[...]
    acc = _dot(hg.astype(BF16), w_ref[0:HG_WIDTH, :])
    hyv = jnp.concatenate([hy[hf, 0] for hf in range(HY_HALVES)], axis=1)
    acc = acc + _dot(hyv.astype(BF16), w_ref[HG_WIDTH:HG_WIDTH + HY_WIDTH, :])
    acc = acc + _dot(ys.astype(BF16), w_ref[HG_WIDTH + HY_WIDTH:, :])
    o_ref[0] = h_ref[0] + gate_ref[...] * acc


def _out_proj(of, ob, u_hg, hy, yf, yb, zgate, hgn, ssn, w, h, mods, l, ctx_row, tm):
    bsz, seq, d = h.shape
    tok = lambda n, col=0: pl.BlockSpec((1, tm, n), lambda b, t: (b, t, col))
    return pl.pallas_call(
        _out_body,
        grid=(bsz, seq // tm),
        in_specs=[tok(HG_WIDTH), tok(HG_WIDTH), tok(HG_WIDTH, 4),
                  pl.BlockSpec((HY_HALVES, 1, tm, LANES), lambda b, t: (0, b, t, 0)),
                  tok(SSD_WIDTH), tok(SSD_WIDTH), tok(SSD_WIDTH),
                  _layer_spec(hgn, l), _layer_spec(ssn, l), _layer_spec(w, l),
                  tok(d),
                  _mod_spec(l, ctx_row, MOD_GATE1)],
        out_specs=tok(d),
        out_shape=jax.ShapeDtypeStruct(h.shape, F32),
        compiler_params=_cparams(("parallel", "arbitrary"), big=True),
        name="mixer_out_proj",
    )(of, ob, u_hg, hy, yf, yb, zgate, hgn, ssn, w, h, mods)


def _ffn_body(*refs, tm, grid_w, vertical, final, cchunk):
    if vertical:
        x_ref, hp, hn, sh_ref, sc_ref, g_ref, wg, wu, cw, cb, wd, gate_ref = refs[:12]
        rest = refs[12:]
    else:
        x_ref, sh_ref, sc_ref, g_ref, wg, wu, cw, cb, wd, gate_ref = refs[:10]
        rest = refs[10:]
    if final:
        fg_ref, o_ref, act_ref = rest
    else:
        o_ref, act_ref = rest
    t = pl.program_id(1)
    nt = pl.num_programs(1)

    def normmod(v):
        ms = jnp.mean(v * v, axis=-1, keepdims=True)
        y = v * lax.rsqrt(ms + EPS) * g_ref[...]
        return (y * (1.0 + sc_ref[...]) + sh_ref[...]).astype(BF16)

    x = x_ref[0]
    yb = normmod(x)
    pad = grid_w if vertical else 0
    ext_rows = tm + 2 * pad
    ye = jnp.concatenate([normmod(hp[0]), yb, normmod(hn[0])], axis=0) if vertical else yb
    col = lax.broadcasted_iota(jnp.int32, (ext_rows, cchunk), 0) % grid_w
    for j in range(0, D_FF, cchunk):
        cs = slice(j, j + cchunk)
        ext = _dot(ye, wg[:, cs])
        if vertical:
            ext = jnp.concatenate([jnp.where(t > 0, ext[:pad], 0.0), ext[pad:pad + tm],
                                   jnp.where(t < nt - 1, ext[pad + tm:], 0.0)], axis=0)
        left = jnp.where(col != 0, pltpu.roll(ext, 1, 0), 0.0)
        right = jnp.where(col != grid_w - 1, pltpu.roll(ext, ext_rows - 1, 0), 0.0)
        acc = jnp.zeros((tm, cchunk), F32) + cb[:, cs]
        for dy in ((-1, 0, 1) if vertical else (0,)):
            r0 = pad + dy * grid_w
            ky = dy + 1
            acc = acc + left[r0:r0 + tm] * cw[3 * ky:3 * ky + 1, cs]
            acc = acc + ext[r0:r0 + tm] * cw[3 * ky + 1:3 * ky + 2, cs]
            acc = acc + right[r0:r0 + tm] * cw[3 * ky + 2:3 * ky + 3, cs]
        act_ref[:, cs] = (_silu(acc) * _dot(yb, wu[:, cs])).astype(BF16)
    y = _dot(act_ref[...], wd[...])
    x = x + gate_ref[...] * y
    if final:
        ms = jnp.mean(x * x, axis=-1, keepdims=True)
        x = x * lax.rsqrt(ms + EPS) * fg_ref[...]
    o_ref[0] = x


def _ffn(h, mods, l, ctx_row, g, wg, wu, cw, cb, wd, tm, grid_w, vertical, final_g=None):
    bsz, seq, d = h.shape
    nhb = tm // grid_w
    tok = lambda n: pl.BlockSpec((1, tm, n), lambda b, t: (b, t, 0))
    full = lambda a: _layer_spec(a, l)
    resident = lambda a: pl.BlockSpec((None,) + tuple(a.shape[1:]), lambda b, t: (l, 0, 0),
                                      pipeline_mode=pl.Buffered(1))
    in_specs = [tok(d)]
    args = [h]
    if vertical:
        nrow = seq // grid_w
        in_specs += [
            pl.BlockSpec((1, grid_w, d), lambda b, t: (b, jnp.maximum(t * nhb - 1, 0), 0)),
            pl.BlockSpec((1, grid_w, d), lambda b, t: (b, jnp.minimum((t + 1) * nhb, nrow - 1), 0)),
        ]
        args += [h, h]
    in_specs += [_mod_spec(l, ctx_row, MOD_SHIFT2), _mod_spec(l, ctx_row, MOD_SCALE2), full(g),
                 resident(wg), resident(wu), full(cw), full(cb), resident(wd),
                 _mod_spec(l, ctx_row, MOD_GATE2)]
    args += [mods, mods, g, wg, wu, cw, cb, wd, mods]
    if final_g is not None:
        in_specs.append(pl.BlockSpec((1, d), lambda b, t: (0, 0)))
        args.append(final_g.reshape(1, d))
    return pl.pallas_call(
        functools.partial(_ffn_body, tm=tm, grid_w=grid_w, vertical=vertical,
                          final=final_g is not None, cchunk=256),
        grid=(bsz, seq // tm),
        in_specs=in_specs,
        out_specs=tok(d),
        out_shape=jax.ShapeDtypeStruct(h.shape, F32),
        scratch_shapes=[pltpu.VMEM((tm, D_FF), BF16)],
        compiler_params=_cparams(("parallel", "arbitrary"), big=True),
        name="ffn",
    )(*args)


def kernel(x, c, ctx, c_ctx, w_ada, b_ada, norm1_g, norm2_g, w_in, w_out, hg_lb_logits, hg_norm_g, hy_conv_w, hy_conv_b, hy_w1, hy_b1, hy_freq1, hy_w2, hy_b2, hy_freq2, hy_w3, hy_bias, ssd_conv_w, ssd_conv_b, ssd_dt_bias, ssd_a_log, ssd_d, ssd_norm_g, ffn_w_gate, ffn_w_up, ffn_conv_w, ffn_conv_b, ffn_w_down, final_norm_g):
    bsz, seq, d = x.shape
    depth = w_in.shape[0]
    cc = jnp.zeros((8, d), F32).at[:bsz].set(c).at[bsz].set(c_ctx)
    mods = _ada(cc, w_ada, b_ada).reshape(depth, 8, 6, 1, d)
    ctx_row = bsz

    consts = _dft_constants(seq)
    cs_ctx = _ctx_dft_constants()
    lb_logits = hg_lb_logits.astype(F32)

    row3 = lambda a: a.reshape(depth, 1, -1)
    rep = lambda a: jnp.repeat(a, SSD_HEAD_DIM, axis=-1)
    w_in_b = jnp.pad(w_in, ((0, 0), (0, 0), (0, IN_COLS_PAD - w_in.shape[-1]))).astype(BF16)
    w_out_b = w_out.astype(BF16)
    wg_b, wu_b, wd_b = ffn_w_gate.astype(BF16), ffn_w_up.astype(BF16), ffn_w_down.astype(BF16)
    g1, g2 = row3(norm1_g), row3(norm2_g)
    hcb, scb = row3(hy_conv_b), row3(ssd_conv_b)
    hgn, ssn = row3(hg_norm_g), row3(ssd_norm_g)
    cw_f, cb_f = ffn_conv_w.reshape(depth, 9, D_FF), row3(ffn_conv_b)
    prm = dict(dtb_e=rep(ssd_dt_bias), alog_e=rep(ssd_a_log),
               dtb_t=ssd_dt_bias.reshape(depth, -1, 1), alog_t=ssd_a_log.reshape(depth, -1, 1),
               dsk=row3(rep(ssd_d)))

    for l in range(depth):
        last = l == depth - 1
        proj = lambda h, row, tm, name: _inproj(h, mods, l, row, g1, w_in_b, hy_conv_w, hcb, ssd_conv_w, scb,
                                                tm, name)
        uhg_x, v_x, x1_x, x2_x, xbc_x, z_x, dt_x = proj(x, None, 1024, "in_proj_x")
        uhg_c, v_c, x1_c, x2_c, xbc_c, z_c, dt_c = proj(ctx, ctx_row, CTX_LEN, "in_proj_ctx")

        of_x, ob_x, of_c, ob_c = _gla(uhg_x, uhg_c, lb_logits, l)

        dtt_x = jnp.swapaxes(dt_x[:, :, DT_LANE0:DT_LANE0 + 2 * SSD_HEADS], 1, 2)
        dtt_c = jnp.swapaxes(dt_c[:, :, DT_LANE0:DT_LANE0 + 2 * SSD_HEADS], 1, 2)
        yf_x, yb_x, yf_c, yb_c = _ssd(xbc_x, xbc_c, dt_x, dt_c, dtt_x, dtt_c, prm, l)

        fp = _hyena_filter_params(hy_w1[l], hy_b1[l], hy_freq1[l], hy_w2[l], hy_b2[l], hy_freq2[l], hy_w3[l])
        buf, sumabs = _filt_gen(fp, seq, 1024)
        hspec = _hyena_spectrum(buf.reshape(2, HY_HALVES, FFT_N1, FFT_N1, LANES), sumabs, consts, 2 * seq)
        ohy_x = _hyena_x(v_x, x1_x, x2_x, hspec, hy_bias.reshape(depth, 2, 1, HY_WIDTH), l, consts)

        x = _out_proj(of_x, ob_x, uhg_x, ohy_x, yf_x, yb_x, z_x, hgn, ssn, w_out_b, x, mods, l, None, 512)

        if not last:
            buf_c, sumabs_c = _filt_gen(fp, CTX_LEN, CTX_LEN)
            ohy_c = _hy_ctx(v_c, x1_c, x2_c, buf_c, sumabs_c, cs_ctx, hy_bias[l])
            ctx = _out_proj(of_c, ob_c, uhg_c, ohy_c, yf_c, yb_c, z_c, hgn, ssn, w_out_b, ctx, mods, l, ctx_row,
                            CTX_LEN)
            ctx = _ffn(ctx, mods, l, ctx_row, g2, wg_b, wu_b, cw_f, cb_f, wd_b, CTX_LEN, CTX_LEN, False)

        x = _ffn(x, mods, l, None, g2, wg_b, wu_b, cw_f, cb_f, wd_b, 1024, GRID_W, True,
                 final_g=final_norm_g if last else None)
    return x
```
